```python
import jax, jax.numpy as jnp
from jax import lax
import numpy as np

D_MODEL = 1024
BATCH = 8
SEQ = 2048
DEPTH = 1

MIX_DIM = D_MODEL
HEAD_DIM = 64
ATTN_DIM = MIX_DIM // 2
N_ATTN_HEADS = ATTN_DIM // HEAD_DIM
POOL_DIM = MIX_DIM - ATTN_DIM
POOL_WINDOWS = (2, 4, 8, 16)
N_POOL_GROUPS = len(POOL_WINDOWS)
POOL_GROUP_DIM = POOL_DIM // N_POOL_GROUPS
Q_BLOCK = 128
IN_DIM = POOL_DIM + 3 * ATTN_DIM + N_ATTN_HEADS
N_EXPERT_GROUPS = 4
EXPERTS_PER_GROUP = 8
TOP_K = 2
D_EXPERT = D_MODEL // 2
PLE_DIM = 256
EPS = 1e-6

kernel_name = "hybrid_pool_fox_hmoe_ple"


def rms_norm(x, g):
    xf = x.astype(jnp.float32)
    y = xf * lax.rsqrt(jnp.mean(xf * xf, axis=-1, keepdims=True) + EPS)
    return (y * g.astype(jnp.float32)).astype(x.dtype)


def causal_pool_mixer(u, w_pool, s_pool):
    B, S, _ = u.shape
    ug = u.reshape(B, S, N_POOL_GROUPS, POOL_GROUP_DIM)
    cs = jnp.cumsum(ug.astype(jnp.float32), axis=1)
    pos = jnp.arange(1, S + 1, dtype=jnp.float32)
    means = []
    for gi, w in enumerate(POOL_WINDOWS):
        c = cs[:, :, gi]
        lag = jnp.pad(c, ((0, 0), (w, 0), (0, 0)))[:, :S]
        cnt = jnp.minimum(pos, jnp.float32(w))[None, :, None]
        means.append((c - lag) / cnt)
    mean = jnp.stack(means, axis=2)
    d = (mean - ug.astype(jnp.float32)).astype(u.dtype)
    y = jnp.einsum('bsgc,gcd->bsgd', d, w_pool)
    return y.reshape(B, S, POOL_DIM) * s_pool


def forgetting_attention(q, k, v, log_f):
    B, S, H, Dh = q.shape
    c = jnp.cumsum(log_f.astype(jnp.float32), axis=1).transpose(0, 2, 1)
    qf = q.astype(jnp.float32).transpose(0, 2, 1, 3) * (Dh ** -0.5)
    kf = k.astype(jnp.float32).transpose(0, 2, 1, 3)
    vt = v.transpose(0, 2, 1, 3)
    tri = jnp.tril(jnp.ones((Q_BLOCK, Q_BLOCK), dtype=bool))
    outs = []
    for i in range(S // Q_BLOCK):
        q0, q1 = i * Q_BLOCK, (i + 1) * Q_BLOCK
        s = jnp.einsum('bhqd,bhkd->bhqk', qf[:, :, q0:q1], kf[:, :, :q1])
        s = s + (c[:, :, q0:q1, None] - c[:, :, None, :q1])
        mask = jnp.concatenate([jnp.ones((Q_BLOCK, q0), dtype=bool), tri], axis=1)
        s = jnp.where(mask, s, -jnp.inf)
        pr = jax.nn.softmax(s, axis=-1)
        outs.append(jnp.einsum('bhqk,bhkd->bhqd', pr.astype(v.dtype), vt[:, :, :q1]))
    o = jnp.concatenate(outs, axis=2)
    return o.transpose(0, 2, 1, 3).reshape(B, S, H * Dh)


def hierarchical_moe(m, w_grp, b_grp, w_rt, b_rt, w_e_gate, w_e_up, w_e_down):
    B, S, D = m.shape
    G, E = N_EXPERT_GROUPS, EXPERTS_PER_GROUP
    t = m.reshape(-1, D)
    grp_prob = jax.nn.softmax((t @ w_grp + b_grp).astype(jnp.float32), axis=-1)
    g_idx = jnp.argmax(grp_prob, axis=-1)
    g_w = jnp.max(grp_prob, axis=-1)
    ex_logits = (t @ w_rt + b_rt).astype(jnp.float32).reshape(-1, G, E)
    sel = jnp.take_along_axis(ex_logits, g_idx[:, None, None], axis=1)[:, 0]
    top_v, top_i = lax.top_k(sel, TOP_K)
    top_w = jax.nn.softmax(top_v, axis=-1) * g_w[:, None]
    e_comb = jnp.sum(jax.nn.one_hot(top_i, E, dtype=jnp.float32) * top_w[..., None], axis=1)
    comb = (jax.nn.one_hot(g_idx, G, dtype=jnp.float32)[:, :, None] * e_comb[:, None, :]).astype(t.dtype)
    y = jnp.zeros_like(t)
    for g in range(G):
        hg = jnp.einsum('nd,edf->nef', t, w_e_gate[g])
        hu = jnp.einsum('nd,edf->nef', t, w_e_up[g])
        a = jax.nn.silu(hg) * hu * comb[:, g, :, None]
        y = y + jnp.einsum('nef,efd->nd', a, w_e_down[g])
    return y.reshape(B, S, D)


def setup_inputs(seed: int = 0) -> dict:
    key = jax.random.key(seed)
    ks = jax.random.split(key, 24)
    L, D, H = DEPTH, D_MODEL, N_ATTN_HEADS
    G, E, F = N_EXPERT_GROUPS, EXPERTS_PER_GROUP, D_EXPERT
    nrm = lambda k, shape, fan_in: jax.random.normal(k, shape, jnp.float32) * (fan_in ** -0.5)
    gain = lambda k, shape: 1.0 + 0.02 * jax.random.normal(k, shape, jnp.float32)
    return {
        "x": jax.random.normal(ks[0], (BATCH, SEQ, D), jnp.float32),
        "p": jax.random.normal(ks[1], (L, BATCH, SEQ, PLE_DIM), jnp.float32),
        "g_mix": gain(ks[2], (L, D)),
        "w_in": nrm(ks[3], (L, D, IN_DIM), D),
        "b_f": jnp.linspace(1.0, 4.0, H, dtype=jnp.float32)[None, :] + 0.1 * jax.random.normal(ks[4], (L, H), jnp.float32),
        "w_pool": nrm(ks[5], (L, N_POOL_GROUPS, POOL_GROUP_DIM, POOL_GROUP_DIM), POOL_GROUP_DIM),
        "s_pool": gain(ks[6], (L, POOL_DIM)),
        "w_out": nrm(ks[7], (L, MIX_DIM, D), MIX_DIM),
        "g_ffn": gain(ks[8], (L, D)),
        "w_grp": nrm(ks[9], (L, D, G), D),
        "b_grp": 0.01 * jax.random.normal(ks[10], (L, G), jnp.float32),
        "w_rt": nrm(ks[11], (L, D, G * E), D),
        "b_rt": 0.01 * jax.random.normal(ks[12], (L, G * E), jnp.float32),
        "w_e_gate": nrm(ks[13], (L, G, E, D, F), D),
        "w_e_up": nrm(ks[14], (L, G, E, D, F), D),
        "w_e_down": nrm(ks[15], (L, G, E, F, D), F),
        "g_ple": gain(ks[16], (L, D)),
        "w_ple_gate": nrm(ks[17], (L, D, D), D),
        "w_ple_proj": nrm(ks[18], (L, PLE_DIM, D), PLE_DIM),
        "g_final": gain(ks[19], (D,)),
    }


def reference(x, p, g_mix, w_in, b_f, w_pool, s_pool, w_out, g_ffn, w_grp, b_grp,
              w_rt, b_rt, w_e_gate, w_e_up, w_e_down, g_ple, w_ple_gate, w_ple_proj, g_final):
    B, S, D = x.shape
    H = N_ATTN_HEADS
    h = x
    for i in range(DEPTH):
        a = rms_norm(h, g_mix[i])
        z = a @ w_in[i]
        o0 = POOL_DIM
        u_pool = z[..., :o0]
        q = z[..., o0:o0 + ATTN_DIM].reshape(B, S, H, HEAD_DIM)
        k = z[..., o0 + ATTN_DIM:o0 + 2 * ATTN_DIM].reshape(B, S, H, HEAD_DIM)
        v = z[..., o0 + 2 * ATTN_DIM:o0 + 3 * ATTN_DIM].reshape(B, S, H, HEAD_DIM)
        f_logit = z[..., o0 + 3 * ATTN_DIM:] + b_f[i]
        log_f = jax.nn.log_sigmoid(f_logit.astype(jnp.float32))
        pool_out = causal_pool_mixer(u_pool, w_pool[i], s_pool[i])
        attn_out = forgetting_attention(q, k, v, log_f).astype(h.dtype)
        h = h + jnp.concatenate([pool_out, attn_out], axis=-1) @ w_out[i]
        h = h + hierarchical_moe(rms_norm(h, g_ffn[i]), w_grp[i], b_grp[i], w_rt[i], b_rt[i],
                                 w_e_gate[i], w_e_up[i], w_e_down[i])
        gate = jax.nn.sigmoid(rms_norm(h, g_ple[i]) @ w_ple_gate[i])
        h = h + gate * (p[i] @ w_ple_proj[i])
    return rms_norm(h, g_final)
```

```python
import functools

import jax
import jax.numpy as jnp
from jax import lax
from jax.experimental import pallas as pl
from jax.experimental.pallas import tpu as pltpu

HEAD_DIM = 64
N_HEADS = 8
POOL_WINDOWS = (2, 4, 8, 16)
POOL_GROUP_DIM = 128
POOL_HISTORY = 16
N_GROUPS = 4
EXPERTS_PER_GROUP = 8
N_EXPERTS = N_GROUPS * EXPERTS_PER_GROUP
EPS = 1e-6
LANES = 128
ROUTER_LANE0 = N_GROUPS
NEG_INF = float("-inf")

ROW_TILE = 512
ATTN_Q = 256
ATTN_K = 256
MOE_TILE = 1024
VMEM_LIMIT = 48 * 1024 * 1024

BF16 = jnp.bfloat16
F32 = jnp.float32


def _dot(a, b):
    return jnp.dot(a, b, preferred_element_type=F32)


def _split2(a):
    hi = a.astype(BF16)
    lo = (a - hi.astype(F32)).astype(BF16)
    return hi, lo


def _split3(a):
    hi = a.astype(BF16)
    r = a - hi.astype(F32)
    mid = r.astype(BF16)
    lo = (r - mid.astype(F32)).astype(BF16)
    return hi, mid, lo


def _dot_precise(a, w):
    a1, a2 = _split2(a)
    w1, w2 = _split2(w)
    return _dot(a1, w1) + (_dot(a1, w2) + _dot(a2, w1))


def _rms_norm(x, g):
    return x * lax.rsqrt(jnp.mean(x * x, axis=-1, keepdims=True) + EPS) * g


def _inproj_kernel(tiles_per_seq, x_ref, g_ref, w_ref, wf_ref, bf_ref, wp_ref, sp_ref,
                   pool_ref, q_ref, k_ref, v_ref, c_ref, carry_c, carry_u):
    i = pl.program_id(0)
    seq_tile = i % tiles_per_seq
    tm = x_ref.shape[0]
    pool_dim = pool_ref.shape[1]
    attn_dim = q_ref.shape[1]

    @pl.when(seq_tile == 0)
    def _():
        carry_c[...] = jnp.zeros_like(carry_c)
        carry_u[...] = jnp.zeros_like(carry_u)

    a = _rms_norm(x_ref[...], g_ref[...])
    ab = a.astype(BF16)
    o0 = pool_dim
    u = _dot(ab, w_ref[:, 0:o0])
    q_ref[...] = (_dot(ab, w_ref[:, o0:o0 + attn_dim]) * (HEAD_DIM ** -0.5)).astype(BF16)
    k_ref[...] = _dot(ab, w_ref[:, o0 + attn_dim:o0 + 2 * attn_dim]).astype(BF16)
    v_ref[...] = _dot(ab, w_ref[:, o0 + 2 * attn_dim:o0 + 3 * attn_dim]).astype(BF16)

    fl = _dot_precise(a, wf_ref[...]) + bf_ref[...]
    lf = jnp.minimum(fl, 0.0) - jnp.log1p(jnp.exp(-jnp.abs(fl)))
    row = lax.broadcasted_iota(jnp.int32, (tm, tm), 0)
    col = lax.broadcasted_iota(jnp.int32, (tm, tm), 1)
    tril = (col <= row).astype(BF16)
    l1, l2, l3 = _split3(lf)
    c = carry_c[...] + (_dot(tril, l1) + (_dot(tril, l2) + _dot(tril, l3)))
    c_ref[...] = c
    carry_c[...] = c[tm - 1:tm, :]

    ext = jnp.concatenate([carry_u[...], u], axis=0)
    carry_u[...] = u[tm - POOL_HISTORY:, :]
    pos = (seq_tile * tm + 1 + lax.broadcasted_iota(jnp.int32, (tm, 1), 0)).astype(F32)
    for gi, w in enumerate(POOL_WINDOWS):
        lo, hi = gi * POOL_GROUP_DIM, (gi + 1) * POOL_GROUP_DIM
        s = ext[:, lo:hi]
        shift = 1
        while shift < w:
            s = s + pltpu.roll(s, shift, axis=0)
            shift *= 2
        mean = s[POOL_HISTORY:, :] / jnp.minimum(pos, float(w))
        d = mean - u[:, lo:hi]
        y = _dot(d.astype(BF16), wp_ref[gi]) * sp_ref[:, lo:hi]
        pool_ref[:, lo:hi] = y.astype(BF16)


def _inproj(x2, g_mix, w_main, w_f, b_f, w_pool, s_pool, seq_len):
    n, d = x2.shape
    tm = ROW_TILE
    pool_dim = w_pool.shape[0] * w_pool.shape[1]
    attn_dim = (w_main.shape[1] - pool_dim) // 3
    row = lambda i: (i, 0)
    full2 = lambda i: (0, 0)
    return pl.pallas_call(
        functools.partial(_inproj_kernel, seq_len // tm),
        grid=(n // tm,),
        in_specs=[
            pl.BlockSpec((tm, d), row),
            pl.BlockSpec((1, d), full2),
            pl.BlockSpec(w_main.shape, full2),
            pl.BlockSpec(w_f.shape, full2),
            pl.BlockSpec((1, LANES), full2),
            pl.BlockSpec(w_pool.shape, lambda i: (0, 0, 0)),
            pl.BlockSpec((1, pool_dim), full2),
        ],
        out_specs=[
            pl.BlockSpec((tm, pool_dim), row),
            pl.BlockSpec((tm, attn_dim), row),
            pl.BlockSpec((tm, attn_dim), row),
            pl.BlockSpec((tm, attn_dim), row),
            pl.BlockSpec((tm, LANES), row),
        ],
        out_shape=[
            jax.ShapeDtypeStruct((n, pool_dim), BF16),
            jax.ShapeDtypeStruct((n, attn_dim), BF16),
            jax.ShapeDtypeStruct((n, attn_dim), BF16),
            jax.ShapeDtypeStruct((n, attn_dim), BF16),
            jax.ShapeDtypeStruct((n, LANES), F32),
        ],
        scratch_shapes=[pltpu.VMEM((1, LANES), F32), pltpu.VMEM((POOL_HISTORY, pool_dim), F32)],
        compiler_params=pltpu.CompilerParams(
            dimension_semantics=("arbitrary",), vmem_limit_bytes=VMEM_LIMIT),
        name="inproj",
    )(x2, g_mix, w_main, w_f, b_f, w_pool, s_pool)


def _attn_kernel(q_ref, k_ref, v_ref, ccol_ref, crow_ref, o_ref):
    hp = pl.program_id(1)
    qi = pl.program_id(2)
    tq = q_ref.shape[0]
    tk = ATTN_K
    q = q_ref[...]
    ccol = ccol_ref[...]
    lane = lax.broadcasted_iota(jnp.int32, (1, LANES), 1)
    row_g = qi * tq + lax.broadcasted_iota(jnp.int32, (tq, tk), 0)
    col_l = lax.broadcasted_iota(jnp.int32, (tq, tk), 1)
    out = jnp.zeros((tq, LANES), F32)
    for hh in range(2):
        half = (lane >= hh * HEAD_DIM) & (lane < (hh + 1) * HEAD_DIM)
        qh = jnp.where(half, q, jnp.zeros_like(q))
        ct = jnp.sum(jnp.where(lane == hp * 2 + hh, ccol, 0.0), axis=-1, keepdims=True)

        def body(j, carry):
            m, l, acc = carry
            off = pl.multiple_of(j * tk, tk)
            kb = k_ref[pl.ds(off, tk), :]
            vb = v_ref[pl.ds(off, tk), :]
            vh = jnp.where(half, vb, jnp.zeros_like(vb))
            s = lax.dot_general(qh, kb, (((1,), (1,)), ((), ())), preferred_element_type=F32)
            cs = crow_ref[0, 0, hh:hh + 1, pl.ds(off, tk)]
            s = s + (ct - cs)
            s = jnp.where(col_l + off <= row_g, s, NEG_INF)
            m_new = jnp.maximum(m, jnp.max(s, axis=-1, keepdims=True))
            alpha = jnp.exp(m - m_new)
            p = jnp.exp(s - m_new)
            l = alpha * l + jnp.sum(p, axis=-1, keepdims=True)
            acc = alpha * acc + _dot(p.astype(BF16), vh)
            return m_new, l, acc

        init = (jnp.full((tq, 1), NEG_INF, F32), jnp.zeros((tq, 1), F32),
                jnp.zeros((tq, LANES), F32))
        n_kv = (qi * tq + tq + tk - 1) // tk
        _, l, acc = lax.fori_loop(0, n_kv, body, init)
        out = out + acc / l
    o_ref[...] = out.astype(BF16)


def _attention(q, k, v, c_col, c_row, batch, seq_len):
    n, attn_dim = q.shape
    pairs = attn_dim // LANES
    tq = ATTN_Q
    qt = seq_len // tq
    return pl.pallas_call(
        _attn_kernel,
        grid=(batch, pairs, qt),
        in_specs=[
            pl.BlockSpec((tq, LANES), lambda b, hp, qi: (b * qt + qi, hp)),
            pl.BlockSpec((seq_len, LANES), lambda b, hp, qi: (b, hp)),
            pl.BlockSpec((seq_len, LANES), lambda b, hp, qi: (b, hp)),
            pl.BlockSpec((tq, LANES), lambda b, hp, qi: (b * qt + qi, 0)),
            pl.BlockSpec((1, 1, 2, seq_len), lambda b, hp, qi: (b, hp, 0, 0)),
        ],
        out_specs=pl.BlockSpec((tq, LANES), lambda b, hp, qi: (b * qt + qi, hp)),
        out_shape=jax.ShapeDtypeStruct((n, attn_dim), BF16),
        compiler_params=pltpu.CompilerParams(
            dimension_semantics=("arbitrary", "arbitrary", "arbitrary"),
            vmem_limit_bytes=VMEM_LIMIT),
        name="attn",
    )(q, k, v, c_col, c_row)


def _route(logits):
    lane = lax.broadcasted_iota(jnp.int32, logits.shape, 1)
    big = jnp.int32(LANES)
    gl = jnp.where(lane < N_GROUPS, logits, NEG_INF)
    gmax = jnp.max(gl, axis=-1, keepdims=True)
    g_w = 1.0 / jnp.sum(jnp.exp(gl - gmax), axis=-1, keepdims=True)
    g_idx = jnp.min(jnp.where(gl == gmax, lane, big), axis=-1, keepdims=True)
    e_lo = ROUTER_LANE0 + EXPERTS_PER_GROUP * g_idx
    el = jnp.where((lane >= e_lo) & (lane < e_lo + EXPERTS_PER_GROUP), logits, NEG_INF)
    v1 = jnp.max(el, axis=-1, keepdims=True)
    i1 = jnp.min(jnp.where(el == v1, lane, big), axis=-1, keepdims=True)
    el2 = jnp.where(lane == i1, NEG_INF, el)
    v2 = jnp.max(el2, axis=-1, keepdims=True)
    i2 = jnp.min(jnp.where(el2 == v2, lane, big), axis=-1, keepdims=True)
    e2 = jnp.exp(v2 - v1)
    w1 = g_w / (1.0 + e2)
    w2 = g_w * e2 / (1.0 + e2)
    return jnp.where(lane == i1, w1, 0.0) + jnp.where(lane == i2, w2, 0.0)


def _mix_kernel(x_ref, pool_ref, attn_ref, wo_ref, g_ref, wr_ref, br_ref,
                h_ref, m_ref, comb_ref):
    pool_dim = pool_ref.shape[1]
    h = x_ref[...] + (_dot(pool_ref[...], wo_ref[0:pool_dim, :])
                      + _dot(attn_ref[...], wo_ref[pool_dim:, :]))
    h_ref[...] = h
    m = _rms_norm(h, g_ref[...])
    m_ref[...] = m.astype(BF16)
    logits = _dot_precise(m, wr_ref[...]) + br_ref[...]
    comb_ref[...] = _route(logits)


def _mix(x2, pool, attn, w_out, g_ffn, w_router, b_router):
    n, d = x2.shape
    tm = ROW_TILE
    row = lambda i: (i, 0)
    full2 = lambda i: (0, 0)
    return pl.pallas_call(
        _mix_kernel,
        grid=(n // tm,),
        in_specs=[
            pl.BlockSpec((tm, d), row),
            pl.BlockSpec((tm, pool.shape[1]), row),
            pl.BlockSpec((tm, attn.shape[1]), row),
            pl.BlockSpec(w_out.shape, full2),
            pl.BlockSpec((1, d), full2),
            pl.BlockSpec(w_router.shape, full2),
            pl.BlockSpec((1, LANES), full2),
        ],
        out_specs=[
            pl.BlockSpec((tm, d), row),
            pl.BlockSpec((tm, d), row),
            pl.BlockSpec((tm, LANES), row),
        ],
        out_shape=[
            jax.ShapeDtypeStruct((n, d), F32),
            jax.ShapeDtypeStruct((n, d), BF16),
            jax.ShapeDtypeStruct((n, LANES), F32),
        ],
        compiler_params=pltpu.CompilerParams(
            dimension_semantics=("arbitrary",), vmem_limit_bytes=VMEM_LIMIT),
        name="mix",
    )(x2, pool, attn, w_out, g_ffn, w_router, b_router)


def _moe_kernel(h_ref, m_ref, comb_ref, wg_ref, wu_ref, wd_ref, o_ref, acc_ref):
    e = pl.program_id(1)

    @pl.when(e == 0)
    def _():
        acc_ref[...] = jnp.zeros_like(acc_ref)

    x = m_ref[...]
    comb = comb_ref[...]
    lane = lax.broadcasted_iota(jnp.int32, comb.shape, 1)
    cw = jnp.sum(jnp.where(lane == ROUTER_LANE0 + e, comb, 0.0), axis=-1, keepdims=True)
    hg = _dot(x, wg_ref[0])
    hu = _dot(x, wu_ref[0])
    a = hg * jax.nn.sigmoid(hg) * hu * cw
    acc_ref[...] += _dot(a.astype(BF16), wd_ref[0])

    @pl.when(e == pl.num_programs(1) - 1)
    def _():
        o_ref[...] = h_ref[...] + acc_ref[...]


def _moe(h1, m, comb, w_gate, w_up, w_down):
    n, d = h1.shape
    n_exp, _, f = w_gate.shape
    tm = MOE_TILE
    row = lambda i, e: (i, 0)
    return pl.pallas_call(
        _moe_kernel,
        grid=(n // tm, n_exp),
        in_specs=[
            pl.BlockSpec((tm, d), row),
            pl.BlockSpec((tm, d), row),
            pl.BlockSpec((tm, LANES), row),
            pl.BlockSpec((1, d, f), lambda i, e: (e, 0, 0)),
            pl.BlockSpec((1, d, f), lambda i, e: (e, 0, 0)),
            pl.BlockSpec((1, f, d), lambda i, e: (e, 0, 0)),
        ],
        out_specs=pl.BlockSpec((tm, d), row),
        out_shape=jax.ShapeDtypeStruct((n, d), F32),
        scratch_shapes=[pltpu.VMEM((tm, d), F32)],
        compiler_params=pltpu.CompilerParams(
            dimension_semantics=("arbitrary", "arbitrary"), vmem_limit_bytes=VMEM_LIMIT),
        name="moe",
    )(h1, m, comb, w_gate, w_up, w_down)


def _ple_kernel(h_ref, p_ref, g_ref, wg_ref, wp_ref, gf_ref, o_ref):
    h = h_ref[...]
    gate = jax.nn.sigmoid(_dot(_rms_norm(h, g_ref[...]).astype(BF16), wg_ref[...]))
    h = h + gate * _dot(p_ref[...].astype(BF16), wp_ref[...])
    o_ref[...] = _rms_norm(h, gf_ref[...])


def _ple(h2, p2, g_ple, w_gate, w_proj, g_final):
    n, d = h2.shape
    tm = ROW_TILE
    row = lambda i: (i, 0)
    full2 = lambda i: (0, 0)
    return pl.pallas_call(
        _ple_kernel,
        grid=(n // tm,),
        in_specs=[
            pl.BlockSpec((tm, d), row),
            pl.BlockSpec((tm, p2.shape[1]), row),
            pl.BlockSpec((1, d), full2),
            pl.BlockSpec(w_gate.shape, full2),
            pl.BlockSpec(w_proj.shape, full2),
            pl.BlockSpec((1, d), full2),
        ],
        out_specs=pl.BlockSpec((tm, d), row),
        out_shape=jax.ShapeDtypeStruct((n, d), F32),
        compiler_params=pltpu.CompilerParams(
            dimension_semantics=("arbitrary",), vmem_limit_bytes=VMEM_LIMIT),
        name="ple",
    )(h2, p2, g_ple, w_gate, w_proj, g_final)


def _pad_lanes(a):
    return jnp.pad(a, ((0, 0), (0, LANES - a.shape[1])))


def kernel(x, p, g_mix, w_in, b_f, w_pool, s_pool, w_out, g_ffn, w_grp, b_grp, w_rt, b_rt,
           w_e_gate, w_e_up, w_e_down, g_ple, w_ple_gate, w_ple_proj, g_final):
    batch, seq_len, d = x.shape
    n = batch * seq_len
    assert w_in.shape[0] == 1, "single-layer stack only: the final norm is fused into the layer"
    i = 0
    pool_dim = s_pool.shape[1]
    attn_dim = N_HEADS * HEAD_DIM
    main = pool_dim + 3 * attn_dim
    h = x.reshape(n, d)
    w_main = w_in[i, :, :main].astype(BF16)
    w_f = _pad_lanes(w_in[i, :, main:])
    pool, q, k, v, c_col = _inproj(
        h, g_mix[i][None], w_main, w_f, _pad_lanes(b_f[i][None]),
        w_pool[i].astype(BF16), s_pool[i][None], seq_len)
    c_row = (c_col[:, :N_HEADS].reshape(batch, seq_len, N_HEADS // 2, 2)
             .transpose(0, 2, 3, 1))
    attn = _attention(q, k, v, c_col, c_row, batch, seq_len)
    w_router = _pad_lanes(jnp.concatenate([w_grp[i], w_rt[i]], axis=1))
    b_router = _pad_lanes(jnp.concatenate([b_grp[i], b_rt[i]])[None])
    h1, m, comb = _mix(h, pool, attn, w_out[i].astype(BF16), g_ffn[i][None],
                       w_router, b_router)
    f = w_e_gate.shape[-1]
    h2 = _moe(h1, m, comb,
              w_e_gate[i].reshape(N_EXPERTS, d, f).astype(BF16),
              w_e_up[i].reshape(N_EXPERTS, d, f).astype(BF16),
              w_e_down[i].reshape(N_EXPERTS, f, d).astype(BF16))
    out = _ple(h2, p[i].reshape(n, -1), g_ple[i][None], w_ple_gate[i].astype(BF16),
               w_ple_proj[i].astype(BF16), g_final[None])
    return out.reshape(batch, seq_len, d)
```

```python
import functools
import math

import numpy as np
import jax
import jax.numpy as jnp
from jax import lax
from jax.experimental import pallas as pl
from jax.experimental.pallas import tpu as pltpu

HEAD_DIM = 64
N_HEADS = 8
POOL_WINDOWS = (2, 4, 8, 16)
POOL_GROUP_DIM = 128
POOL_HISTORY = 16
N_GROUPS = 4
EXPERTS_PER_GROUP = 8
N_EXPERTS = N_GROUPS * EXPERTS_PER_GROUP
EPS = 1e-6
LANES = 128
ROUTER_LANE0 = N_GROUPS
NEG_INF = float("-inf")
LOG2E = math.log2(math.e)
N_SPLIT = 3

ROW_TILE = 512
ATTN_Q = 512
MOE_TILE = 1024
VMEM_LIMIT = 48 * 1024 * 1024

BF16 = jnp.bfloat16
F32 = jnp.float32


def _dot(a, b):
    return jnp.dot(a, b, preferred_element_type=F32)


def _split2(a):
    hi = a.astype(BF16)
    lo = (a - hi.astype(F32)).astype(BF16)
    return hi, lo


def _split3(a):
    hi = a.astype(BF16)
    r = a - hi.astype(F32)
    mid = r.astype(BF16)
    lo = (r - mid.astype(F32)).astype(BF16)
    return hi, mid, lo


def _dot_precise(a, w):
    a1, a2 = _split2(a)
    w1, w2 = _split2(w)
    return _dot(a1, w1) + (_dot(a1, w2) + _dot(a2, w1))


def _rms_norm(x, g):
    return x * lax.rsqrt(jnp.mean(x * x, axis=-1, keepdims=True) + EPS) * g


def _own_half_start(head):
    return 0 if head % 2 == 0 else HEAD_DIM


def _bias_placement():
    width = N_HEADS * LANES
    eq = np.zeros((LANES, width), np.float32)
    ek = np.zeros((LANES, width), np.float32)
    ones_q = np.zeros((1, width), np.float32)
    ones_k = np.zeros((1, width), np.float32)
    for h in range(N_HEADS):
        a0 = h * LANES + (HEAD_DIM - _own_half_start(h))
        for piece in range(N_SPLIT):
            eq[piece * N_HEADS + h, a0 + piece] = 1.0
            ones_q[0, a0 + N_SPLIT + piece] = 1.0
            ones_k[0, a0 + piece] = 1.0
            ek[piece * N_HEADS + h, a0 + N_SPLIT + piece] = -1.0
    return eq, ek, ones_q, ones_k


def _inproj_kernel(tiles_per_seq, x_ref, g_ref, w_ref, wf_ref, bf_ref, wp_ref, sp_ref,
                   eq_ref, ek_ref, oq_ref, ok_ref,
                   pool_ref, q_ref, k_ref, v_ref, carry_c, carry_u):
    i = pl.program_id(0)
    seq_tile = i % tiles_per_seq
    tm = x_ref.shape[0]
    pool_dim = pool_ref.shape[1]
    attn_dim = N_HEADS * HEAD_DIM

    @pl.when(seq_tile == 0)
    def _():
        carry_c[...] = jnp.zeros_like(carry_c)
        carry_u[...] = jnp.zeros_like(carry_u)

    a = _rms_norm(x_ref[...], g_ref[...])
    ab = a.astype(BF16)
    o0 = pool_dim
    u = _dot(ab, w_ref[:, 0:o0])
    qf = _dot(ab, w_ref[:, o0:o0 + attn_dim]) * (LOG2E * HEAD_DIM ** -0.5)
    kf = _dot(ab, w_ref[:, o0 + attn_dim:o0 + 2 * attn_dim])
    vf = _dot(ab, w_ref[:, o0 + 2 * attn_dim:o0 + 3 * attn_dim])

    fl = _dot_precise(a, wf_ref[...]) + bf_ref[...]
    lf = jnp.minimum(fl, 0.0) - jnp.log1p(jnp.exp(-jnp.abs(fl)))
    row = lax.broadcasted_iota(jnp.int32, (tm, tm), 0)
    col = lax.broadcasted_iota(jnp.int32, (tm, tm), 1)
    tril = (col <= row).astype(BF16)
    l1, l2, l3 = _split3(lf)
    c = carry_c[...] + (_dot(tril, l1) + (_dot(tril, l2) + _dot(tril, l3)))
    carry_c[...] = c[tm - 1:tm, :]
    c1, c2, c3 = _split3(c * LOG2E)
    lane = lax.broadcasted_iota(jnp.int32, (1, LANES), 1)
    pieces = jnp.where(lane < N_HEADS, c1, jnp.where(lane < 2 * N_HEADS, c2, c3))
    bias_q = _dot(pieces, eq_ref[...]) + oq_ref[...]
    bias_k = _dot(pieces, ek_ref[...]) + ok_ref[...]
    for h in range(N_HEADS):
        own = (lane >= _own_half_start(h)) & (lane < _own_half_start(h) + HEAD_DIM)
        pair = slice((h // 2) * LANES, (h // 2 + 1) * LANES)
        blk = slice(h * LANES, (h + 1) * LANES)
        q_ref[:, blk] = jnp.where(own, qf[:, pair], bias_q[:, blk]).astype(BF16)
        k_ref[:, blk] = jnp.where(own, kf[:, pair], bias_k[:, blk]).astype(BF16)
        one_col = (lane == HEAD_DIM - _own_half_start(h)).astype(F32)
        v_ref[:, blk] = jnp.where(own, vf[:, pair], one_col).astype(BF16)

    ext = jnp.concatenate([carry_u[...], u], axis=0)
    carry_u[...] = u[tm - POOL_HISTORY:, :]
    pos = (seq_tile * tm + 1 + lax.broadcasted_iota(jnp.int32, (tm, 1), 0)).astype(F32)
    for gi, w in enumerate(POOL_WINDOWS):
        lo, hi = gi * POOL_GROUP_DIM, (gi + 1) * POOL_GROUP_DIM
        s = ext[:, lo:hi]
        shift = 1
        while shift < w:
            s = s + pltpu.roll(s, shift, axis=0)
            shift *= 2
        mean = s[POOL_HISTORY:, :] / jnp.minimum(pos, float(w))
        d = mean - u[:, lo:hi]
        y = _dot(d.astype(BF16), wp_ref[gi]) * sp_ref[:, lo:hi]
        pool_ref[:, lo:hi] = y.astype(BF16)


def _inproj(x2, g_mix, w_main, w_f, b_f, w_pool, s_pool, seq_len):
    n, d = x2.shape
    tm = ROW_TILE
    pool_dim = w_pool.shape[0] * w_pool.shape[1]
    head_w = N_HEADS * LANES
    eq, ek, ones_q, ones_k = _bias_placement()
    row = lambda i: (i, 0)
    full2 = lambda i: (0, 0)
    return pl.pallas_call(
        functools.partial(_inproj_kernel, seq_len // tm),
        grid=(n // tm,),
        in_specs=[
            pl.BlockSpec((tm, d), row),
            pl.BlockSpec((1, d), full2),
            pl.BlockSpec(w_main.shape, full2),
            pl.BlockSpec(w_f.shape, full2),
            pl.BlockSpec((1, LANES), full2),
            pl.BlockSpec(w_pool.shape, lambda i: (0, 0, 0)),
            pl.BlockSpec((1, pool_dim), full2),
            pl.BlockSpec((LANES, head_w), full2),
            pl.BlockSpec((LANES, head_w), full2),
            pl.BlockSpec((1, head_w), full2),
            pl.BlockSpec((1, head_w), full2),
        ],
        out_specs=[
            pl.BlockSpec((tm, pool_dim), row),
            pl.BlockSpec((tm, head_w), row),
            pl.BlockSpec((tm, head_w), row),
            pl.BlockSpec((tm, head_w), row),
        ],
        out_shape=[
            jax.ShapeDtypeStruct((n, pool_dim), BF16),
            jax.ShapeDtypeStruct((n, head_w), BF16),
            jax.ShapeDtypeStruct((n, head_w), BF16),
            jax.ShapeDtypeStruct((n, head_w), BF16),
        ],
        scratch_shapes=[pltpu.VMEM((1, LANES), F32), pltpu.VMEM((POOL_HISTORY, pool_dim), F32)],
        compiler_params=pltpu.CompilerParams(
            dimension_semantics=("arbitrary",), vmem_limit_bytes=VMEM_LIMIT),
        name="inproj",
    )(x2, g_mix, w_main, w_f, b_f, w_pool, s_pool,
      jnp.asarray(eq, BF16), jnp.asarray(ek, BF16), jnp.asarray(ones_q), jnp.asarray(ones_k))


def _attend(q, k_ref, v_ref, blk, n_keys):
    tq = q.shape[0]
    past = n_keys - tq
    nt = (((1,), (1,)), ((), ()))
    s_diag = lax.dot_general(q, k_ref[past:n_keys, blk], nt, preferred_element_type=F32)
    row = lax.broadcasted_iota(jnp.int32, (tq, tq), 0)
    col = lax.broadcasted_iota(jnp.int32, (tq, tq), 1)
    s_diag = jnp.where(col <= row, s_diag, NEG_INF)
    m = jnp.max(s_diag, axis=-1, keepdims=True)
    if past:
        s_past = lax.dot_general(q, k_ref[0:past, blk], nt, preferred_element_type=F32)
        m = jnp.maximum(m, jnp.max(s_past, axis=-1, keepdims=True))
    acc = _dot(jnp.exp2(s_diag - m).astype(BF16), v_ref[past:n_keys, blk])
    if past:
        acc = acc + _dot(jnp.exp2(s_past - m).astype(BF16), v_ref[0:past, blk])
    return acc


def _attn_kernel(q_ref, k_ref, v_ref, o_ref):
    qi = pl.program_id(2)
    tq = q_ref.shape[0]
    lane = lax.broadcasted_iota(jnp.int32, (1, LANES), 1)
    for tile in range(k_ref.shape[0] // tq):
        @pl.when(qi == tile)
        def _():
            out = None
            for hh in range(2):
                blk = slice(hh * LANES, (hh + 1) * LANES)
                acc = _attend(q_ref[:, blk], k_ref, v_ref, blk, (tile + 1) * tq)
                sum_lane = HEAD_DIM - _own_half_start(hh)
                l = jnp.sum(jnp.where(lane == sum_lane, acc, 0.0), axis=-1, keepdims=True)
                o = acc / l
                out = o if out is None else jnp.where(lane < HEAD_DIM, out, o)
            o_ref[...] = out.astype(BF16)


def _attention(q, k, v, batch, seq_len):
    n = q.shape[0]
    pairs = N_HEADS // 2
    tq = ATTN_Q
    qt = seq_len // tq
    pair_block = lambda b, hp, qi: (b, hp)
    return pl.pallas_call(
        _attn_kernel,
        grid=(batch, pairs, qt),
        in_specs=[
            pl.BlockSpec((tq, 2 * LANES), lambda b, hp, qi: (b * qt + qi, hp)),
            pl.BlockSpec((seq_len, 2 * LANES), pair_block),
            pl.BlockSpec((seq_len, 2 * LANES), pair_block),
        ],
        out_specs=pl.BlockSpec((tq, LANES), lambda b, hp, qi: (b * qt + qi, hp)),
        out_shape=jax.ShapeDtypeStruct((n, pairs * LANES), BF16),
        compiler_params=pltpu.CompilerParams(
            dimension_semantics=("arbitrary", "arbitrary", "arbitrary"),
            vmem_limit_bytes=VMEM_LIMIT),
        name="attn",
    )(q, k, v)


def _route(logits):
    lane = lax.broadcasted_iota(jnp.int32, logits.shape, 1)
    big = jnp.int32(LANES)
    gl = jnp.where(lane < N_GROUPS, logits, NEG_INF)
    gmax = jnp.max(gl, axis=-1, keepdims=True)
    g_w = 1.0 / jnp.sum(jnp.exp(gl - gmax), axis=-1, keepdims=True)
    g_idx = jnp.min(jnp.where(gl == gmax, lane, big), axis=-1, keepdims=True)
    e_lo = ROUTER_LANE0 + EXPERTS_PER_GROUP * g_idx
    el = jnp.where((lane >= e_lo) & (lane < e_lo + EXPERTS_PER_GROUP), logits, NEG_INF)
    v1 = jnp.max(el, axis=-1, keepdims=True)
    i1 = jnp.min(jnp.where(el == v1, lane, big), axis=-1, keepdims=True)
    el2 = jnp.where(lane == i1, NEG_INF, el)
    v2 = jnp.max(el2, axis=-1, keepdims=True)
    i2 = jnp.min(jnp.where(el2 == v2, lane, big), axis=-1, keepdims=True)
    e2 = jnp.exp(v2 - v1)
    w1 = g_w / (1.0 + e2)
    w2 = g_w * e2 / (1.0 + e2)
    return jnp.where(lane == i1, w1, 0.0) + jnp.where(lane == i2, w2, 0.0)


def _mix_kernel(x_ref, pool_ref, attn_ref, wo_ref, g_ref, wr_ref, br_ref,
                h_ref, m_ref, comb_ref):
    pool_dim = pool_ref.shape[1]
    h = x_ref[...] + (_dot(pool_ref[...], wo_ref[0:pool_dim, :])
                      + _dot(attn_ref[...], wo_ref[pool_dim:, :]))
    h_ref[...] = h
    m = _rms_norm(h, g_ref[...])
    m_ref[...] = m.astype(BF16)
    logits = _dot_precise(m, wr_ref[...]) + br_ref[...]
    comb_ref[...] = _route(logits)


def _mix(x2, pool, attn, w_out, g_ffn, w_router, b_router):
    n, d = x2.shape
    tm = ROW_TILE
    row = lambda i: (i, 0)
    full2 = lambda i: (0, 0)
    return pl.pallas_call(
        _mix_kernel,
        grid=(n // tm,),
        in_specs=[
            pl.BlockSpec((tm, d), row),
            pl.BlockSpec((tm, pool.shape[1]), row),
            pl.BlockSpec((tm, attn.shape[1]), row),
            pl.BlockSpec(w_out.shape, full2),
            pl.BlockSpec((1, d), full2),
            pl.BlockSpec(w_router.shape, full2),
            pl.BlockSpec((1, LANES), full2),
        ],
        out_specs=[
            pl.BlockSpec((tm, d), row),
            pl.BlockSpec((tm, d), row),
            pl.BlockSpec((tm, LANES), row),
        ],
        out_shape=[
            jax.ShapeDtypeStruct((n, d), F32),
            jax.ShapeDtypeStruct((n, d), BF16),
            jax.ShapeDtypeStruct((n, LANES), F32),
        ],
        compiler_params=pltpu.CompilerParams(
            dimension_semantics=("arbitrary",), vmem_limit_bytes=VMEM_LIMIT),
        name="mix",
    )(x2, pool, attn, w_out, g_ffn, w_router, b_router)


def _moe_kernel(h_ref, m_ref, comb_ref, wg_ref, wu_ref, wd_ref, o_ref, acc_ref):
    e = pl.program_id(1)

    @pl.when(e == 0)
    def _():
        acc_ref[...] = jnp.zeros_like(acc_ref)

    x = m_ref[...]
    comb = comb_ref[...]
    lane = lax.broadcasted_iota(jnp.int32, comb.shape, 1)
    cw = jnp.sum(jnp.where(lane == ROUTER_LANE0 + e, comb, 0.0), axis=-1, keepdims=True)
    hg = _dot(x, wg_ref[0])
    hu = _dot(x, wu_ref[0])
    a = hg * jax.nn.sigmoid(hg) * hu * cw
    acc_ref[...] += _dot(a.astype(BF16), wd_ref[0])

    @pl.when(e == pl.num_programs(1) - 1)
    def _():
        o_ref[...] = h_ref[...] + acc_ref[...]


def _moe(h1, m, comb, w_gate, w_up, w_down):
    n, d = h1.shape
    n_exp, _, f = w_gate.shape
    tm = MOE_TILE
    row = lambda i, e: (i, 0)
    return pl.pallas_call(
        _moe_kernel,
        grid=(n // tm, n_exp),
        in_specs=[
            pl.BlockSpec((tm, d), row),
            pl.BlockSpec((tm, d), row),
            pl.BlockSpec((tm, LANES), row),
            pl.BlockSpec((1, d, f), lambda i, e: (e, 0, 0)),
            pl.BlockSpec((1, d, f), lambda i, e: (e, 0, 0)),
            pl.BlockSpec((1, f, d), lambda i, e: (e, 0, 0)),
        ],
        out_specs=pl.BlockSpec((tm, d), row),
        out_shape=jax.ShapeDtypeStruct((n, d), F32),
        scratch_shapes=[pltpu.VMEM((tm, d), F32)],
        compiler_params=pltpu.CompilerParams(
            dimension_semantics=("arbitrary", "arbitrary"), vmem_limit_bytes=VMEM_LIMIT),
        name="moe",
    )(h1, m, comb, w_gate, w_up, w_down)


def _ple_kernel(h_ref, p_ref, g_ref, wg_ref, wp_ref, gf_ref, o_ref):
    h = h_ref[...]
    gate = jax.nn.sigmoid(_dot(_rms_norm(h, g_ref[...]).astype(BF16), wg_ref[...]))
    h = h + gate * _dot(p_ref[...].astype(BF16), wp_ref[...])
    o_ref[...] = _rms_norm(h, gf_ref[...])


def _ple(h2, p2, g_ple, w_gate, w_proj, g_final):
    n, d = h2.shape
    tm = ROW_TILE
    row = lambda i: (i, 0)
    full2 = lambda i: (0, 0)
    return pl.pallas_call(
        _ple_kernel,
        grid=(n // tm,),
        in_specs=[
            pl.BlockSpec((tm, d), row),
            pl.BlockSpec((tm, p2.shape[1]), row),
            pl.BlockSpec((1, d), full2),
            pl.BlockSpec(w_gate.shape, full2),
            pl.BlockSpec(w_proj.shape, full2),
            pl.BlockSpec((1, d), full2),
        ],
        out_specs=pl.BlockSpec((tm, d), row),
        out_shape=jax.ShapeDtypeStruct((n, d), F32),
        compiler_params=pltpu.CompilerParams(
            dimension_semantics=("arbitrary",), vmem_limit_bytes=VMEM_LIMIT),
        name="ple",
    )(h2, p2, g_ple, w_gate, w_proj, g_final)


def _pad_lanes(a):
    return jnp.pad(a, ((0, 0), (0, LANES - a.shape[1])))


def kernel(x, p, g_mix, w_in, b_f, w_pool, s_pool, w_out, g_ffn, w_grp, b_grp, w_rt, b_rt,
           w_e_gate, w_e_up, w_e_down, g_ple, w_ple_gate, w_ple_proj, g_final):
    batch, seq_len, d = x.shape
    n = batch * seq_len
    assert w_in.shape[0] == 1, "single-layer stack only: the final norm is fused into the layer"
    i = 0
    pool_dim = s_pool.shape[1]
    attn_dim = N_HEADS * HEAD_DIM
    main = pool_dim + 3 * attn_dim
    h = x.reshape(n, d)
    w_main = w_in[i, :, :main].astype(BF16)
    w_f = _pad_lanes(jnp.tile(w_in[i, :, main:], (1, N_SPLIT)))
    b_f3 = _pad_lanes(jnp.tile(b_f[i], N_SPLIT)[None])
    pool, q, k, v = _inproj(h, g_mix[i][None], w_main, w_f, b_f3,
                            w_pool[i].astype(BF16), s_pool[i][None], seq_len)
    attn = _attention(q, k, v, batch, seq_len)
    w_router = _pad_lanes(jnp.concatenate([w_grp[i], w_rt[i]], axis=1))
    b_router = _pad_lanes(jnp.concatenate([b_grp[i], b_rt[i]])[None])
    h1, m, comb = _mix(h, pool, attn, w_out[i].astype(BF16), g_ffn[i][None],
                       w_router, b_router)
    f = w_e_gate.shape[-1]
    h2 = _moe(h1, m, comb,
              w_e_gate[i].reshape(N_EXPERTS, d, f).astype(BF16),
              w_e_up[i].reshape(N_EXPERTS, d, f).astype(BF16),
              w_e_down[i].reshape(N_EXPERTS, f, d).astype(BF16))
    out = _ple(h2, p[i].reshape(n, -1), g_ple[i][None], w_ple_gate[i].astype(BF16),
               w_ple_proj[i].astype(BF16), g_final[None])
    return out.reshape(batch, seq_len, d)
```

```python
import functools
import math

import numpy as np
import jax
import jax.numpy as jnp
from jax import lax
from jax.experimental import pallas as pl
from jax.experimental.pallas import tpu as pltpu

HEAD_DIM = 64
N_HEADS = 8
POOL_WINDOWS = (2, 4, 8, 16)
POOL_GROUP_DIM = 128
POOL_HISTORY = 16
N_GROUPS = 4
EXPERTS_PER_GROUP = 8
N_EXPERTS = N_GROUPS * EXPERTS_PER_GROUP
EPS = 1e-6
LANES = 128
ROUTER_LANE0 = N_GROUPS
NEG_INF = float("-inf")
LOG2E = math.log2(math.e)
N_SPLIT = 3

ROW_TILE = 512
ATTN_Q = 512
VMEM_LIMIT = 48 * 1024 * 1024

BF16 = jnp.bfloat16
F32 = jnp.float32


def _dot(a, b):
    return jnp.dot(a, b, preferred_element_type=F32)


def _split2(a):
    hi = a.astype(BF16)
    lo = (a - hi.astype(F32)).astype(BF16)
    return hi, lo


def _split3(a):
    hi = a.astype(BF16)
    r = a - hi.astype(F32)
    mid = r.astype(BF16)
    lo = (r - mid.astype(F32)).astype(BF16)
    return hi, mid, lo


def _dot_precise(a, w):
    a1, a2 = _split2(a)
    w1, w2 = _split2(w)
    return _dot(a1, w1) + (_dot(a1, w2) + _dot(a2, w1))


def _rms_norm(x, g):
    return x * lax.rsqrt(jnp.mean(x * x, axis=-1, keepdims=True) + EPS) * g


def _own_half_start(head):
    return 0 if head % 2 == 0 else HEAD_DIM


def _bias_placement():
    width = N_HEADS * LANES
    eq = np.zeros((LANES, width), np.float32)
    ek = np.zeros((LANES, width), np.float32)
    ones_q = np.zeros((1, width), np.float32)
    ones_k = np.zeros((1, width), np.float32)
    for h in range(N_HEADS):
        a0 = h * LANES + (HEAD_DIM - _own_half_start(h))
        for piece in range(N_SPLIT):
            eq[piece * N_HEADS + h, a0 + piece] = 1.0
            ones_q[0, a0 + N_SPLIT + piece] = 1.0
            ones_k[0, a0 + piece] = 1.0
            ek[piece * N_HEADS + h, a0 + N_SPLIT + piece] = -1.0
    return eq, ek, ones_q, ones_k


def _inproj_kernel(tiles_per_seq, x_ref, g_ref, w_ref, wf_ref, bf_ref, wp_ref, sp_ref,
                   eq_ref, ek_ref, oq_ref, ok_ref,
                   pool_ref, q_ref, k_ref, v_ref, carry_c, carry_u):
    i = pl.program_id(0)
    seq_tile = i % tiles_per_seq
    tm = x_ref.shape[0]
    pool_dim = pool_ref.shape[1]
    attn_dim = N_HEADS * HEAD_DIM

    @pl.when(seq_tile == 0)
    def _():
        carry_c[...] = jnp.zeros_like(carry_c)
        carry_u[...] = jnp.zeros_like(carry_u)

    a = _rms_norm(x_ref[...], g_ref[...])
    ab = a.astype(BF16)
    o0 = pool_dim
    u = _dot(ab, w_ref[:, 0:o0])
    qf = _dot(ab, w_ref[:, o0:o0 + attn_dim]) * (LOG2E * HEAD_DIM ** -0.5)
    kf = _dot(ab, w_ref[:, o0 + attn_dim:o0 + 2 * attn_dim])
    vf = _dot(ab, w_ref[:, o0 + 2 * attn_dim:o0 + 3 * attn_dim])

    fl = _dot_precise(a, wf_ref[...]) + bf_ref[...]
    lf = jnp.minimum(fl, 0.0) - jnp.log1p(jnp.exp(-jnp.abs(fl)))
    row = lax.broadcasted_iota(jnp.int32, (tm, tm), 0)
    col = lax.broadcasted_iota(jnp.int32, (tm, tm), 1)
    tril = (col <= row).astype(BF16)
    l1, l2, l3 = _split3(lf)
    c = carry_c[...] + (_dot(tril, l1) + (_dot(tril, l2) + _dot(tril, l3)))
    carry_c[...] = c[tm - 1:tm, :]
    c1, c2, c3 = _split3(c * LOG2E)
    lane = lax.broadcasted_iota(jnp.int32, (1, LANES), 1)
    pieces = jnp.where(lane < N_HEADS, c1, jnp.where(lane < 2 * N_HEADS, c2, c3))
    bias_q = _dot(pieces, eq_ref[...]) + oq_ref[...]
    bias_k = _dot(pieces, ek_ref[...]) + ok_ref[...]
    for h in range(N_HEADS):
        own = (lane >= _own_half_start(h)) & (lane < _own_half_start(h) + HEAD_DIM)
        pair = slice((h // 2) * LANES, (h // 2 + 1) * LANES)
        blk = slice(h * LANES, (h + 1) * LANES)
        q_ref[:, blk] = jnp.where(own, qf[:, pair], bias_q[:, blk]).astype(BF16)
        k_ref[:, blk] = jnp.where(own, kf[:, pair], bias_k[:, blk]).astype(BF16)
        one_col = (lane == HEAD_DIM - _own_half_start(h)).astype(F32)
        v_ref[:, blk] = jnp.where(own, vf[:, pair], one_col).astype(BF16)

    ext = jnp.concatenate([carry_u[...], u], axis=0)
    carry_u[...] = u[tm - POOL_HISTORY:, :]
    pos = (seq_tile * tm + 1 + lax.broadcasted_iota(jnp.int32, (tm, 1), 0)).astype(F32)
    for gi, w in enumerate(POOL_WINDOWS):
        lo, hi = gi * POOL_GROUP_DIM, (gi + 1) * POOL_GROUP_DIM
        s = ext[:, lo:hi]
        shift = 1
        while shift < w:
            s = s + pltpu.roll(s, shift, axis=0)
            shift *= 2
        mean = s[POOL_HISTORY:, :] / jnp.minimum(pos, float(w))
        d = mean - u[:, lo:hi]
        y = _dot(d.astype(BF16), wp_ref[gi]) * sp_ref[:, lo:hi]
        pool_ref[:, lo:hi] = y.astype(BF16)


def _inproj(x2, g_mix, w_main, w_f, b_f, w_pool, s_pool, seq_len):
    n, d = x2.shape
    tm = ROW_TILE
    pool_dim = w_pool.shape[0] * w_pool.shape[1]
    head_w = N_HEADS * LANES
    eq, ek, ones_q, ones_k = _bias_placement()
    row = lambda i: (i, 0)
    full2 = lambda i: (0, 0)
    return pl.pallas_call(
        functools.partial(_inproj_kernel, seq_len // tm),
        grid=(n // tm,),
        in_specs=[
            pl.BlockSpec((tm, d), row),
            pl.BlockSpec((1, d), full2),
            pl.BlockSpec(w_main.shape, full2),
            pl.BlockSpec(w_f.shape, full2),
            pl.BlockSpec((1, LANES), full2),
            pl.BlockSpec(w_pool.shape, lambda i: (0, 0, 0)),
            pl.BlockSpec((1, pool_dim), full2),
            pl.BlockSpec((LANES, head_w), full2),
            pl.BlockSpec((LANES, head_w), full2),
            pl.BlockSpec((1, head_w), full2),
            pl.BlockSpec((1, head_w), full2),
        ],
        out_specs=[
            pl.BlockSpec((tm, pool_dim), row),
            pl.BlockSpec((tm, head_w), row),
            pl.BlockSpec((tm, head_w), row),
            pl.BlockSpec((tm, head_w), row),
        ],
        out_shape=[
            jax.ShapeDtypeStruct((n, pool_dim), BF16),
            jax.ShapeDtypeStruct((n, head_w), BF16),
            jax.ShapeDtypeStruct((n, head_w), BF16),
            jax.ShapeDtypeStruct((n, head_w), BF16),
        ],
        scratch_shapes=[pltpu.VMEM((1, LANES), F32), pltpu.VMEM((POOL_HISTORY, pool_dim), F32)],
        compiler_params=pltpu.CompilerParams(
            dimension_semantics=("arbitrary",), vmem_limit_bytes=VMEM_LIMIT),
        name="inproj",
    )(x2, g_mix, w_main, w_f, b_f, w_pool, s_pool,
      jnp.asarray(eq, BF16), jnp.asarray(ek, BF16), jnp.asarray(ones_q), jnp.asarray(ones_k))


def _attend(q, k_ref, v_ref, blk, n_keys):
    tq = q.shape[0]
    past = n_keys - tq
    nt = (((1,), (1,)), ((), ()))
    s_diag = lax.dot_general(q, k_ref[past:n_keys, blk], nt, preferred_element_type=F32)
    row = lax.broadcasted_iota(jnp.int32, (tq, tq), 0)
    col = lax.broadcasted_iota(jnp.int32, (tq, tq), 1)
    s_diag = jnp.where(col <= row, s_diag, NEG_INF)
    m = jnp.max(s_diag, axis=-1, keepdims=True)
    if past:
        s_past = lax.dot_general(q, k_ref[0:past, blk], nt, preferred_element_type=F32)
        m = jnp.maximum(m, jnp.max(s_past, axis=-1, keepdims=True))
    acc = _dot(jnp.exp2(s_diag - m).astype(BF16), v_ref[past:n_keys, blk])
    if past:
        acc = acc + _dot(jnp.exp2(s_past - m).astype(BF16), v_ref[0:past, blk])
    return acc


def _attn_kernel(q_ref, k_ref, v_ref, o_ref):
    qi = pl.program_id(2)
    tq = q_ref.shape[0]
    lane = lax.broadcasted_iota(jnp.int32, (1, LANES), 1)
    for tile in range(k_ref.shape[0] // tq):
        @pl.when(qi == tile)
        def _():
            out = None
            for hh in range(2):
                blk = slice(hh * LANES, (hh + 1) * LANES)
                acc = _attend(q_ref[:, blk], k_ref, v_ref, blk, (tile + 1) * tq)
                sum_lane = HEAD_DIM - _own_half_start(hh)
                l = jnp.sum(jnp.where(lane == sum_lane, acc, 0.0), axis=-1, keepdims=True)
                o = acc / l
                out = o if out is None else jnp.where(lane < HEAD_DIM, out, o)
            o_ref[...] = out.astype(BF16)


def _attention(q, k, v, batch, seq_len):
    n = q.shape[0]
    pairs = N_HEADS // 2
    tq = ATTN_Q
    qt = seq_len // tq
    pair_block = lambda b, hp, qi: (b, hp)
    return pl.pallas_call(
        _attn_kernel,
        grid=(batch, pairs, qt),
        in_specs=[
            pl.BlockSpec((tq, 2 * LANES), lambda b, hp, qi: (b * qt + qi, hp)),
            pl.BlockSpec((seq_len, 2 * LANES), pair_block),
            pl.BlockSpec((seq_len, 2 * LANES), pair_block),
        ],
        out_specs=pl.BlockSpec((tq, LANES), lambda b, hp, qi: (b * qt + qi, hp)),
        out_shape=jax.ShapeDtypeStruct((n, pairs * LANES), BF16),
        compiler_params=pltpu.CompilerParams(
            dimension_semantics=("arbitrary", "arbitrary", "arbitrary"),
            vmem_limit_bytes=VMEM_LIMIT),
        name="attn",
    )(q, k, v)


def _route(logits):
    lane = lax.broadcasted_iota(jnp.int32, logits.shape, 1)
    big = jnp.int32(LANES)
    gl = jnp.where(lane < N_GROUPS, logits, NEG_INF)
    gmax = jnp.max(gl, axis=-1, keepdims=True)
    g_w = 1.0 / jnp.sum(jnp.exp(gl - gmax), axis=-1, keepdims=True)
    g_idx = jnp.min(jnp.where(gl == gmax, lane, big), axis=-1, keepdims=True)
    e_lo = ROUTER_LANE0 + EXPERTS_PER_GROUP * g_idx
    el = jnp.where((lane >= e_lo) & (lane < e_lo + EXPERTS_PER_GROUP), logits, NEG_INF)
    v1 = jnp.max(el, axis=-1, keepdims=True)
    i1 = jnp.min(jnp.where(el == v1, lane, big), axis=-1, keepdims=True)
    el2 = jnp.where(lane == i1, NEG_INF, el)
    v2 = jnp.max(el2, axis=-1, keepdims=True)
    i2 = jnp.min(jnp.where(el2 == v2, lane, big), axis=-1, keepdims=True)
    e2 = jnp.exp(v2 - v1)
    w1 = g_w / (1.0 + e2)
    w2 = g_w * e2 / (1.0 + e2)
    ids = [(i1 - ROUTER_LANE0).astype(F32), (i2 - ROUTER_LANE0).astype(F32), w1, w2]
    out = jnp.zeros(logits.shape, F32)
    for k, val in enumerate(ids):
        out = jnp.where(lane == k, val, out)
    return out


def _mix_kernel(x_ref, pool_ref, attn_ref, wo_ref, g_ref, wr_ref, br_ref,
                h_ref, m_ref, rt_ref, cnt_ref):
    pool_dim = pool_ref.shape[1]
    h = x_ref[...] + (_dot(pool_ref[...], wo_ref[0:pool_dim, :])
                      + _dot(attn_ref[...], wo_ref[pool_dim:, :]))
    h_ref[...] = h
    m = _rms_norm(h, g_ref[...])
    m_ref[...] = m.astype(BF16)
    logits = _dot_precise(m, wr_ref[...]) + br_ref[...]
    rt = _route(logits)
    rt_ref[...] = rt
    lane = lax.broadcasted_iota(jnp.int32, rt.shape, 1).astype(F32)
    picked = (lane == rt[:, 0:1]) | (lane == rt[:, 1:2])
    cnt_ref[0] = jnp.sum(picked.astype(F32), axis=0, keepdims=True)


def _mix(x2, pool, attn, w_out, g_ffn, w_router, b_router):
    n, d = x2.shape
    tm = ROW_TILE
    row = lambda i: (i, 0)
    full2 = lambda i: (0, 0)
    return pl.pallas_call(
        _mix_kernel,
        grid=(n // tm,),
        in_specs=[
            pl.BlockSpec((tm, d), row),
            pl.BlockSpec((tm, pool.shape[1]), row),
            pl.BlockSpec((tm, attn.shape[1]), row),
            pl.BlockSpec(w_out.shape, full2),
            pl.BlockSpec((1, d), full2),
            pl.BlockSpec(w_router.shape, full2),
            pl.BlockSpec((1, LANES), full2),
        ],
        out_specs=[
            pl.BlockSpec((tm, d), row),
            pl.BlockSpec((tm, d), row),
            pl.BlockSpec((tm, LANES), row),
            pl.BlockSpec((1, 1, LANES), lambda i: (i, 0, 0)),
        ],
        out_shape=[
            jax.ShapeDtypeStruct((n, d), F32),
            jax.ShapeDtypeStruct((n, d), BF16),
            jax.ShapeDtypeStruct((n, LANES), F32),
            jax.ShapeDtypeStruct((n // tm, 1, LANES), F32),
        ],
        compiler_params=pltpu.CompilerParams(
            dimension_semantics=("arbitrary",), vmem_limit_bytes=VMEM_LIMIT),
        name="mix",
    )(x2, pool, attn, w_out, g_ffn, w_router, b_router)


SEG_ALIGN = 8
COPY_ROWS = 2 * SEG_ALIGN
EXPERT_TILE = 256
LOCAL_ROWS = 2 * ROW_TILE + N_EXPERTS * SEG_ALIGN
HIGH_HALF = -65536


def _bits(x):
    return lax.bitcast_convert_type(x, jnp.int32)


def _pack_pairs(lo, hi):
    return (_bits(hi) & HIGH_HALF) | lax.shift_right_logical(_bits(lo), 16)


def _unpack_pairs(words):
    lo = lax.bitcast_convert_type(lax.shift_left(words, 16), F32)
    hi = lax.bitcast_convert_type(words & HIGH_HALF, F32)
    return lo.astype(BF16), hi.astype(BF16)


def _slot_masks(rt, lstart):
    t = rt.shape[0]
    lane = lax.broadcasted_iota(jnp.int32, rt.shape, 1).astype(F32)
    oh0 = lane == rt[:, 0:1]
    oh1 = lane == rt[:, 1:2]
    row = lax.broadcasted_iota(jnp.int32, (t, t), 0)
    col = lax.broadcasted_iota(jnp.int32, (t, t), 1)
    earlier = (col < row).astype(BF16)
    base = _dot(earlier, (oh0 | oh1).astype(BF16)) + lstart
    slot0 = jnp.sum(jnp.where(oh0, base, 0.0), axis=-1, keepdims=True).astype(jnp.int32)
    slot1 = jnp.sum(jnp.where(oh1, base, 0.0), axis=-1, keepdims=True).astype(jnp.int32)
    s = lax.broadcasted_iota(jnp.int32, (1, LOCAL_ROWS), 1)
    return s == slot0, s == slot1


def _for_each_piece(n8_ref, lrow_ref, grow_ref, tile, make_copy, act):
    base = tile * N_EXPERTS

    def per_expert(e, carry):
        n = n8_ref[base + e]
        l0 = lrow_ref[base + e] * SEG_ALIGN
        g0 = grow_ref[base + e] * SEG_ALIGN

        def per_block(c, carry):
            off = c * COPY_ROWS
            act(make_copy(pl.multiple_of(l0 + off, SEG_ALIGN),
                          pl.multiple_of(g0 + off, SEG_ALIGN), COPY_ROWS))
            return carry

        lax.fori_loop(0, n // 2, per_block, 0)

        @pl.when(n % 2 == 1)
        def _():
            off = (n // 2) * COPY_ROWS
            act(make_copy(pl.multiple_of(l0 + off, SEG_ALIGN),
                          pl.multiple_of(g0 + off, SEG_ALIGN), SEG_ALIGN))

        return carry

    lax.fori_loop(0, N_EXPERTS, per_expert, 0)


def _start(copy):
    copy.start()


def _wait(copy):
    copy.wait()


def _weight_pieces(w):
    lane = lax.broadcasted_iota(jnp.int32, (1, LANES), 1)
    p1, p2, p3 = (piece.astype(F32) for piece in _split3(w))
    return jnp.where(lane == 0, p1, jnp.where(lane == 1, p2, jnp.where(lane == 2, p3, 0.0))
                     ).astype(BF16)


def _dispatch_kernel(n_steps, n8_ref, lrow_ref, grow_ref, tail0_ref, tailn_ref, nu_ref,
                     m_ref, rt_ref, ls_ref, xs_hbm, buf, zeros, sems, tail_sem):
    i = pl.program_id(0)
    slot = i % 2
    half = m_ref.shape[1] // 2
    n_expert_tiles = xs_hbm.shape[0] // EXPERT_TILE

    def seg_copy(slot_):
        def make(l, g, rows):
            return pltpu.make_async_copy(buf.at[slot_, pl.ds(l, rows), :],
                                         xs_hbm.at[pl.ds(g, rows), :], sems.at[slot_])
        return make

    def for_each_tail(act):
        @pl.when(i < N_EXPERTS)
        def _():
            e = jnp.minimum(i, N_EXPERTS - 1)
            g0 = tail0_ref[e] * SEG_ALIGN

            def body(c, carry):
                g = pl.multiple_of(g0 + c * SEG_ALIGN, SEG_ALIGN)
                act(pltpu.make_async_copy(zeros.at[pl.ds(0, SEG_ALIGN), :],
                                          xs_hbm.at[pl.ds(g, SEG_ALIGN), :], tail_sem))
                return carry

            lax.fori_loop(0, tailn_ref[e], body, 0)

        for k in range(-(-n_expert_tiles // n_steps)):
            t = nu_ref[0] + i + k * n_steps

            @pl.when(t < n_expert_tiles)
            def _():
                g = pl.multiple_of(t * EXPERT_TILE, EXPERT_TILE)
                act(pltpu.make_async_copy(zeros, xs_hbm.at[pl.ds(g, EXPERT_TILE), :], tail_sem))

    @pl.when(i == 0)
    def _():
        zeros[...] = jnp.zeros_like(zeros)

    for_each_tail(_start)

    rt = rt_ref[...]
    mask0, mask1 = _slot_masks(rt, ls_ref[0])
    ta = (((0,), (0,)), ((), ()))
    perm = (mask0 | mask1).astype(BF16)
    xs = lax.dot_general(perm, m_ref[...], ta, preferred_element_type=F32)
    ws = (lax.dot_general(mask0.astype(BF16), _weight_pieces(rt[:, 2:3]), ta,
                          preferred_element_type=F32)
          + lax.dot_general(mask1.astype(BF16), _weight_pieces(rt[:, 3:4]), ta,
                            preferred_element_type=F32))
    buf[slot, :, 0:half] = _pack_pairs(xs[:, 0:half], xs[:, half:])
    buf[slot, :, half:] = _bits(ws)

    _for_each_piece(n8_ref, lrow_ref, grow_ref, i, seg_copy(slot), _start)

    @pl.when(i > 0)
    def _():
        _for_each_piece(n8_ref, lrow_ref, grow_ref, i - 1, seg_copy(1 - slot), _wait)

    for_each_tail(_wait)

    @pl.when(i == n_steps - 1)
    def _():
        _for_each_piece(n8_ref, lrow_ref, grow_ref, i, seg_copy(slot), _wait)


def _dispatch(m, rt, lstart, tables, n_rows):
    n, d = m.shape
    tm = ROW_TILE
    assert n // tm >= N_EXPERTS, "each grid step zero-fills the tail of one expert"
    width = d // 2 + LANES
    row = lambda i, *_: (i, 0)
    return pl.pallas_call(
        functools.partial(_dispatch_kernel, n // tm),
        grid_spec=pltpu.PrefetchScalarGridSpec(
            num_scalar_prefetch=len(tables),
            grid=(n // tm,),
            in_specs=[
                pl.BlockSpec((tm, d), row),
                pl.BlockSpec((tm, LANES), row),
                pl.BlockSpec((1, 1, LANES), lambda i, *_: (i, 0, 0)),
            ],
            out_specs=pl.BlockSpec(memory_space=pl.ANY),
            scratch_shapes=[
                pltpu.VMEM((2, LOCAL_ROWS, width), jnp.int32),
                pltpu.VMEM((EXPERT_TILE, width), jnp.int32),
                pltpu.SemaphoreType.DMA((2,)),
                pltpu.SemaphoreType.DMA(()),
            ],
        ),
        out_shape=jax.ShapeDtypeStruct((n_rows, width), jnp.int32),
        compiler_params=pltpu.CompilerParams(
            dimension_semantics=("arbitrary",), vmem_limit_bytes=VMEM_LIMIT),
        name="dispatch",
    )(*tables, m, rt, lstart)


def _expert_kernel(te_ref, nu_ref, xs_ref, wg_ref, wu_ref, wd_ref, os_ref, wg_s, wu_s, wd_s):
    t = pl.program_id(0)
    half = os_ref.shape[1]

    @pl.when(t < nu_ref[0])
    def _():
        @pl.when((t == 0) | (te_ref[t] != te_ref[jnp.maximum(t - 1, 0)]))
        def _():
            wg_s[...] = wg_ref[0].astype(BF16)
            wu_s[...] = wu_ref[0].astype(BF16)
            wd_s[...] = wd_ref[0].astype(BF16)

        x_lo, x_hi = _unpack_pairs(xs_ref[:, 0:half])
        w = jnp.sum(lax.bitcast_convert_type(xs_ref[:, half:], F32), axis=-1, keepdims=True)
        hg = _dot(x_lo, wg_s[0:half, :]) + _dot(x_hi, wg_s[half:, :])
        hu = _dot(x_lo, wu_s[0:half, :]) + _dot(x_hi, wu_s[half:, :])
        a = hg * jax.nn.sigmoid(hg) * hu * w
        o = _dot(a.astype(BF16), wd_s[...]).astype(BF16).astype(F32)
        os_ref[...] = _pack_pairs(o[:, 0:half], o[:, half:])

    @pl.when(t >= nu_ref[0])
    def _():
        os_ref[...] = jnp.zeros_like(os_ref)


def _experts(xs, tile_expert, n_used, w_gate, w_up, w_down):
    n_rows, width = xs.shape
    n_exp, d, f = w_gate.shape
    te = EXPERT_TILE
    used = lambda t, te_ref, nu_ref: jnp.minimum(t, nu_ref[0] - 1)
    rows = lambda t, te_ref, nu_ref: (used(t, te_ref, nu_ref), 0)
    weight = lambda t, te_ref, nu_ref: (te_ref[used(t, te_ref, nu_ref)], 0, 0)
    return pl.pallas_call(
        _expert_kernel,
        grid_spec=pltpu.PrefetchScalarGridSpec(
            num_scalar_prefetch=2,
            grid=(n_rows // te,),
            in_specs=[
                pl.BlockSpec((te, width), rows),
                pl.BlockSpec((1, d, f), weight),
                pl.BlockSpec((1, d, f), weight),
                pl.BlockSpec((1, f, d), weight),
            ],
            out_specs=pl.BlockSpec((te, d // 2), lambda t, te_ref, nu_ref: (t, 0)),
            scratch_shapes=[pltpu.VMEM((d, f), BF16), pltpu.VMEM((d, f), BF16),
                            pltpu.VMEM((f, d), BF16)],
        ),
        out_shape=jax.ShapeDtypeStruct((n_rows, d // 2), jnp.int32),
        compiler_params=pltpu.CompilerParams(
            dimension_semantics=("arbitrary",), vmem_limit_bytes=VMEM_LIMIT),
        name="experts",
    )(tile_expert, n_used, xs, w_gate, w_up, w_down)


def _moe_layout(cnt, n_pairs):
    c = cnt[:, 0, :N_EXPERTS].astype(jnp.int32)
    n_tiles = c.shape[0]
    n8 = (c + SEG_ALIGN - 1) // SEG_ALIGN
    lrow = jnp.cumsum(n8, axis=1) - n8
    units_e = jnp.sum(n8, axis=0)
    per_tile = EXPERT_TILE // SEG_ALIGN
    tiles_e = (units_e + per_tile - 1) // per_tile
    e_end = jnp.cumsum(tiles_e)
    e_off = (e_end - tiles_e) * per_tile
    grow = e_off[None, :] + jnp.cumsum(n8, axis=0) - n8
    tail0 = e_off + units_e
    tailn = tiles_e * per_tile - units_e
    worst = n_pairs + n_tiles * N_EXPERTS * (SEG_ALIGN - 1) + N_EXPERTS * (EXPERT_TILE - SEG_ALIGN)
    n_rows = -(-worst // EXPERT_TILE) * EXPERT_TILE
    tile_expert = jnp.minimum(
        jnp.searchsorted(e_end, jnp.arange(n_rows // EXPERT_TILE, dtype=jnp.int32), side="right"),
        N_EXPERTS - 1).astype(jnp.int32)
    lstart = _pad_lanes((lrow * SEG_ALIGN).astype(F32))[:, None, :]
    seg_tables = (n8.reshape(-1), lrow.reshape(-1), grow.reshape(-1))
    return seg_tables, (tail0, tailn), lstart, tile_expert, e_end[-1:], n_rows


def _combine_kernel(n8_ref, lrow_ref, grow_ref,
                    h_ref, rt_ref, ls_ref, p_ref, g_ref, wg_ref, wp_ref, gf_ref, os_hbm,
                    o_ref, buf, sems):
    i = pl.program_id(0)
    nt = pl.num_programs(0)
    slot = i % 2

    def seg_copy(slot_):
        def make(l, g, rows):
            return pltpu.make_async_copy(os_hbm.at[pl.ds(g, rows), :],
                                         buf.at[slot_, pl.ds(l, rows), :], sems.at[slot_])
        return make

    @pl.when(i == 0)
    def _():
        buf[...] = jnp.zeros_like(buf)
        _for_each_piece(n8_ref, lrow_ref, grow_ref, 0, seg_copy(0), _start)

    @pl.when(i + 1 < nt)
    def _():
        _for_each_piece(n8_ref, lrow_ref, grow_ref, i + 1, seg_copy(1 - slot), _start)

    _for_each_piece(n8_ref, lrow_ref, grow_ref, i, seg_copy(slot), _wait)

    mask0, mask1 = _slot_masks(rt_ref[...], ls_ref[0])
    o_lo, o_hi = _unpack_pairs(buf[slot])
    y = _dot((mask0 | mask1).astype(BF16), jnp.concatenate([o_lo, o_hi], axis=1))
    h = h_ref[...] + y
    gate = jax.nn.sigmoid(_dot(_rms_norm(h, g_ref[...]).astype(BF16), wg_ref[...]))
    h = h + gate * _dot(p_ref[...].astype(BF16), wp_ref[...])
    o_ref[...] = _rms_norm(h, gf_ref[...])


def _combine(h1, rt, lstart, seg_tables, o_sorted, p2, g_ple, w_gate, w_proj, g_final):
    n, d = h1.shape
    tm = ROW_TILE
    row = lambda i, *_: (i, 0)
    full2 = lambda i, *_: (0, 0)
    return pl.pallas_call(
        _combine_kernel,
        grid_spec=pltpu.PrefetchScalarGridSpec(
            num_scalar_prefetch=3,
            grid=(n // tm,),
            in_specs=[
                pl.BlockSpec((tm, d), row),
                pl.BlockSpec((tm, LANES), row),
                pl.BlockSpec((1, 1, LANES), lambda i, *_: (i, 0, 0)),
                pl.BlockSpec((tm, p2.shape[1]), row),
                pl.BlockSpec((1, d), full2),
                pl.BlockSpec(w_gate.shape, full2),
                pl.BlockSpec(w_proj.shape, full2),
                pl.BlockSpec((1, d), full2),
                pl.BlockSpec(memory_space=pl.ANY),
            ],
            out_specs=pl.BlockSpec((tm, d), row),
            scratch_shapes=[
                pltpu.VMEM((2, LOCAL_ROWS, d // 2), jnp.int32),
                pltpu.SemaphoreType.DMA((2,)),
            ],
        ),
        out_shape=jax.ShapeDtypeStruct((n, d), F32),
        compiler_params=pltpu.CompilerParams(
            dimension_semantics=("arbitrary",), vmem_limit_bytes=VMEM_LIMIT),
        name="combine",
    )(*seg_tables, h1, rt, lstart, p2, g_ple, w_gate, w_proj, g_final, o_sorted)


def _pad_lanes(a):
    return jnp.pad(a, ((0, 0), (0, LANES - a.shape[1])))


def kernel(x, p, g_mix, w_in, b_f, w_pool, s_pool, w_out, g_ffn, w_grp, b_grp, w_rt, b_rt,
           w_e_gate, w_e_up, w_e_down, g_ple, w_ple_gate, w_ple_proj, g_final):
    batch, seq_len, d = x.shape
    n = batch * seq_len
    assert w_in.shape[0] == 1, "single-layer stack only: the final norm is fused into the layer"
    i = 0
    pool_dim = s_pool.shape[1]
    attn_dim = N_HEADS * HEAD_DIM
    main = pool_dim + 3 * attn_dim
    h = x.reshape(n, d)
    w_main = w_in[i, :, :main].astype(BF16)
    w_f = _pad_lanes(jnp.tile(w_in[i, :, main:], (1, N_SPLIT)))
    b_f3 = _pad_lanes(jnp.tile(b_f[i], N_SPLIT)[None])
    pool, q, k, v = _inproj(h, g_mix[i][None], w_main, w_f, b_f3,
                            w_pool[i].astype(BF16), s_pool[i][None], seq_len)
    attn = _attention(q, k, v, batch, seq_len)
    w_router = _pad_lanes(jnp.concatenate([w_grp[i], w_rt[i]], axis=1))
    b_router = _pad_lanes(jnp.concatenate([b_grp[i], b_rt[i]])[None])
    h1, m, rt, cnt = _mix(h, pool, attn, w_out[i].astype(BF16), g_ffn[i][None],
                          w_router, b_router)
    seg_tables, tail_tables, lstart, tile_expert, n_used, n_rows = _moe_layout(cnt, 2 * n)
    x_sorted = _dispatch(m, rt, lstart, seg_tables + tail_tables + (n_used,), n_rows)
    f = w_e_gate.shape[-1]
    o_sorted = _experts(x_sorted, tile_expert, n_used,
                        w_e_gate[i].reshape(N_EXPERTS, d, f),
                        w_e_up[i].reshape(N_EXPERTS, d, f),
                        w_e_down[i].reshape(N_EXPERTS, f, d))
    out = _combine(h1, rt, lstart, seg_tables, o_sorted, p[i].reshape(n, -1), g_ple[i][None],
                   w_ple_gate[i].astype(BF16), w_ple_proj[i].astype(BF16), g_final[None])
    return out.reshape(batch, seq_len, d)
```

```python
import functools
import math

import numpy as np
import jax
import jax.numpy as jnp
from jax import lax
from jax.experimental import pallas as pl
from jax.experimental.pallas import tpu as pltpu

HEAD_DIM = 64
N_HEADS = 8
POOL_WINDOWS = (2, 4, 8, 16)
POOL_GROUP_DIM = 128
POOL_HISTORY = 16
N_GROUPS = 4
EXPERTS_PER_GROUP = 8
N_EXPERTS = N_GROUPS * EXPERTS_PER_GROUP
EPS = 1e-6
LANES = 128
ROUTER_LANE0 = N_GROUPS
NEG_INF = float("-inf")
LOG2E = math.log2(math.e)
N_SPLIT = 3

ROW_TILE = 512
ATTN_Q = 512
VMEM_LIMIT = 48 * 1024 * 1024

BF16 = jnp.bfloat16
F32 = jnp.float32


def _dot(a, b):
    return jnp.dot(a, b, preferred_element_type=F32)


def _split2(a):
    hi = a.astype(BF16)
    lo = (a - hi.astype(F32)).astype(BF16)
    return hi, lo


def _split3(a):
    hi = a.astype(BF16)
    r = a - hi.astype(F32)
    mid = r.astype(BF16)
    lo = (r - mid.astype(F32)).astype(BF16)
    return hi, mid, lo


def _dot_precise(a, w):
    a1, a2 = _split2(a)
    w1, w2 = _split2(w)
    return _dot(a1, w1) + (_dot(a1, w2) + _dot(a2, w1))


def _rms_norm(x, g):
    return x * lax.rsqrt(jnp.mean(x * x, axis=-1, keepdims=True) + EPS) * g


def _own_half_start(head):
    return 0 if head % 2 == 0 else HEAD_DIM


def _bias_placement():
    width = N_HEADS * LANES
    eq = np.zeros((LANES, width), np.float32)
    ek = np.zeros((LANES, width), np.float32)
    ones_q = np.zeros((1, width), np.float32)
    ones_k = np.zeros((1, width), np.float32)
    for h in range(N_HEADS):
        a0 = h * LANES + (HEAD_DIM - _own_half_start(h))
        for piece in range(N_SPLIT):
            eq[piece * N_HEADS + h, a0 + piece] = 1.0
            ones_q[0, a0 + N_SPLIT + piece] = 1.0
            ones_k[0, a0 + piece] = 1.0
            ek[piece * N_HEADS + h, a0 + N_SPLIT + piece] = -1.0
    return eq, ek, ones_q, ones_k


def _inproj_kernel(tiles_per_seq, x_ref, g_ref, w_ref, bf_ref, wp_ref, sp_ref,
                   eq_ref, ek_ref, oq_ref, ok_ref,
                   pool_ref, q_ref, k_ref, v_ref, carry_c, carry_u):
    i = pl.program_id(0)
    seq_tile = i % tiles_per_seq
    tm = x_ref.shape[0]
    pool_dim = pool_ref.shape[1]
    attn_dim = N_HEADS * HEAD_DIM

    @pl.when(seq_tile == 0)
    def _():
        carry_c[...] = jnp.zeros_like(carry_c)
        carry_u[...] = jnp.zeros_like(carry_u)

    a = _rms_norm(x_ref[...], g_ref[...])
    ab = a.astype(BF16)
    o0 = pool_dim
    u = _dot(ab, w_ref[:, 0:o0])
    qf = _dot(ab, w_ref[:, o0:o0 + attn_dim]) * (LOG2E * HEAD_DIM ** -0.5)
    kf = _dot(ab, w_ref[:, o0 + attn_dim:o0 + 2 * attn_dim])
    vf = _dot(ab, w_ref[:, o0 + 2 * attn_dim:o0 + 3 * attn_dim])

    fl = _dot(ab, w_ref[:, o0 + 3 * attn_dim:]) + bf_ref[...]
    lf = jnp.minimum(fl, 0.0) - jnp.log1p(jnp.exp(-jnp.abs(fl)))
    row = lax.broadcasted_iota(jnp.int32, (tm, tm), 0)
    col = lax.broadcasted_iota(jnp.int32, (tm, tm), 1)
    tril = (col <= row).astype(BF16)
    l1, l2, l3 = _split3(lf)
    c = carry_c[...] + (_dot(tril, l1) + (_dot(tril, l2) + _dot(tril, l3)))
    carry_c[...] = c[tm - 1:tm, :]
    c1, c2, c3 = _split3(c * LOG2E)
    lane = lax.broadcasted_iota(jnp.int32, (1, LANES), 1)
    pieces = jnp.where(lane < N_HEADS, c1, jnp.where(lane < 2 * N_HEADS, c2, c3))
    bias_q = _dot(pieces, eq_ref[...]) + oq_ref[...]
    bias_k = _dot(pieces, ek_ref[...]) + ok_ref[...]
    for h in range(N_HEADS):
        own = (lane >= _own_half_start(h)) & (lane < _own_half_start(h) + HEAD_DIM)
        pair = slice((h // 2) * LANES, (h // 2 + 1) * LANES)
        blk = slice(h * LANES, (h + 1) * LANES)
        q_ref[:, blk] = jnp.where(own, qf[:, pair], bias_q[:, blk]).astype(BF16)
        k_ref[:, blk] = jnp.where(own, kf[:, pair], bias_k[:, blk]).astype(BF16)
        one_col = (lane == HEAD_DIM - _own_half_start(h)).astype(F32)
        v_ref[:, blk] = jnp.where(own, vf[:, pair], one_col).astype(BF16)

    ext = jnp.concatenate([carry_u[...], u], axis=0)
    carry_u[...] = u[tm - POOL_HISTORY:, :]
    pos = (seq_tile * tm + 1 + lax.broadcasted_iota(jnp.int32, (tm, 1), 0)).astype(F32)
    for gi, w in enumerate(POOL_WINDOWS):
        lo, hi = gi * POOL_GROUP_DIM, (gi + 1) * POOL_GROUP_DIM
        s = ext[:, lo:hi]
        shift = 1
        while shift < w:
            s = s + pltpu.roll(s, shift, axis=0)
            shift *= 2
        mean = s[POOL_HISTORY:, :] / jnp.minimum(pos, float(w))
        d = mean - u[:, lo:hi]
        y = _dot(d.astype(BF16), wp_ref[gi]) * sp_ref[:, lo:hi]
        pool_ref[:, lo:hi] = y.astype(BF16)


def _inproj(x2, g_mix, w_main, b_f, w_pool, s_pool, seq_len):
    n, d = x2.shape
    tm = ROW_TILE
    pool_dim = w_pool.shape[0] * w_pool.shape[1]
    head_w = N_HEADS * LANES
    eq, ek, ones_q, ones_k = _bias_placement()
    row = lambda i: (i, 0)
    full2 = lambda i: (0, 0)
    return pl.pallas_call(
        functools.partial(_inproj_kernel, seq_len // tm),
        grid=(n // tm,),
        in_specs=[
            pl.BlockSpec((tm, d), row),
            pl.BlockSpec((1, d), full2),
            pl.BlockSpec(w_main.shape, full2),
            pl.BlockSpec((1, LANES), full2),
            pl.BlockSpec(w_pool.shape, lambda i: (0, 0, 0)),
            pl.BlockSpec((1, pool_dim), full2),
            pl.BlockSpec((LANES, head_w), full2),
            pl.BlockSpec((LANES, head_w), full2),
            pl.BlockSpec((1, head_w), full2),
            pl.BlockSpec((1, head_w), full2),
        ],
        out_specs=[
            pl.BlockSpec((tm, pool_dim), row),
            pl.BlockSpec((tm, head_w), row),
            pl.BlockSpec((tm, head_w), row),
            pl.BlockSpec((tm, head_w), row),
        ],
        out_shape=[
            jax.ShapeDtypeStruct((n, pool_dim), BF16),
            jax.ShapeDtypeStruct((n, head_w), BF16),
            jax.ShapeDtypeStruct((n, head_w), BF16),
            jax.ShapeDtypeStruct((n, head_w), BF16),
        ],
        scratch_shapes=[pltpu.VMEM((1, LANES), F32), pltpu.VMEM((POOL_HISTORY, pool_dim), F32)],
        compiler_params=pltpu.CompilerParams(
            dimension_semantics=("arbitrary",), vmem_limit_bytes=VMEM_LIMIT),
        name="inproj",
    )(x2, g_mix, w_main, b_f, w_pool, s_pool,
      jnp.asarray(eq, BF16), jnp.asarray(ek, BF16), jnp.asarray(ones_q), jnp.asarray(ones_k))


def _attend(q, k_ref, v_ref, blk, n_keys):
    tq = q.shape[0]
    past = n_keys - tq
    nt = (((1,), (1,)), ((), ()))
    s_diag = lax.dot_general(q, k_ref[past:n_keys, blk], nt, preferred_element_type=F32)
    row = lax.broadcasted_iota(jnp.int32, (tq, tq), 0)
    col = lax.broadcasted_iota(jnp.int32, (tq, tq), 1)
    s_diag = jnp.where(col <= row, s_diag, NEG_INF)
    m = jnp.max(s_diag, axis=-1, keepdims=True)
    if past:
        s_past = lax.dot_general(q, k_ref[0:past, blk], nt, preferred_element_type=F32)
        m = jnp.maximum(m, jnp.max(s_past, axis=-1, keepdims=True))
    acc = _dot(jnp.exp2(s_diag - m).astype(BF16), v_ref[past:n_keys, blk])
    if past:
        acc = acc + _dot(jnp.exp2(s_past - m).astype(BF16), v_ref[0:past, blk])
    return acc


def _attn_kernel(q_ref, k_ref, v_ref, o_ref):
    qi = pl.program_id(2)
    tq = q_ref.shape[0]
    lane = lax.broadcasted_iota(jnp.int32, (1, LANES), 1)
    for tile in range(k_ref.shape[0] // tq):
        @pl.when(qi == tile)
        def _():
            out = None
            for hh in range(2):
                blk = slice(hh * LANES, (hh + 1) * LANES)
                acc = _attend(q_ref[:, blk], k_ref, v_ref, blk, (tile + 1) * tq)
                sum_lane = HEAD_DIM - _own_half_start(hh)
                l = jnp.sum(jnp.where(lane == sum_lane, acc, 0.0), axis=-1, keepdims=True)
                o = acc / l
                out = o if out is None else jnp.where(lane < HEAD_DIM, out, o)
            o_ref[...] = out.astype(BF16)


def _attention(q, k, v, batch, seq_len):
    n = q.shape[0]
    pairs = N_HEADS // 2
    tq = ATTN_Q
    qt = seq_len // tq
    pair_block = lambda b, hp, qi: (b, hp)
    return pl.pallas_call(
        _attn_kernel,
        grid=(batch, pairs, qt),
        in_specs=[
            pl.BlockSpec((tq, 2 * LANES), lambda b, hp, qi: (b * qt + qi, hp)),
            pl.BlockSpec((seq_len, 2 * LANES), pair_block),
            pl.BlockSpec((seq_len, 2 * LANES), pair_block),
        ],
        out_specs=pl.BlockSpec((tq, LANES), lambda b, hp, qi: (b * qt + qi, hp)),
        out_shape=jax.ShapeDtypeStruct((n, pairs * LANES), BF16),
        compiler_params=pltpu.CompilerParams(
            dimension_semantics=("arbitrary", "arbitrary", "arbitrary"),
            vmem_limit_bytes=VMEM_LIMIT),
        name="attn",
    )(q, k, v)


def _route(logits):
    lane = lax.broadcasted_iota(jnp.int32, logits.shape, 1)
    big = jnp.int32(LANES)
    gl = jnp.where(lane < N_GROUPS, logits, NEG_INF)
    gmax = jnp.max(gl, axis=-1, keepdims=True)
    g_w = 1.0 / jnp.sum(jnp.exp(gl - gmax), axis=-1, keepdims=True)
    g_idx = jnp.min(jnp.where(gl == gmax, lane, big), axis=-1, keepdims=True)
    e_lo = ROUTER_LANE0 + EXPERTS_PER_GROUP * g_idx
    el = jnp.where((lane >= e_lo) & (lane < e_lo + EXPERTS_PER_GROUP), logits, NEG_INF)
    v1 = jnp.max(el, axis=-1, keepdims=True)
    i1 = jnp.min(jnp.where(el == v1, lane, big), axis=-1, keepdims=True)
    el2 = jnp.where(lane == i1, NEG_INF, el)
    v2 = jnp.max(el2, axis=-1, keepdims=True)
    i2 = jnp.min(jnp.where(el2 == v2, lane, big), axis=-1, keepdims=True)
    e2 = jnp.exp(v2 - v1)
    w1 = g_w / (1.0 + e2)
    w2 = g_w * e2 / (1.0 + e2)
    ids = [(i1 - ROUTER_LANE0).astype(F32), (i2 - ROUTER_LANE0).astype(F32), w1, w2]
    out = jnp.zeros(logits.shape, F32)
    for k, val in enumerate(ids):
        out = jnp.where(lane == k, val, out)
    return out


def _mix_kernel(x_ref, pool_ref, attn_ref, wo_ref, g_ref, wr_ref, br_ref,
                h_ref, m_ref, rt_ref, cnt_ref):
    pool_dim = pool_ref.shape[1]
    h = x_ref[...] + (_dot(pool_ref[...], wo_ref[0:pool_dim, :])
                      + _dot(attn_ref[...], wo_ref[pool_dim:, :]))
    h_ref[...] = h
    m = _rms_norm(h, g_ref[...])
    m_ref[...] = m.astype(BF16)
    logits = _dot_precise(m, wr_ref[...]) + br_ref[...]
    rt = _route(logits)
    rt_ref[...] = rt
    lane = lax.broadcasted_iota(jnp.int32, rt.shape, 1).astype(F32)
    picked = (lane == rt[:, 0:1]) | (lane == rt[:, 1:2])
    cnt_ref[0] = jnp.sum(picked.astype(F32), axis=0, keepdims=True)


def _mix(x2, pool, attn, w_out, g_ffn, w_router, b_router):
    n, d = x2.shape
    tm = ROW_TILE
    row = lambda i: (i, 0)
    full2 = lambda i: (0, 0)
    return pl.pallas_call(
        _mix_kernel,
        grid=(n // tm,),
        in_specs=[
            pl.BlockSpec((tm, d), row),
            pl.BlockSpec((tm, pool.shape[1]), row),
            pl.BlockSpec((tm, attn.shape[1]), row),
            pl.BlockSpec(w_out.shape, full2),
            pl.BlockSpec((1, d), full2),
            pl.BlockSpec(w_router.shape, full2),
            pl.BlockSpec((1, LANES), full2),
        ],
        out_specs=[
            pl.BlockSpec((tm, d), row),
            pl.BlockSpec((tm, d), row),
            pl.BlockSpec((tm, LANES), row),
            pl.BlockSpec((1, 1, LANES), lambda i: (i, 0, 0)),
        ],
        out_shape=[
            jax.ShapeDtypeStruct((n, d), F32),
            jax.ShapeDtypeStruct((n, d), BF16),
            jax.ShapeDtypeStruct((n, LANES), F32),
            jax.ShapeDtypeStruct((n // tm, 1, LANES), F32),
        ],
        compiler_params=pltpu.CompilerParams(
            dimension_semantics=("arbitrary",), vmem_limit_bytes=VMEM_LIMIT),
        name="mix",
    )(x2, pool, attn, w_out, g_ffn, w_router, b_router)


SEG_ALIGN = 16
COPY_ROWS = 2 * SEG_ALIGN
EXPERT_TILE = 256
LOCAL_ROWS = 2 * ROW_TILE + N_EXPERTS * SEG_ALIGN


def _slot_masks(rt, lstart):
    t = rt.shape[0]
    lane = lax.broadcasted_iota(jnp.int32, rt.shape, 1).astype(F32)
    oh0 = lane == rt[:, 0:1]
    oh1 = lane == rt[:, 1:2]
    row = lax.broadcasted_iota(jnp.int32, (t, t), 0)
    col = lax.broadcasted_iota(jnp.int32, (t, t), 1)
    earlier = (col < row).astype(BF16)
    base = _dot(earlier, (oh0 | oh1).astype(BF16)) + lstart
    slot0 = jnp.sum(jnp.where(oh0, base, 0.0), axis=-1, keepdims=True).astype(jnp.int32)
    slot1 = jnp.sum(jnp.where(oh1, base, 0.0), axis=-1, keepdims=True).astype(jnp.int32)
    s = lax.broadcasted_iota(jnp.int32, (1, LOCAL_ROWS), 1)
    return s == slot0, s == slot1


def _for_each_piece(n8_ref, lrow_ref, grow_ref, tile, make_copy, act):
    base = tile * N_EXPERTS

    def per_expert(e, carry):
        n = n8_ref[base + e]
        l0 = lrow_ref[base + e] * SEG_ALIGN
        g0 = grow_ref[base + e] * SEG_ALIGN

        def per_block(c, carry):
            off = c * COPY_ROWS
            act(make_copy(pl.multiple_of(l0 + off, SEG_ALIGN),
                          pl.multiple_of(g0 + off, SEG_ALIGN), COPY_ROWS))
            return carry

        lax.fori_loop(0, n // 2, per_block, 0)

        @pl.when(n % 2 == 1)
        def _():
            off = (n // 2) * COPY_ROWS
            act(make_copy(pl.multiple_of(l0 + off, SEG_ALIGN),
                          pl.multiple_of(g0 + off, SEG_ALIGN), SEG_ALIGN))

        return carry

    lax.fori_loop(0, N_EXPERTS, per_expert, 0)


def _wait_rows(total_units, make_copy):
    for b in range((LOCAL_ROWS // SEG_ALIGN).bit_length()):
        @pl.when(((total_units >> b) & 1) == 1)
        def _():
            make_copy(0, 0, SEG_ALIGN << b).wait()


def _start(copy):
    copy.start()


def _wait(copy):
    copy.wait()


def _weight_pieces(w):
    lane = lax.broadcasted_iota(jnp.int32, (1, LANES), 1)
    p1, p2, p3 = (piece.astype(F32) for piece in _split3(w))
    return jnp.where(lane == 0, p1, jnp.where(lane == 1, p2, jnp.where(lane == 2, p3, 0.0))
                     ).astype(BF16)


def _dispatch_kernel(n_steps, n8_ref, lrow_ref, grow_ref, tot_ref, tail0_ref, tailn_ref, nu_ref,
                     m_ref, rt_ref, ls_ref, xs_hbm, buf, zeros, sems, tail_sem):
    i = pl.program_id(0)
    slot = i % 2
    d = m_ref.shape[1]
    n_expert_tiles = xs_hbm.shape[0] // EXPERT_TILE

    def seg_copy(slot_):
        def make(l, g, rows):
            return pltpu.make_async_copy(buf.at[slot_, pl.ds(l, rows), :],
                                         xs_hbm.at[pl.ds(g, rows), :], sems.at[slot_])
        return make

    def for_each_tail(act):
        @pl.when(i < N_EXPERTS)
        def _():
            e = jnp.minimum(i, N_EXPERTS - 1)
            g0 = tail0_ref[e] * SEG_ALIGN

            def body(c, carry):
                g = pl.multiple_of(g0 + c * SEG_ALIGN, SEG_ALIGN)
                act(pltpu.make_async_copy(zeros.at[pl.ds(0, SEG_ALIGN), :],
                                          xs_hbm.at[pl.ds(g, SEG_ALIGN), :], tail_sem))
                return carry

            lax.fori_loop(0, tailn_ref[e], body, 0)

        for k in range(-(-n_expert_tiles // n_steps)):
            t = nu_ref[0] + i + k * n_steps

            @pl.when(t < n_expert_tiles)
            def _():
                g = pl.multiple_of(t * EXPERT_TILE, EXPERT_TILE)
                act(pltpu.make_async_copy(zeros, xs_hbm.at[pl.ds(g, EXPERT_TILE), :], tail_sem))

    @pl.when(i == 0)
    def _():
        zeros[...] = jnp.zeros_like(zeros)

    for_each_tail(_start)

    rt = rt_ref[...]
    mask0, mask1 = _slot_masks(rt, ls_ref[0])
    ta = (((0,), (0,)), ((), ()))
    perm = (mask0 | mask1).astype(BF16)
    xs = lax.dot_general(perm, m_ref[...], ta, preferred_element_type=F32)
    ws = (lax.dot_general(mask0.astype(BF16), _weight_pieces(rt[:, 2:3]), ta,
                          preferred_element_type=F32)
          + lax.dot_general(mask1.astype(BF16), _weight_pieces(rt[:, 3:4]), ta,
                            preferred_element_type=F32))
    buf[slot, :, 0:d] = xs.astype(BF16)
    buf[slot, :, d:] = ws.astype(BF16)

    _for_each_piece(n8_ref, lrow_ref, grow_ref, i, seg_copy(slot), _start)

    @pl.when(i > 0)
    def _():
        _wait_rows(tot_ref[i - 1], seg_copy(1 - slot))

    for_each_tail(_wait)

    @pl.when(i == n_steps - 1)
    def _():
        _wait_rows(tot_ref[i], seg_copy(slot))


def _dispatch(m, rt, lstart, tables, n_rows):
    n, d = m.shape
    tm = ROW_TILE
    assert n // tm >= N_EXPERTS, "each grid step zero-fills the tail of one expert"
    width = d + LANES
    row = lambda i, *_: (i, 0)
    return pl.pallas_call(
        functools.partial(_dispatch_kernel, n // tm),
        grid_spec=pltpu.PrefetchScalarGridSpec(
            num_scalar_prefetch=len(tables),
            grid=(n // tm,),
            in_specs=[
                pl.BlockSpec((tm, d), row),
                pl.BlockSpec((tm, LANES), row),
                pl.BlockSpec((1, 1, LANES), lambda i, *_: (i, 0, 0)),
            ],
            out_specs=pl.BlockSpec(memory_space=pl.ANY),
            scratch_shapes=[
                pltpu.VMEM((2, LOCAL_ROWS, width), BF16),
                pltpu.VMEM((EXPERT_TILE, width), BF16),
                pltpu.SemaphoreType.DMA((2,)),
                pltpu.SemaphoreType.DMA(()),
            ],
        ),
        out_shape=jax.ShapeDtypeStruct((n_rows, width), BF16),
        compiler_params=pltpu.CompilerParams(
            dimension_semantics=("arbitrary",), vmem_limit_bytes=VMEM_LIMIT),
        name="dispatch",
    )(*tables, m, rt, lstart)


def _expert_kernel(first_ref, count_ref, nu_ref, wg_ref, wu_ref, wd_ref, xs_hbm, os_hbm,
                   xbuf, obuf, zeros, wg_s, wu_s, wd_s, in_sems, out_sems, zero_sem):
    e = pl.program_id(0)
    d = os_hbm.shape[1]
    n = count_ref[e]
    t0 = first_ref[e]
    n_tiles = os_hbm.shape[0] // EXPERT_TILE

    def rows(t):
        return pl.ds(pl.multiple_of(t * EXPERT_TILE, EXPERT_TILE), EXPERT_TILE)

    def in_copy(k, slot):
        return pltpu.make_async_copy(xs_hbm.at[rows(t0 + k), :], xbuf.at[slot], in_sems.at[slot])

    def out_copy(k, slot):
        return pltpu.make_async_copy(obuf.at[slot], os_hbm.at[rows(t0 + k), :], out_sems.at[slot])

    def for_each_unused(act):
        for j in range(-(-n_tiles // N_EXPERTS)):
            t = nu_ref[0] + e + j * N_EXPERTS

            @pl.when(t < n_tiles)
            def _():
                act(pltpu.make_async_copy(zeros, os_hbm.at[rows(t), :], zero_sem))

    @pl.when(e == 0)
    def _():
        zeros[...] = jnp.zeros_like(zeros)

    for_each_unused(_start)

    @pl.when(n > 0)
    def _():
        in_copy(0, 0).start()
        wg_s[...] = wg_ref[0].astype(BF16)
        wu_s[...] = wu_ref[0].astype(BF16)
        wd_s[...] = wd_ref[0].astype(BF16)

    def tile(k, carry):
        slot = k % 2

        @pl.when(k + 1 < n)
        def _():
            in_copy(k + 1, 1 - slot).start()

        in_copy(k, slot).wait()

        @pl.when(k >= 2)
        def _():
            out_copy(k - 2, slot).wait()

        x = xbuf[slot, :, 0:d]
        w = jnp.sum(xbuf[slot, :, d:].astype(F32), axis=-1, keepdims=True)
        hg = _dot(x, wg_s[...])
        hu = _dot(x, wu_s[...])
        a = hg * jax.nn.sigmoid(hg) * hu * w
        obuf[slot] = _dot(a.astype(BF16), wd_s[...]).astype(BF16)
        out_copy(k, slot).start()
        return carry

    lax.fori_loop(0, n, tile, 0)

    for back in (2, 1):
        @pl.when(n >= back)
        def _():
            out_copy(n - back, (n - back) % 2).wait()

    for_each_unused(_wait)


def _experts(xs, first_tile, tile_count, n_used, w_gate, w_up, w_down):
    n_rows, width = xs.shape
    n_exp, d, f = w_gate.shape
    te = EXPERT_TILE
    weight = lambda e, *_: (e, 0, 0)
    return pl.pallas_call(
        _expert_kernel,
        grid_spec=pltpu.PrefetchScalarGridSpec(
            num_scalar_prefetch=3,
            grid=(n_exp,),
            in_specs=[
                pl.BlockSpec((1, d, f), weight),
                pl.BlockSpec((1, d, f), weight),
                pl.BlockSpec((1, f, d), weight),
                pl.BlockSpec(memory_space=pl.ANY),
            ],
            out_specs=pl.BlockSpec(memory_space=pl.ANY),
            scratch_shapes=[
                pltpu.VMEM((2, te, width), BF16),
                pltpu.VMEM((2, te, d), BF16),
                pltpu.VMEM((te, d), BF16),
                pltpu.VMEM((d, f), BF16), pltpu.VMEM((d, f), BF16), pltpu.VMEM((f, d), BF16),
                pltpu.SemaphoreType.DMA((2,)),
                pltpu.SemaphoreType.DMA((2,)),
                pltpu.SemaphoreType.DMA(()),
            ],
        ),
        out_shape=jax.ShapeDtypeStruct((n_rows, d), BF16),
        compiler_params=pltpu.CompilerParams(
            dimension_semantics=("arbitrary",), vmem_limit_bytes=VMEM_LIMIT),
        name="experts",
    )(first_tile, tile_count, n_used, w_gate, w_up, w_down, xs)


def _moe_layout(cnt, n_pairs):
    c = cnt[:, 0, :N_EXPERTS].astype(jnp.int32)
    n_tiles = c.shape[0]
    n8 = (c + SEG_ALIGN - 1) // SEG_ALIGN
    lrow = jnp.cumsum(n8, axis=1) - n8
    units_e = jnp.sum(n8, axis=0)
    per_tile = EXPERT_TILE // SEG_ALIGN
    tiles_e = (units_e + per_tile - 1) // per_tile
    e_end = jnp.cumsum(tiles_e)
    e_off = (e_end - tiles_e) * per_tile
    grow = e_off[None, :] + jnp.cumsum(n8, axis=0) - n8
    tail0 = e_off + units_e
    tailn = tiles_e * per_tile - units_e
    worst = n_pairs + n_tiles * N_EXPERTS * (SEG_ALIGN - 1) + N_EXPERTS * (EXPERT_TILE - SEG_ALIGN)
    n_rows = -(-worst // EXPERT_TILE) * EXPERT_TILE
    lstart = _pad_lanes((lrow * SEG_ALIGN).astype(F32))[:, None, :]
    seg_tables = (n8.reshape(-1), lrow.reshape(-1), grow.reshape(-1), jnp.sum(n8, axis=1))
    expert_tables = (e_end - tiles_e, tiles_e, e_end[-1:])
    return seg_tables, (tail0, tailn), lstart, expert_tables, n_rows


def _combine_kernel(n8_ref, lrow_ref, grow_ref, tot_ref,
                    h_ref, rt_ref, ls_ref, p_ref, g_ref, wg_ref, wp_ref, gf_ref, os_hbm,
                    o_ref, buf, sems):
    i = pl.program_id(0)
    nt = pl.num_programs(0)
    slot = i % 2

    def seg_copy(slot_):
        def make(l, g, rows):
            return pltpu.make_async_copy(os_hbm.at[pl.ds(g, rows), :],
                                         buf.at[slot_, pl.ds(l, rows), :], sems.at[slot_])
        return make

    @pl.when(i == 0)
    def _():
        buf[...] = jnp.zeros_like(buf)
        _for_each_piece(n8_ref, lrow_ref, grow_ref, 0, seg_copy(0), _start)

    @pl.when(i + 1 < nt)
    def _():
        _for_each_piece(n8_ref, lrow_ref, grow_ref, i + 1, seg_copy(1 - slot), _start)

    _wait_rows(tot_ref[i], seg_copy(slot))

    mask0, mask1 = _slot_masks(rt_ref[...], ls_ref[0])
    y = _dot((mask0 | mask1).astype(BF16), buf[slot])
    h = h_ref[...] + y
    gate = jax.nn.sigmoid(_dot(_rms_norm(h, g_ref[...]).astype(BF16), wg_ref[...]))
    h = h + gate * _dot(p_ref[...].astype(BF16), wp_ref[...])
    o_ref[...] = _rms_norm(h, gf_ref[...])


def _combine(h1, rt, lstart, seg_tables, o_sorted, p2, g_ple, w_gate, w_proj, g_final):
    n, d = h1.shape
    tm = ROW_TILE
    row = lambda i, *_: (i, 0)
    full2 = lambda i, *_: (0, 0)
    return pl.pallas_call(
        _combine_kernel,
        grid_spec=pltpu.PrefetchScalarGridSpec(
            num_scalar_prefetch=len(seg_tables),
            grid=(n // tm,),
            in_specs=[
                pl.BlockSpec((tm, d), row),
                pl.BlockSpec((tm, LANES), row),
                pl.BlockSpec((1, 1, LANES), lambda i, *_: (i, 0, 0)),
                pl.BlockSpec((tm, p2.shape[1]), row),
                pl.BlockSpec((1, d), full2),
                pl.BlockSpec(w_gate.shape, full2),
                pl.BlockSpec(w_proj.shape, full2),
                pl.BlockSpec((1, d), full2),
                pl.BlockSpec(memory_space=pl.ANY),
            ],
            out_specs=pl.BlockSpec((tm, d), row),
            scratch_shapes=[
                pltpu.VMEM((2, LOCAL_ROWS, d), BF16),
                pltpu.SemaphoreType.DMA((2,)),
            ],
        ),
        out_shape=jax.ShapeDtypeStruct((n, d), F32),
        compiler_params=pltpu.CompilerParams(
            dimension_semantics=("arbitrary",), vmem_limit_bytes=VMEM_LIMIT),
        name="combine",
    )(*seg_tables, h1, rt, lstart, p2, g_ple, w_gate, w_proj, g_final, o_sorted)


def _pad_lanes(a):
    return jnp.pad(a, ((0, 0), (0, LANES - a.shape[1])))


def kernel(x, p, g_mix, w_in, b_f, w_pool, s_pool, w_out, g_ffn, w_grp, b_grp, w_rt, b_rt,
           w_e_gate, w_e_up, w_e_down, g_ple, w_ple_gate, w_ple_proj, g_final):
    batch, seq_len, d = x.shape
    n = batch * seq_len
    assert w_in.shape[0] == 1, "single-layer stack only: the final norm is fused into the layer"
    i = 0
    pool_dim = s_pool.shape[1]
    attn_dim = N_HEADS * HEAD_DIM
    main = pool_dim + 3 * attn_dim
    h = x.reshape(n, d)
    w_f = _pad_lanes(jnp.tile(w_in[i, :, main:], (1, N_SPLIT)))
    w_main = jnp.concatenate([w_in[i, :, :main], w_f], axis=1).astype(BF16)
    b_f3 = _pad_lanes(jnp.tile(b_f[i], N_SPLIT)[None])
    pool, q, k, v = _inproj(h, g_mix[i][None], w_main, b_f3,
                            w_pool[i].astype(BF16), s_pool[i][None], seq_len)
    attn = _attention(q, k, v, batch, seq_len)
    w_router = _pad_lanes(jnp.concatenate([w_grp[i], w_rt[i]], axis=1))
    b_router = _pad_lanes(jnp.concatenate([b_grp[i], b_rt[i]])[None])
    h1, m, rt, cnt = _mix(h, pool, attn, w_out[i].astype(BF16), g_ffn[i][None],
                          w_router, b_router)
    seg_tables, tail_tables, lstart, expert_tables, n_rows = _moe_layout(cnt, 2 * n)
    x_sorted = _dispatch(m, rt, lstart, seg_tables + tail_tables + expert_tables[2:], n_rows)
    f = w_e_gate.shape[-1]
    o_sorted = _experts(x_sorted, *expert_tables,
                        w_e_gate[i].reshape(N_EXPERTS, d, f),
                        w_e_up[i].reshape(N_EXPERTS, d, f),
                        w_e_down[i].reshape(N_EXPERTS, f, d))
    out = _combine(h1, rt, lstart, seg_tables, o_sorted, p[i].reshape(n, -1), g_ple[i][None],
                   w_ple_gate[i].astype(BF16), w_ple_proj[i].astype(BF16), g_final[None])
    return out.reshape(batch, seq_len, d)
```

```python
import functools
import math

import numpy as np
import jax
import jax.numpy as jnp
from jax import lax
from jax.experimental import pallas as pl
from jax.experimental.pallas import tpu as pltpu

HEAD_DIM = 64
N_HEADS = 8
POOL_WINDOWS = (2, 4, 8, 16)
POOL_GROUP_DIM = 128
POOL_HISTORY = 16
N_GROUPS = 4
EXPERTS_PER_GROUP = 8
N_EXPERTS = N_GROUPS * EXPERTS_PER_GROUP
EPS = 1e-6
LANES = 128
ROUTER_LANE0 = N_GROUPS
NEG_INF = float("-inf")
LOG2E = math.log2(math.e)
N_SPLIT = 3

ROW_TILE = 512
ATTN_Q = 512
VMEM_LIMIT = 48 * 1024 * 1024

BF16 = jnp.bfloat16
F32 = jnp.float32


def _dot(a, b):
    return jnp.dot(a, b, preferred_element_type=F32)


def _split2(a):
    hi = a.astype(BF16)
    lo = (a - hi.astype(F32)).astype(BF16)
    return hi, lo


def _split3(a):
    hi = a.astype(BF16)
    r = a - hi.astype(F32)
    mid = r.astype(BF16)
    lo = (r - mid.astype(F32)).astype(BF16)
    return hi, mid, lo


def _dot_precise(a, w):
    a1, a2 = _split2(a)
    w1, w2 = _split2(w)
    return _dot(a1, w1) + (_dot(a1, w2) + _dot(a2, w1))


def _row_parts(rows, n_parts=2):
    step = rows // n_parts
    return [slice(k * step, (k + 1) * step) for k in range(n_parts)]


def _rms_norm(x, g):
    return x * lax.rsqrt(jnp.mean(x * x, axis=-1, keepdims=True) + EPS) * g


def _own_half_start(head):
    return 0 if head % 2 == 0 else HEAD_DIM


def _bias_placement():
    width = N_HEADS * LANES
    eq = np.zeros((LANES, width), np.float32)
    ek = np.zeros((LANES, width), np.float32)
    ones_q = np.zeros((1, width), np.float32)
    ones_k = np.zeros((1, width), np.float32)
    for h in range(N_HEADS):
        a0 = h * LANES + (HEAD_DIM - _own_half_start(h))
        for piece in range(N_SPLIT):
            eq[piece * N_HEADS + h, a0 + piece] = 1.0
            ones_q[0, a0 + N_SPLIT + piece] = 1.0
            ones_k[0, a0 + piece] = 1.0
            ek[piece * N_HEADS + h, a0 + N_SPLIT + piece] = -1.0
    return eq, ek, ones_q, ones_k


def _inproj_kernel(tiles_per_seq, x_ref, g_ref, w_ref, bf_ref, wp_ref, sp_ref,
                   eq_ref, ek_ref, oq_ref, ok_ref,
                   pool_ref, q_ref, k_ref, v_ref, carry_c, carry_u):
    i = pl.program_id(0)
    seq_tile = i % tiles_per_seq
    tm = x_ref.shape[0]
    pool_dim = pool_ref.shape[1]
    attn_dim = N_HEADS * HEAD_DIM

    @pl.when(seq_tile == 0)
    def _():
        carry_c[...] = jnp.zeros_like(carry_c)
        carry_u[...] = jnp.zeros_like(carry_u)

    a = _rms_norm(x_ref[...], g_ref[...])
    ab = a.astype(BF16)
    o0 = pool_dim
    u = _dot(ab, w_ref[:, 0:o0])
    qf = _dot(ab, w_ref[:, o0:o0 + attn_dim]) * (LOG2E * HEAD_DIM ** -0.5)
    kf = _dot(ab, w_ref[:, o0 + attn_dim:o0 + 2 * attn_dim])
    vf = _dot(ab, w_ref[:, o0 + 2 * attn_dim:o0 + 3 * attn_dim])

    fl = _dot(ab, w_ref[:, o0 + 3 * attn_dim:]) + bf_ref[...]
    lf = jnp.minimum(fl, 0.0) - jnp.log1p(jnp.exp(-jnp.abs(fl)))
    row = lax.broadcasted_iota(jnp.int32, (tm, tm), 0)
    col = lax.broadcasted_iota(jnp.int32, (tm, tm), 1)
    tril = (col <= row).astype(BF16)
    sums = _dot(tril, jnp.concatenate(_split3(lf), axis=1))
    c = carry_c[...] + (sums[:, 0:LANES] + (sums[:, LANES:2 * LANES] + sums[:, 2 * LANES:]))
    carry_c[...] = c[tm - 1:tm, :]
    c1, c2, c3 = _split3(c * LOG2E)
    lane = lax.broadcasted_iota(jnp.int32, (1, LANES), 1)
    pieces = jnp.where(lane < N_HEADS, c1, jnp.where(lane < 2 * N_HEADS, c2, c3))
    bias_q = _dot(pieces, eq_ref[...]) + oq_ref[...]
    bias_k = _dot(pieces, ek_ref[...]) + ok_ref[...]
    for h in range(N_HEADS):
        own = (lane >= _own_half_start(h)) & (lane < _own_half_start(h) + HEAD_DIM)
        pair = slice((h // 2) * LANES, (h // 2 + 1) * LANES)
        blk = slice(h * LANES, (h + 1) * LANES)
        q_ref[:, blk] = jnp.where(own, qf[:, pair], bias_q[:, blk]).astype(BF16)
        k_ref[:, blk] = jnp.where(own, kf[:, pair], bias_k[:, blk]).astype(BF16)
        one_col = (lane == HEAD_DIM - _own_half_start(h)).astype(F32)
        v_ref[:, blk] = jnp.where(own, vf[:, pair], one_col).astype(BF16)

    ext = jnp.concatenate([carry_u[...], u], axis=0)
    carry_u[...] = u[tm - POOL_HISTORY:, :]
    pos = (seq_tile * tm + 1 + lax.broadcasted_iota(jnp.int32, (tm, 1), 0)).astype(F32)
    for gi, w in enumerate(POOL_WINDOWS):
        lo, hi = gi * POOL_GROUP_DIM, (gi + 1) * POOL_GROUP_DIM
        s = ext[:, lo:hi]
        shift = 1
        while shift < w:
            s = s + pltpu.roll(s, shift, axis=0)
            shift *= 2
        mean = s[POOL_HISTORY:, :] / jnp.minimum(pos, float(w))
        d = mean - u[:, lo:hi]
        y = _dot(d.astype(BF16), wp_ref[gi]) * sp_ref[:, lo:hi]
        pool_ref[:, lo:hi] = y.astype(BF16)


def _inproj(x2, g_mix, w_main, b_f, w_pool, s_pool, seq_len):
    n, d = x2.shape
    tm = ROW_TILE
    pool_dim = w_pool.shape[0] * w_pool.shape[1]
    head_w = N_HEADS * LANES
    eq, ek, ones_q, ones_k = _bias_placement()
    row = lambda i: (i, 0)
    full2 = lambda i: (0, 0)
    return pl.pallas_call(
        functools.partial(_inproj_kernel, seq_len // tm),
        grid=(n // tm,),
        in_specs=[
            pl.BlockSpec((tm, d), row),
            pl.BlockSpec((1, d), full2),
            pl.BlockSpec(w_main.shape, full2),
            pl.BlockSpec((1, LANES), full2),
            pl.BlockSpec(w_pool.shape, lambda i: (0, 0, 0)),
            pl.BlockSpec((1, pool_dim), full2),
            pl.BlockSpec((LANES, head_w), full2),
            pl.BlockSpec((LANES, head_w), full2),
            pl.BlockSpec((1, head_w), full2),
            pl.BlockSpec((1, head_w), full2),
        ],
        out_specs=[
            pl.BlockSpec((tm, pool_dim), row),
            pl.BlockSpec((tm, head_w), row),
            pl.BlockSpec((tm, head_w), row),
            pl.BlockSpec((tm, head_w), row),
        ],
        out_shape=[
            jax.ShapeDtypeStruct((n, pool_dim), BF16),
            jax.ShapeDtypeStruct((n, head_w), BF16),
            jax.ShapeDtypeStruct((n, head_w), BF16),
            jax.ShapeDtypeStruct((n, head_w), BF16),
        ],
        scratch_shapes=[pltpu.VMEM((1, LANES), F32), pltpu.VMEM((POOL_HISTORY, pool_dim), F32)],
        compiler_params=pltpu.CompilerParams(
            dimension_semantics=("arbitrary",), vmem_limit_bytes=VMEM_LIMIT),
        name="inproj",
    )(x2, g_mix, w_main, b_f, w_pool, s_pool,
      jnp.asarray(eq, BF16), jnp.asarray(ek, BF16), jnp.asarray(ones_q), jnp.asarray(ones_k))


def _attend_pair(q_ref, k_ref, v_ref, n_keys):
    tq = q_ref.shape[0]
    past = n_keys - tq
    nt = (((1,), (1,)), ((), ()))
    blks = [slice(hh * LANES, (hh + 1) * LANES) for hh in range(2)]
    row = lax.broadcasted_iota(jnp.int32, (tq, tq), 0)
    col = lax.broadcasted_iota(jnp.int32, (tq, tq), 1)
    qs = [q_ref[:, blk] for blk in blks]
    s_diag = [jnp.where(col <= row,
                        lax.dot_general(q, k_ref[past:n_keys, blk], nt,
                                        preferred_element_type=F32), NEG_INF)
              for q, blk in zip(qs, blks)]
    m = [jnp.max(s, axis=-1, keepdims=True) for s in s_diag]
    if past:
        s_past = [lax.dot_general(q, k_ref[0:past, blk], nt, preferred_element_type=F32)
                  for q, blk in zip(qs, blks)]
        m = [jnp.maximum(mi, jnp.max(s, axis=-1, keepdims=True)) for mi, s in zip(m, s_past)]
    acc = [_dot(jnp.exp2(s - mi).astype(BF16), v_ref[past:n_keys, blk])
           for s, mi, blk in zip(s_diag, m, blks)]
    if past:
        acc = [a + _dot(jnp.exp2(s - mi).astype(BF16), v_ref[0:past, blk])
               for a, s, mi, blk in zip(acc, s_past, m, blks)]
    return acc


def _attn_kernel(q_ref, k_ref, v_ref, o_ref):
    qi = pl.program_id(2)
    tq = q_ref.shape[0]
    lane = lax.broadcasted_iota(jnp.int32, (1, LANES), 1)
    for tile in range(k_ref.shape[0] // tq):
        @pl.when(qi == tile)
        def _():
            outs = []
            for hh, acc in enumerate(_attend_pair(q_ref, k_ref, v_ref, (tile + 1) * tq)):
                sum_lane = HEAD_DIM - _own_half_start(hh)
                l = jnp.sum(jnp.where(lane == sum_lane, acc, 0.0), axis=-1, keepdims=True)
                outs.append(acc / l)
            o_ref[...] = jnp.where(lane < HEAD_DIM, outs[0], outs[1]).astype(BF16)


def _attention(q, k, v, batch, seq_len):
    n = q.shape[0]
    pairs = N_HEADS // 2
    tq = ATTN_Q
    qt = seq_len // tq
    pair_block = lambda b, hp, qi: (b, hp)
    return pl.pallas_call(
        _attn_kernel,
        grid=(batch, pairs, qt),
        in_specs=[
            pl.BlockSpec((tq, 2 * LANES), lambda b, hp, qi: (b * qt + qi, hp)),
            pl.BlockSpec((seq_len, 2 * LANES), pair_block),
            pl.BlockSpec((seq_len, 2 * LANES), pair_block),
        ],
        out_specs=pl.BlockSpec((tq, LANES), lambda b, hp, qi: (b * qt + qi, hp)),
        out_shape=jax.ShapeDtypeStruct((n, pairs * LANES), BF16),
        compiler_params=pltpu.CompilerParams(
            dimension_semantics=("arbitrary", "arbitrary", "arbitrary"),
            vmem_limit_bytes=VMEM_LIMIT),
        name="attn",
    )(q, k, v)


def _route(logits):
    lane = lax.broadcasted_iota(jnp.int32, logits.shape, 1)
    big = jnp.int32(LANES)
    gl = jnp.where(lane < N_GROUPS, logits, NEG_INF)
    gmax = jnp.max(gl, axis=-1, keepdims=True)
    g_w = 1.0 / jnp.sum(jnp.exp(gl - gmax), axis=-1, keepdims=True)
    g_idx = jnp.min(jnp.where(gl == gmax, lane, big), axis=-1, keepdims=True)
    e_lo = ROUTER_LANE0 + EXPERTS_PER_GROUP * g_idx
    el = jnp.where((lane >= e_lo) & (lane < e_lo + EXPERTS_PER_GROUP), logits, NEG_INF)
    v1 = jnp.max(el, axis=-1, keepdims=True)
    i1 = jnp.min(jnp.where(el == v1, lane, big), axis=-1, keepdims=True)
    el2 = jnp.where(lane == i1, NEG_INF, el)
    v2 = jnp.max(el2, axis=-1, keepdims=True)
    i2 = jnp.min(jnp.where(el2 == v2, lane, big), axis=-1, keepdims=True)
    e2 = jnp.exp(v2 - v1)
    w1 = g_w / (1.0 + e2)
    w2 = g_w * e2 / (1.0 + e2)
    return (i1 - ROUTER_LANE0).astype(F32), (i2 - ROUTER_LANE0).astype(F32), w1, w2


EXT_W0 = 0
EXT_W1 = N_SPLIT
EXT_E0 = 2 * N_SPLIT


def _routing_record(e0, e1, w0, w1):
    lane = lax.broadcasted_iota(jnp.int32, (1, LANES), 1)
    rec = jnp.where(lane == EXT_E0, e0, jnp.where(lane == EXT_E0 + 1, e1, 0.0))
    for first, w in ((EXT_W0, w0), (EXT_W1, w1)):
        for k, piece in enumerate(_split3(w)):
            rec = jnp.where(lane == first + k, piece.astype(F32), rec)
    return rec.astype(BF16)


def _local_slots(e0, e1):
    t = e0.shape[0]
    lane = lax.broadcasted_iota(jnp.int32, (t, LANES), 1).astype(F32)
    oh0 = lane == e0
    oh1 = lane == e1
    picked = (oh0 | oh1).astype(BF16)
    cnt = jnp.sum(picked.astype(F32), axis=0, keepdims=True)
    units = jnp.floor((cnt + (SEG_ALIGN - 1)) * (1.0 / SEG_ALIGN))
    r128 = lax.broadcasted_iota(jnp.int32, (LANES, LANES), 0)
    c128 = lax.broadcasted_iota(jnp.int32, (LANES, LANES), 1)
    before = (r128 < c128).astype(BF16)
    lstart = SEG_ALIGN * _dot(jnp.broadcast_to(units, (8, LANES)).astype(BF16), before)[0:1, :]
    row = lax.broadcasted_iota(jnp.int32, (t, t), 0)
    col = lax.broadcasted_iota(jnp.int32, (t, t), 1)
    earlier = (col < row).astype(BF16)
    base = _dot(earlier, picked) + lstart
    slot0 = jnp.sum(jnp.where(oh0, base, 0.0), axis=-1, keepdims=True)
    slot1 = jnp.sum(jnp.where(oh1, base, 0.0), axis=-1, keepdims=True)
    return slot0, slot1, cnt


def _mix_kernel(x_ref, pool_ref, attn_ref, wo_ref, g_ref, wr_ref, br_ref,
                h_ref, m_ref, ext_ref, slots_ref, slots_t_ref, cnt_ref, logits_s):
    pool_dim = pool_ref.shape[1]

    @pl.when(pl.program_id(0) == 0)
    def _():
        logits_s[...] = jnp.zeros_like(logits_s)

    prev_logits = logits_s[...]
    h = x_ref[...] + (_dot(pool_ref[...], wo_ref[0:pool_dim, :])
                      + _dot(attn_ref[...], wo_ref[pool_dim:, :]))
    h_ref[...] = h
    e0, e1, w0, w1 = _route(prev_logits)
    ext_ref[...] = _routing_record(e0, e1, w0, w1)
    slot0, slot1, cnt = _local_slots(e0, e1)
    cnt_ref[0] = cnt
    lane = lax.broadcasted_iota(jnp.int32, (1, LANES), 1)
    slots = jnp.where(lane == 0, slot0, jnp.where(lane == 1, slot1, 0.0))
    slots_ref[...] = slots
    slots_t_ref[0] = slots.T[0:8, :]
    m = _rms_norm(h, g_ref[...])
    m_ref[...] = m.astype(BF16)
    logits_s[...] = _dot_precise(m, wr_ref[...]) + br_ref[...]


def _mix(x2, pool, attn, w_out, g_ffn, w_router, b_router):
    n, d = x2.shape
    tm = ROW_TILE
    n_tiles = n // tm
    row = lambda i: (jnp.minimum(i, n_tiles - 1), 0)
    routed = lambda i: (jnp.maximum(i - 1, 0), 0)
    full2 = lambda i: (0, 0)
    return pl.pallas_call(
        _mix_kernel,
        grid=(n_tiles + 1,),
        in_specs=[
            pl.BlockSpec((tm, d), row),
            pl.BlockSpec((tm, pool.shape[1]), row),
            pl.BlockSpec((tm, attn.shape[1]), row),
            pl.BlockSpec(w_out.shape, full2),
            pl.BlockSpec((1, d), full2),
            pl.BlockSpec(w_router.shape, full2),
            pl.BlockSpec((1, LANES), full2),
        ],
        out_specs=[
            pl.BlockSpec((tm, d), row),
            pl.BlockSpec((tm, d), row),
            pl.BlockSpec((tm, LANES), routed),
            pl.BlockSpec((tm, LANES), routed),
            pl.BlockSpec((1, 8, tm), lambda i: (jnp.maximum(i - 1, 0), 0, 0)),
            pl.BlockSpec((1, 1, LANES), lambda i: (jnp.maximum(i - 1, 0), 0, 0)),
        ],
        out_shape=[
            jax.ShapeDtypeStruct((n, d), F32),
            jax.ShapeDtypeStruct((n, d), BF16),
            jax.ShapeDtypeStruct((n, LANES), BF16),
            jax.ShapeDtypeStruct((n, LANES), F32),
            jax.ShapeDtypeStruct((n_tiles, 8, tm), F32),
            jax.ShapeDtypeStruct((n_tiles, 1, LANES), F32),
        ],
        scratch_shapes=[pltpu.VMEM((tm, LANES), F32)],
        compiler_params=pltpu.CompilerParams(
            dimension_semantics=("arbitrary",), vmem_limit_bytes=VMEM_LIMIT),
        name="mix",
    )(x2, pool, attn, w_out, g_ffn, w_router, b_router)


SEG_ALIGN = 16
COPY_UNITS = 4
COPY_ROWS = COPY_UNITS * SEG_ALIGN
EXPERT_TILE = 512
LOCAL_ROWS = 2 * ROW_TILE + N_EXPERTS * SEG_ALIGN


def _permutation(slot0, slot1, slots_axis):
    shape = (1, LOCAL_ROWS) if slots_axis == 1 else (LOCAL_ROWS, 1)
    s = lax.broadcasted_iota(jnp.int32, shape, slots_axis)
    return ((s == slot0.astype(jnp.int32)) | (s == slot1.astype(jnp.int32))).astype(BF16)


def _for_each_piece(n8_ref, lrow_ref, grow_ref, tile, make_copy, act):
    base = tile * N_EXPERTS

    def per_expert(e, carry):
        n = n8_ref[base + e]
        l0 = lrow_ref[base + e] * SEG_ALIGN
        g0 = grow_ref[base + e] * SEG_ALIGN

        def piece(off, rows):
            act(make_copy(pl.multiple_of(l0 + off, SEG_ALIGN),
                          pl.multiple_of(g0 + off, SEG_ALIGN), rows))

        n_full = n // COPY_UNITS

        def per_block(c, carry):
            piece(c * COPY_ROWS, COPY_ROWS)
            return carry

        lax.fori_loop(0, n_full, per_block, 0)
        for b in reversed(range(COPY_UNITS.bit_length() - 1)):
            @pl.when(((n >> b) & 1) == 1)
            def _():
                higher = (n >> (b + 1)) << (b + 1)
                piece(higher * SEG_ALIGN, SEG_ALIGN << b)

        return carry

    lax.fori_loop(0, N_EXPERTS, per_expert, 0)


def _wait_rows(total_units, make_copy):
    for b in range((LOCAL_ROWS // SEG_ALIGN).bit_length()):
        @pl.when(((total_units >> b) & 1) == 1)
        def _():
            make_copy(0, 0, SEG_ALIGN << b).wait()


def _start(copy):
    copy.start()


def _wait(copy):
    copy.wait()


def _dispatch_kernel(n_steps, n8_ref, lrow_ref, grow_ref, tot_ref, tail0_ref, tailn_ref, nu_ref,
                     m_ref, ext_ref, st_ref, st_next_ref, xs_hbm,
                     buf, zeros, perm_even, perm_odd, sems, tail_sem):
    i = pl.program_id(0)
    slot = i % 2
    d = m_ref.shape[1]
    n_expert_tiles = xs_hbm.shape[0] // EXPERT_TILE

    def seg_copy(slot_):
        def make(l, g, rows):
            return pltpu.make_async_copy(buf.at[slot_, pl.ds(l, rows), :],
                                         xs_hbm.at[pl.ds(g, rows), :], sems.at[slot_])
        return make

    def for_each_tail(act):
        @pl.when(i < N_EXPERTS)
        def _():
            e = jnp.minimum(i, N_EXPERTS - 1)
            g0 = tail0_ref[e] * SEG_ALIGN

            def body(c, carry):
                g = pl.multiple_of(g0 + c * SEG_ALIGN, SEG_ALIGN)
                act(pltpu.make_async_copy(zeros.at[pl.ds(0, SEG_ALIGN), :],
                                          xs_hbm.at[pl.ds(g, SEG_ALIGN), :], tail_sem))
                return carry

            lax.fori_loop(0, tailn_ref[e], body, 0)

        for k in range(-(-n_expert_tiles // n_steps)):
            t = nu_ref[0] + i + k * n_steps

            @pl.when(t < n_expert_tiles)
            def _():
                g = pl.multiple_of(t * EXPERT_TILE, EXPERT_TILE)
                act(pltpu.make_async_copy(zeros, xs_hbm.at[pl.ds(g, EXPERT_TILE), :], tail_sem))

    @pl.when(i == 0)
    def _():
        zeros[...] = jnp.zeros_like(zeros)

    for_each_tail(_start)

    def build(slots_t, perm):
        perm[...] = _permutation(slots_t[0, 0:1, :], slots_t[0, 1:2, :], slots_axis=0)

    @pl.when(i == 0)
    def _():
        build(st_ref, perm_even)

    def permute(perm, next_perm):
        p = perm[...]
        build(st_next_ref, next_perm)
        buf[slot, :, 0:d] = _dot(p, m_ref[...]).astype(BF16)
        buf[slot, :, d:] = _dot(p, ext_ref[...]).astype(BF16)

    @pl.when(slot == 0)
    def _():
        permute(perm_even, perm_odd)

    @pl.when(slot == 1)
    def _():
        permute(perm_odd, perm_even)

    _for_each_piece(n8_ref, lrow_ref, grow_ref, i, seg_copy(slot), _start)

    @pl.when(i > 0)
    def _():
        _wait_rows(tot_ref[i - 1], seg_copy(1 - slot))

    for_each_tail(_wait)

    @pl.when(i == n_steps - 1)
    def _():
        _wait_rows(tot_ref[i], seg_copy(slot))


def _dispatch(m, ext, slots_t, tables, n_rows):
    n, d = m.shape
    tm = ROW_TILE
    assert n // tm >= N_EXPERTS, "each grid step zero-fills the tail of one expert"
    width = d + LANES
    last = n // tm - 1
    row = lambda i, *_: (i, 0)
    return pl.pallas_call(
        functools.partial(_dispatch_kernel, n // tm),
        grid_spec=pltpu.PrefetchScalarGridSpec(
            num_scalar_prefetch=len(tables),
            grid=(n // tm,),
            in_specs=[
                pl.BlockSpec((tm, d), row),
                pl.BlockSpec((tm, LANES), row),
                pl.BlockSpec((1, 8, tm), lambda i, *_: (i, 0, 0)),
                pl.BlockSpec((1, 8, tm), lambda i, *_: (jnp.minimum(i + 1, last), 0, 0)),
            ],
            out_specs=pl.BlockSpec(memory_space=pl.ANY),
            scratch_shapes=[
                pltpu.VMEM((2, LOCAL_ROWS, width), BF16),
                pltpu.VMEM((EXPERT_TILE, width), BF16),
                pltpu.VMEM((LOCAL_ROWS, tm), BF16),
                pltpu.VMEM((LOCAL_ROWS, tm), BF16),
                pltpu.SemaphoreType.DMA((2,)),
                pltpu.SemaphoreType.DMA(()),
            ],
        ),
        out_shape=jax.ShapeDtypeStruct((n_rows, width), BF16),
        compiler_params=pltpu.CompilerParams(
            dimension_semantics=("arbitrary",), vmem_limit_bytes=VMEM_LIMIT),
        name="dispatch",
    )(*tables, m, ext, slots_t, slots_t)


def _expert_kernel(first_ref, count_ref, nu_ref, wg_ref, wu_ref, wd_ref, xs_hbm, os_hbm,
                   xbuf, obuf, zeros, wgu_s, wd_s, in_sems, out_sems, zero_sem):
    e = pl.program_id(0)
    d = os_hbm.shape[1]
    f = wg_ref.shape[2]
    n = count_ref[e]
    t0 = first_ref[e]
    n_tiles = os_hbm.shape[0] // EXPERT_TILE

    def rows(t):
        return pl.ds(pl.multiple_of(t * EXPERT_TILE, EXPERT_TILE), EXPERT_TILE)

    def in_copy(k, slot, first=t0):
        return pltpu.make_async_copy(xs_hbm.at[rows(first + k), :], xbuf.at[slot],
                                     in_sems.at[slot])

    def out_copy(k, slot):
        return pltpu.make_async_copy(obuf.at[slot], os_hbm.at[rows(t0 + k), :], out_sems.at[slot])

    def for_each_unused(act):
        for j in range(-(-n_tiles // N_EXPERTS)):
            t = nu_ref[0] + e + j * N_EXPERTS

            @pl.when(t < n_tiles)
            def _():
                act(pltpu.make_async_copy(zeros, os_hbm.at[rows(t), :], zero_sem))

    @pl.when(e == 0)
    def _():
        zeros[...] = jnp.zeros_like(zeros)

    for_each_unused(_start)

    @pl.when((e == 0) & (n > 0))
    def _():
        in_copy(0, 0).start()

    @pl.when(n > 0)
    def _():
        wgu_s[:, 0:f] = wg_ref[0].astype(BF16)
        wgu_s[:, f:] = wu_ref[0].astype(BF16)
        wd_s[...] = wd_ref[0].astype(BF16)

    def tile(k, carry):
        slot = k % 2

        @pl.when(k + 1 < n)
        def _():
            in_copy(k + 1, 1 - slot).start()

        in_copy(k, slot).wait()

        @pl.when(k >= 2)
        def _():
            out_copy(k - 2, slot).wait()

        x = xbuf[slot, :, 0:d]
        rec = xbuf[slot, :, d:].astype(F32)
        lane = lax.broadcasted_iota(jnp.int32, (1, LANES), 1)

        def lanes_sum(first, count):
            keep = (lane >= first) & (lane < first + count)
            return jnp.sum(jnp.where(keep, rec, 0.0), axis=-1, keepdims=True)

        first_choice = lanes_sum(EXT_E0, 1) == e.astype(F32)
        w = jnp.where(first_choice, lanes_sum(EXT_W0, N_SPLIT), lanes_sum(EXT_W1, N_SPLIT))
        h = _dot(x, wgu_s[...])
        hg = h[:, 0:f]
        hu = h[:, f:]
        a = hg * jax.nn.sigmoid(hg) * hu * w
        obuf[slot] = _dot(a.astype(BF16), wd_s[...]).astype(BF16)
        out_copy(k, slot).start()
        return carry

    lax.fori_loop(0, n, tile, 0)

    for back in (2, 1):
        @pl.when(n >= back)
        def _():
            out_copy(n - back, (n - back) % 2).wait()

    nxt = jnp.minimum(e + 1, N_EXPERTS - 1)

    @pl.when((e + 1 < N_EXPERTS) & (count_ref[nxt] > 0))
    def _():
        in_copy(0, 0, first_ref[nxt]).start()

    for_each_unused(_wait)


def _experts(xs, first_tile, tile_count, n_used, w_gate, w_up, w_down):
    n_rows, width = xs.shape
    n_exp, d, f = w_gate.shape
    te = EXPERT_TILE
    weight = lambda e, *_: (e, 0, 0)
    return pl.pallas_call(
        _expert_kernel,
        grid_spec=pltpu.PrefetchScalarGridSpec(
            num_scalar_prefetch=3,
            grid=(n_exp,),
            in_specs=[
                pl.BlockSpec((1, d, f), weight),
                pl.BlockSpec((1, d, f), weight),
                pl.BlockSpec((1, f, d), weight),
                pl.BlockSpec(memory_space=pl.ANY),
            ],
            out_specs=pl.BlockSpec(memory_space=pl.ANY),
            scratch_shapes=[
                pltpu.VMEM((2, te, width), BF16),
                pltpu.VMEM((2, te, d), BF16),
                pltpu.VMEM((te, d), BF16),
                pltpu.VMEM((d, 2 * f), BF16), pltpu.VMEM((f, d), BF16),
                pltpu.SemaphoreType.DMA((2,)),
                pltpu.SemaphoreType.DMA((2,)),
                pltpu.SemaphoreType.DMA(()),
            ],
        ),
        out_shape=jax.ShapeDtypeStruct((n_rows, d), BF16),
        compiler_params=pltpu.CompilerParams(
            dimension_semantics=("arbitrary",), vmem_limit_bytes=VMEM_LIMIT),
        name="experts",
    )(first_tile, tile_count, n_used, w_gate, w_up, w_down, xs)


def _moe_layout(cnt, n_pairs):
    c = cnt[:, 0, :N_EXPERTS].astype(jnp.int32)
    n_tiles = c.shape[0]
    n8 = (c + SEG_ALIGN - 1) // SEG_ALIGN
    lrow = jnp.cumsum(n8, axis=1) - n8
    units_e = jnp.sum(n8, axis=0)
    per_tile = EXPERT_TILE // SEG_ALIGN
    tiles_e = (units_e + per_tile - 1) // per_tile
    e_end = jnp.cumsum(tiles_e)
    e_off = (e_end - tiles_e) * per_tile
    grow = e_off[None, :] + jnp.cumsum(n8, axis=0) - n8
    tail0 = e_off + units_e
    tailn = tiles_e * per_tile - units_e
    worst = n_pairs + n_tiles * N_EXPERTS * (SEG_ALIGN - 1) + N_EXPERTS * (EXPERT_TILE - SEG_ALIGN)
    n_rows = -(-worst // EXPERT_TILE) * EXPERT_TILE
    seg_tables = (n8.reshape(-1), lrow.reshape(-1), grow.reshape(-1), jnp.sum(n8, axis=1))
    expert_tables = (e_end - tiles_e, tiles_e, e_end[-1:])
    return seg_tables, (tail0, tailn), expert_tables, n_rows


def _combine_kernel(n8_ref, lrow_ref, grow_ref, tot_ref,
                    h_ref, slots_ref, p_ref, g_ref, wg_ref, wp_ref, gf_ref, os_hbm,
                    o_ref, buf, sems):
    i = pl.program_id(0)
    nt = pl.num_programs(0)
    slot = i % 2

    def seg_copy(slot_):
        def make(l, g, rows):
            return pltpu.make_async_copy(os_hbm.at[pl.ds(g, rows), :],
                                         buf.at[slot_, pl.ds(l, rows), :], sems.at[slot_])
        return make

    @pl.when(i == 0)
    def _():
        buf[...] = jnp.zeros_like(buf)
        _for_each_piece(n8_ref, lrow_ref, grow_ref, 0, seg_copy(0), _start)

    @pl.when(i + 1 < nt)
    def _():
        _for_each_piece(n8_ref, lrow_ref, grow_ref, i + 1, seg_copy(1 - slot), _start)

    _wait_rows(tot_ref[i], seg_copy(slot))

    slots = slots_ref[...]
    y = _dot(_permutation(slots[:, 0:1], slots[:, 1:2], slots_axis=1), buf[slot])
    h = h_ref[...] + y
    gate = jax.nn.sigmoid(_dot(_rms_norm(h, g_ref[...]).astype(BF16), wg_ref[...]))
    h = h + gate * _dot(p_ref[...].astype(BF16), wp_ref[...])
    o_ref[...] = _rms_norm(h, gf_ref[...])


def _combine(h1, slots, seg_tables, o_sorted, p2, g_ple, w_gate, w_proj, g_final):
    n, d = h1.shape
    tm = ROW_TILE
    row = lambda i, *_: (i, 0)
    full2 = lambda i, *_: (0, 0)
    return pl.pallas_call(
        _combine_kernel,
        grid_spec=pltpu.PrefetchScalarGridSpec(
            num_scalar_prefetch=len(seg_tables),
            grid=(n // tm,),
            in_specs=[
                pl.BlockSpec((tm, d), row),
                pl.BlockSpec((tm, LANES), row),
                pl.BlockSpec((tm, p2.shape[1]), row),
                pl.BlockSpec((1, d), full2),
                pl.BlockSpec(w_gate.shape, full2),
                pl.BlockSpec(w_proj.shape, full2),
                pl.BlockSpec((1, d), full2),
                pl.BlockSpec(memory_space=pl.ANY),
            ],
            out_specs=pl.BlockSpec((tm, d), row),
            scratch_shapes=[
                pltpu.VMEM((2, LOCAL_ROWS, d), BF16),
                pltpu.SemaphoreType.DMA((2,)),
            ],
        ),
        out_shape=jax.ShapeDtypeStruct((n, d), F32),
        compiler_params=pltpu.CompilerParams(
            dimension_semantics=("arbitrary",), vmem_limit_bytes=VMEM_LIMIT),
        name="combine",
    )(*seg_tables, h1, slots, p2, g_ple, w_gate, w_proj, g_final, o_sorted)


def _pad_lanes(a):
    return jnp.pad(a, ((0, 0), (0, LANES - a.shape[1])))


def kernel(x, p, g_mix, w_in, b_f, w_pool, s_pool, w_out, g_ffn, w_grp, b_grp, w_rt, b_rt,
           w_e_gate, w_e_up, w_e_down, g_ple, w_ple_gate, w_ple_proj, g_final):
    batch, seq_len, d = x.shape
    n = batch * seq_len
    assert w_in.shape[0] == 1, "single-layer stack only: the final norm is fused into the layer"
    i = 0
    pool_dim = s_pool.shape[1]
    attn_dim = N_HEADS * HEAD_DIM
    main = pool_dim + 3 * attn_dim
    h = x.reshape(n, d)
    w_f = _pad_lanes(jnp.tile(w_in[i, :, main:], (1, N_SPLIT)))
    w_main = jnp.concatenate([w_in[i, :, :main], w_f], axis=1).astype(BF16)
    b_f3 = _pad_lanes(jnp.tile(b_f[i], N_SPLIT)[None])
    pool, q, k, v = _inproj(h, g_mix[i][None], w_main, b_f3,
                            w_pool[i].astype(BF16), s_pool[i][None], seq_len)
    attn = _attention(q, k, v, batch, seq_len)
    w_router = _pad_lanes(jnp.concatenate([w_grp[i], w_rt[i]], axis=1))
    b_router = _pad_lanes(jnp.concatenate([b_grp[i], b_rt[i]])[None])
    h1, m, ext, slots, slots_t, cnt = _mix(h, pool, attn, w_out[i].astype(BF16), g_ffn[i][None],
                                          w_router, b_router)
    seg_tables, tail_tables, expert_tables, n_rows = _moe_layout(cnt, 2 * n)
    x_sorted = _dispatch(m, ext, slots_t, seg_tables + tail_tables + expert_tables[2:], n_rows)
    f = w_e_gate.shape[-1]
    o_sorted = _experts(x_sorted, *expert_tables,
                        w_e_gate[i].reshape(N_EXPERTS, d, f),
                        w_e_up[i].reshape(N_EXPERTS, d, f),
                        w_e_down[i].reshape(N_EXPERTS, f, d))
    out = _combine(h1, slots, seg_tables, o_sorted, p[i].reshape(n, -1), g_ple[i][None],
                   w_ple_gate[i].astype(BF16), w_ple_proj[i].astype(BF16), g_final[None])
    return out.reshape(batch, seq_len, d)
```

```python
import functools
import math

import numpy as np
import jax
import jax.numpy as jnp
from jax import lax
from jax.experimental import pallas as pl
from jax.experimental.pallas import tpu as pltpu

HEAD_DIM = 64
N_HEADS = 8
POOL_WINDOWS = (2, 4, 8, 16)
POOL_GROUP_DIM = 128
POOL_HISTORY = 16
N_GROUPS = 4
EXPERTS_PER_GROUP = 8
N_EXPERTS = N_GROUPS * EXPERTS_PER_GROUP
EPS = 1e-6
LANES = 128
ROUTER_LANE0 = N_GROUPS
NEG_INF = float("-inf")
LOG2E = math.log2(math.e)
N_SPLIT = 3

ROW_TILE = 512
ATTN_Q = 512
VMEM_LIMIT = 48 * 1024 * 1024

BF16 = jnp.bfloat16
F32 = jnp.float32


def _dot(a, b):
    return jnp.dot(a, b, preferred_element_type=F32)


def _split2(a):
    hi = a.astype(BF16)
    lo = (a - hi.astype(F32)).astype(BF16)
    return hi, lo


def _split3(a):
    hi = a.astype(BF16)
    r = a - hi.astype(F32)
    mid = r.astype(BF16)
    lo = (r - mid.astype(F32)).astype(BF16)
    return hi, mid, lo


def _dot_precise(a, w):
    a1, a2 = _split2(a)
    w1, w2 = _split2(w)
    n = w.shape[1]
    r = _dot(a1, jnp.concatenate([w1, w2], axis=1))
    return r[:, 0:n] + (r[:, n:] + _dot(a2, w1))


def _row_parts(rows, n_parts=2):
    step = rows // n_parts
    return [slice(k * step, (k + 1) * step) for k in range(n_parts)]


def _rms_norm(x, g):
    return x * lax.rsqrt(jnp.mean(x * x, axis=-1, keepdims=True) + EPS) * g


def _own_half_start(head):
    return 0 if head % 2 == 0 else HEAD_DIM


def _bias_placement():
    width = N_HEADS * LANES
    eq = np.zeros((LANES, width), np.float32)
    ek = np.zeros((LANES, width), np.float32)
    ones_q = np.zeros((1, width), np.float32)
    ones_k = np.zeros((1, width), np.float32)
    for h in range(N_HEADS):
        a0 = h * LANES + (HEAD_DIM - _own_half_start(h))
        for piece in range(N_SPLIT):
            eq[piece * N_HEADS + h, a0 + piece] = 1.0
            ones_q[0, a0 + N_SPLIT + piece] = 1.0
            ones_k[0, a0 + piece] = 1.0
            ek[piece * N_HEADS + h, a0 + N_SPLIT + piece] = -1.0
    return eq, ek, ones_q, ones_k


def _inproj_kernel(tiles_per_seq, x_ref, g_ref, w_ref, bf_ref, wp_ref, sp_ref,
                   eq_ref, ek_ref, oq_ref, ok_ref,
                   pool_ref, q_ref, k_ref, v_ref, carry_c, carry_u):
    i = pl.program_id(0)
    seq_tile = i % tiles_per_seq
    tm = x_ref.shape[0]
    pool_dim = pool_ref.shape[1]
    attn_dim = N_HEADS * HEAD_DIM

    @pl.when(seq_tile == 0)
    def _():
        carry_c[...] = jnp.zeros_like(carry_c)
        carry_u[...] = jnp.zeros_like(carry_u)

    a = _rms_norm(x_ref[...], g_ref[...])
    ab = a.astype(BF16)
    o0 = pool_dim
    u = _dot(ab, w_ref[:, 0:o0])
    qf = _dot(ab, w_ref[:, o0:o0 + attn_dim]) * (LOG2E * HEAD_DIM ** -0.5)
    kf = _dot(ab, w_ref[:, o0 + attn_dim:o0 + 2 * attn_dim])
    vf = _dot(ab, w_ref[:, o0 + 2 * attn_dim:o0 + 3 * attn_dim])

    fl = _dot(ab, w_ref[:, o0 + 3 * attn_dim:]) + bf_ref[...]
    lf = jnp.minimum(fl, 0.0) - jnp.log1p(jnp.exp(-jnp.abs(fl)))
    row = lax.broadcasted_iota(jnp.int32, (tm, tm), 0)
    col = lax.broadcasted_iota(jnp.int32, (tm, tm), 1)
    tril = (col <= row).astype(BF16)
    sums = _dot(tril, jnp.concatenate(_split3(lf), axis=1))
    c = carry_c[...] + (sums[:, 0:LANES] + (sums[:, LANES:2 * LANES] + sums[:, 2 * LANES:]))
    carry_c[...] = c[tm - 1:tm, :]
    c1, c2, c3 = _split3(c * LOG2E)
    lane = lax.broadcasted_iota(jnp.int32, (1, LANES), 1)
    pieces = jnp.where(lane < N_HEADS, c1, jnp.where(lane < 2 * N_HEADS, c2, c3))
    bias_q = _dot(pieces, eq_ref[...]) + oq_ref[...]
    bias_k = _dot(pieces, ek_ref[...]) + ok_ref[...]
    for h in range(N_HEADS):
        own = (lane >= _own_half_start(h)) & (lane < _own_half_start(h) + HEAD_DIM)
        pair = slice((h // 2) * LANES, (h // 2 + 1) * LANES)
        blk = slice(h * LANES, (h + 1) * LANES)
        q_ref[:, blk] = jnp.where(own, qf[:, pair], bias_q[:, blk]).astype(BF16)
        k_ref[:, blk] = jnp.where(own, kf[:, pair], bias_k[:, blk]).astype(BF16)
        one_col = (lane == HEAD_DIM - _own_half_start(h)).astype(F32)
        v_ref[:, blk] = jnp.where(own, vf[:, pair], one_col).astype(BF16)

    ext = jnp.concatenate([carry_u[...], u], axis=0)
    carry_u[...] = u[tm - POOL_HISTORY:, :]
    pos = (seq_tile * tm + 1 + lax.broadcasted_iota(jnp.int32, (tm, 1), 0)).astype(F32)
    for gi, w in enumerate(POOL_WINDOWS):
        lo, hi = gi * POOL_GROUP_DIM, (gi + 1) * POOL_GROUP_DIM
        s = ext[:, lo:hi]
        shift = 1
        while shift < w:
            s = s + pltpu.roll(s, shift, axis=0)
            shift *= 2
        mean = s[POOL_HISTORY:, :] / jnp.minimum(pos, float(w))
        d = mean - u[:, lo:hi]
        y = _dot(d.astype(BF16), wp_ref[gi]) * sp_ref[:, lo:hi]
        pool_ref[:, lo:hi] = y.astype(BF16)


def _inproj(x2, g_mix, w_main, b_f, w_pool, s_pool, seq_len):
    n, d = x2.shape
    tm = ROW_TILE
    pool_dim = w_pool.shape[0] * w_pool.shape[1]
    head_w = N_HEADS * LANES
    eq, ek, ones_q, ones_k = _bias_placement()
    row = lambda i: (i, 0)
    full2 = lambda i: (0, 0)
    return pl.pallas_call(
        functools.partial(_inproj_kernel, seq_len // tm),
        grid=(n // tm,),
        in_specs=[
            pl.BlockSpec((tm, d), row),
            pl.BlockSpec((1, d), full2),
            pl.BlockSpec(w_main.shape, full2),
            pl.BlockSpec((1, LANES), full2),
            pl.BlockSpec(w_pool.shape, lambda i: (0, 0, 0)),
            pl.BlockSpec((1, pool_dim), full2),
            pl.BlockSpec((LANES, head_w), full2),
            pl.BlockSpec((LANES, head_w), full2),
            pl.BlockSpec((1, head_w), full2),
            pl.BlockSpec((1, head_w), full2),
        ],
        out_specs=[
            pl.BlockSpec((tm, pool_dim), row),
            pl.BlockSpec((tm, head_w), row),
            pl.BlockSpec((tm, head_w), row),
            pl.BlockSpec((tm, head_w), row),
        ],
        out_shape=[
            jax.ShapeDtypeStruct((n, pool_dim), BF16),
            jax.ShapeDtypeStruct((n, head_w), BF16),
            jax.ShapeDtypeStruct((n, head_w), BF16),
            jax.ShapeDtypeStruct((n, head_w), BF16),
        ],
        scratch_shapes=[pltpu.VMEM((1, LANES), F32), pltpu.VMEM((POOL_HISTORY, pool_dim), F32)],
        compiler_params=pltpu.CompilerParams(
            dimension_semantics=("arbitrary",), vmem_limit_bytes=VMEM_LIMIT),
        name="inproj",
    )(x2, g_mix, w_main, b_f, w_pool, s_pool,
      jnp.asarray(eq, BF16), jnp.asarray(ek, BF16), jnp.asarray(ones_q), jnp.asarray(ones_k))


def _attend_pair(q_ref, k_ref, v_ref, n_keys):
    tq = q_ref.shape[0]
    past = n_keys - tq
    nt = (((1,), (1,)), ((), ()))
    blks = [slice(hh * LANES, (hh + 1) * LANES) for hh in range(2)]
    row = lax.broadcasted_iota(jnp.int32, (tq, tq), 0)
    col = lax.broadcasted_iota(jnp.int32, (tq, tq), 1)
    qs = [q_ref[:, blk] for blk in blks]
    s_diag = [jnp.where(col <= row,
                        lax.dot_general(q, k_ref[past:n_keys, blk], nt,
                                        preferred_element_type=F32), NEG_INF)
              for q, blk in zip(qs, blks)]
    m = [jnp.max(s, axis=-1, keepdims=True) for s in s_diag]
    if past:
        s_past = [lax.dot_general(q, k_ref[0:past, blk], nt, preferred_element_type=F32)
                  for q, blk in zip(qs, blks)]
        m = [jnp.maximum(mi, jnp.max(s, axis=-1, keepdims=True)) for mi, s in zip(m, s_past)]
    acc = [_dot(jnp.exp2(s - mi).astype(BF16), v_ref[past:n_keys, blk])
           for s, mi, blk in zip(s_diag, m, blks)]
    if past:
        acc = [a + _dot(jnp.exp2(s - mi).astype(BF16), v_ref[0:past, blk])
               for a, s, mi, blk in zip(acc, s_past, m, blks)]
    return acc


def _attn_kernel(q_ref, k_ref, v_ref, o_ref):
    qi = pl.program_id(2)
    tq = q_ref.shape[0]
    lane = lax.broadcasted_iota(jnp.int32, (1, LANES), 1)
    for tile in range(k_ref.shape[0] // tq):
        @pl.when(qi == tile)
        def _():
            outs = []
            for hh, acc in enumerate(_attend_pair(q_ref, k_ref, v_ref, (tile + 1) * tq)):
                sum_lane = HEAD_DIM - _own_half_start(hh)
                l = jnp.sum(jnp.where(lane == sum_lane, acc, 0.0), axis=-1, keepdims=True)
                outs.append(acc / l)
            o_ref[...] = jnp.where(lane < HEAD_DIM, outs[0], outs[1]).astype(BF16)


def _attention(q, k, v, batch, seq_len):
    n = q.shape[0]
    pairs = N_HEADS // 2
    tq = ATTN_Q
    qt = seq_len // tq
    pair_block = lambda b, hp, qi: (b, hp)
    return pl.pallas_call(
        _attn_kernel,
        grid=(batch, pairs, qt),
        in_specs=[
            pl.BlockSpec((tq, 2 * LANES), lambda b, hp, qi: (b * qt + qi, hp)),
            pl.BlockSpec((seq_len, 2 * LANES), pair_block),
            pl.BlockSpec((seq_len, 2 * LANES), pair_block),
        ],
        out_specs=pl.BlockSpec((tq, LANES), lambda b, hp, qi: (b * qt + qi, hp)),
        out_shape=jax.ShapeDtypeStruct((n, pairs * LANES), BF16),
        compiler_params=pltpu.CompilerParams(
            dimension_semantics=("arbitrary", "arbitrary", "arbitrary"),
            vmem_limit_bytes=VMEM_LIMIT),
        name="attn",
    )(q, k, v)


def _route(logits):
    lane = lax.broadcasted_iota(jnp.int32, logits.shape, 1)
    big = jnp.int32(LANES)
    gl = jnp.where(lane < N_GROUPS, logits, NEG_INF)
    gmax = jnp.max(gl, axis=-1, keepdims=True)
    g_w = 1.0 / jnp.sum(jnp.exp(gl - gmax), axis=-1, keepdims=True)
    g_idx = jnp.min(jnp.where(gl == gmax, lane, big), axis=-1, keepdims=True)
    e_lo = ROUTER_LANE0 + EXPERTS_PER_GROUP * g_idx
    el = jnp.where((lane >= e_lo) & (lane < e_lo + EXPERTS_PER_GROUP), logits, NEG_INF)
    v1 = jnp.max(el, axis=-1, keepdims=True)
    i1 = jnp.min(jnp.where(el == v1, lane, big), axis=-1, keepdims=True)
    el2 = jnp.where(lane == i1, NEG_INF, el)
    v2 = jnp.max(el2, axis=-1, keepdims=True)
    i2 = jnp.min(jnp.where(el2 == v2, lane, big), axis=-1, keepdims=True)
    e2 = jnp.exp(v2 - v1)
    w1 = g_w / (1.0 + e2)
    w2 = g_w * e2 / (1.0 + e2)
    return (i1 - ROUTER_LANE0).astype(F32), (i2 - ROUTER_LANE0).astype(F32), w1, w2


EXT_W0 = 0
EXT_W1 = N_SPLIT
EXT_E0 = 2 * N_SPLIT


def _routing_record(e0, e1, w0, w1):
    lane = lax.broadcasted_iota(jnp.int32, (1, LANES), 1)
    rec = jnp.where(lane == EXT_E0, e0, jnp.where(lane == EXT_E0 + 1, e1, 0.0))
    for first, w in ((EXT_W0, w0), (EXT_W1, w1)):
        for k, piece in enumerate(_split3(w)):
            rec = jnp.where(lane == first + k, piece.astype(F32), rec)
    return rec.astype(BF16)


def _local_slots(e0, e1):
    t = e0.shape[0]
    lane = lax.broadcasted_iota(jnp.int32, (t, LANES), 1).astype(F32)
    oh0 = lane == e0
    oh1 = lane == e1
    picked = (oh0 | oh1).astype(BF16)
    cnt = jnp.sum(picked.astype(F32), axis=0, keepdims=True)
    units = jnp.floor((cnt + (SEG_ALIGN - 1)) * (1.0 / SEG_ALIGN))
    r128 = lax.broadcasted_iota(jnp.int32, (LANES, LANES), 0)
    c128 = lax.broadcasted_iota(jnp.int32, (LANES, LANES), 1)
    before = (r128 < c128).astype(BF16)
    lstart = SEG_ALIGN * _dot(jnp.broadcast_to(units, (8, LANES)).astype(BF16), before)[0:1, :]
    row = lax.broadcasted_iota(jnp.int32, (t, t), 0)
    col = lax.broadcasted_iota(jnp.int32, (t, t), 1)
    earlier = (col < row).astype(BF16)
    base = _dot(earlier, picked) + lstart
    slot0 = jnp.sum(jnp.where(oh0, base, 0.0), axis=-1, keepdims=True)
    slot1 = jnp.sum(jnp.where(oh1, base, 0.0), axis=-1, keepdims=True)
    return slot0, slot1, cnt


def _mix_kernel(x_ref, pool_ref, attn_ref, wo_ref, g_ref, wr_ref, br_ref,
                h_ref, m_ref, ext_ref, slots_ref, slots_t_ref, cnt_ref, logits_s):
    pool_dim = pool_ref.shape[1]

    @pl.when(pl.program_id(0) == 0)
    def _():
        logits_s[...] = jnp.zeros_like(logits_s)

    prev_logits = logits_s[...]
    h = x_ref[...] + (_dot(pool_ref[...], wo_ref[0:pool_dim, :])
                      + _dot(attn_ref[...], wo_ref[pool_dim:, :]))
    h_ref[...] = h
    e0, e1, w0, w1 = _route(prev_logits)
    ext_ref[...] = _routing_record(e0, e1, w0, w1)
    slot0, slot1, cnt = _local_slots(e0, e1)
    cnt_ref[0] = cnt
    lane = lax.broadcasted_iota(jnp.int32, (1, LANES), 1)
    slots = jnp.where(lane == 0, slot0, jnp.where(lane == 1, slot1, 0.0))
    slots_ref[...] = slots
    slots_t_ref[0] = slots.T[0:8, :]
    m = _rms_norm(h, g_ref[...])
    m_ref[...] = m.astype(BF16)
    logits_s[...] = _dot_precise(m, wr_ref[...]) + br_ref[...]


def _mix(x2, pool, attn, w_out, g_ffn, w_router, b_router):
    n, d = x2.shape
    tm = ROW_TILE
    n_tiles = n // tm
    row = lambda i: (jnp.minimum(i, n_tiles - 1), 0)
    routed = lambda i: (jnp.maximum(i - 1, 0), 0)
    full2 = lambda i: (0, 0)
    return pl.pallas_call(
        _mix_kernel,
        grid=(n_tiles + 1,),
        in_specs=[
            pl.BlockSpec((tm, d), row),
            pl.BlockSpec((tm, pool.shape[1]), row),
            pl.BlockSpec((tm, attn.shape[1]), row),
            pl.BlockSpec(w_out.shape, full2),
            pl.BlockSpec((1, d), full2),
            pl.BlockSpec(w_router.shape, full2),
            pl.BlockSpec((1, LANES), full2),
        ],
        out_specs=[
            pl.BlockSpec((tm, d), row),
            pl.BlockSpec((tm, d), row),
            pl.BlockSpec((tm, LANES), routed),
            pl.BlockSpec((tm, LANES), routed),
            pl.BlockSpec((1, 8, tm), lambda i: (jnp.maximum(i - 1, 0), 0, 0)),
            pl.BlockSpec((1, 1, LANES), lambda i: (jnp.maximum(i - 1, 0), 0, 0)),
        ],
        out_shape=[
            jax.ShapeDtypeStruct((n, d), F32),
            jax.ShapeDtypeStruct((n, d), BF16),
            jax.ShapeDtypeStruct((n, LANES), BF16),
            jax.ShapeDtypeStruct((n, LANES), F32),
            jax.ShapeDtypeStruct((n_tiles, 8, tm), F32),
            jax.ShapeDtypeStruct((n_tiles, 1, LANES), F32),
        ],
        scratch_shapes=[pltpu.VMEM((tm, LANES), F32)],
        compiler_params=pltpu.CompilerParams(
            dimension_semantics=("arbitrary",), vmem_limit_bytes=VMEM_LIMIT),
        name="mix",
    )(x2, pool, attn, w_out, g_ffn, w_router, b_router)


SEG_ALIGN = 16
COPY_UNITS = 4
COPY_ROWS = COPY_UNITS * SEG_ALIGN
EXPERT_TILE = 512
LOCAL_ROWS = 2 * ROW_TILE + N_EXPERTS * SEG_ALIGN


def _permutation(slot0, slot1, slots_axis):
    shape = (1, LOCAL_ROWS) if slots_axis == 1 else (LOCAL_ROWS, 1)
    s = lax.broadcasted_iota(jnp.int32, shape, slots_axis)
    return ((s == slot0.astype(jnp.int32)) | (s == slot1.astype(jnp.int32))).astype(BF16)


def _for_each_piece(n8_ref, lrow_ref, grow_ref, tile, make_copy, act):
    base = tile * N_EXPERTS

    def per_expert(e, carry):
        n = n8_ref[base + e]
        l0 = lrow_ref[base + e] * SEG_ALIGN
        g0 = grow_ref[base + e] * SEG_ALIGN

        def piece(off, rows):
            act(make_copy(pl.multiple_of(l0 + off, SEG_ALIGN),
                          pl.multiple_of(g0 + off, SEG_ALIGN), rows))

        n_full = n // COPY_UNITS

        def per_block(c, carry):
            piece(c * COPY_ROWS, COPY_ROWS)
            return carry

        lax.fori_loop(0, n_full, per_block, 0)
        for b in reversed(range(COPY_UNITS.bit_length() - 1)):
            @pl.when(((n >> b) & 1) == 1)
            def _():
                higher = (n >> (b + 1)) << (b + 1)
                piece(higher * SEG_ALIGN, SEG_ALIGN << b)

        return carry

    lax.fori_loop(0, N_EXPERTS, per_expert, 0)


def _wait_rows(total_units, make_copy):
    for b in range((LOCAL_ROWS // SEG_ALIGN).bit_length()):
        @pl.when(((total_units >> b) & 1) == 1)
        def _():
            make_copy(0, 0, SEG_ALIGN << b).wait()


def _start(copy):
    copy.start()


def _wait(copy):
    copy.wait()


def _dispatch_kernel(n_steps, n8_ref, lrow_ref, grow_ref, tot_ref, tail0_ref, tailn_ref, nu_ref,
                     m_ref, ext_ref, st_ref, st_next_ref, xs_hbm,
                     buf, zeros, perm_even, perm_odd, sems, tail_sem):
    i = pl.program_id(0)
    slot = i % 2
    d = m_ref.shape[1]
    n_expert_tiles = xs_hbm.shape[0] // EXPERT_TILE

    def seg_copy(slot_):
        def make(l, g, rows):
            return pltpu.make_async_copy(buf.at[slot_, pl.ds(l, rows), :],
                                         xs_hbm.at[pl.ds(g, rows), :], sems.at[slot_])
        return make

    def for_each_tail(act):
        @pl.when(i < N_EXPERTS)
        def _():
            e = jnp.minimum(i, N_EXPERTS - 1)
            g0 = tail0_ref[e] * SEG_ALIGN

            def body(c, carry):
                g = pl.multiple_of(g0 + c * SEG_ALIGN, SEG_ALIGN)
                act(pltpu.make_async_copy(zeros.at[pl.ds(0, SEG_ALIGN), :],
                                          xs_hbm.at[pl.ds(g, SEG_ALIGN), :], tail_sem))
                return carry

            lax.fori_loop(0, tailn_ref[e], body, 0)

        for k in range(-(-n_expert_tiles // n_steps)):
            t = nu_ref[0] + i + k * n_steps

            @pl.when(t < n_expert_tiles)
            def _():
                g = pl.multiple_of(t * EXPERT_TILE, EXPERT_TILE)
                act(pltpu.make_async_copy(zeros, xs_hbm.at[pl.ds(g, EXPERT_TILE), :], tail_sem))

    @pl.when(i == 0)
    def _():
        zeros[...] = jnp.zeros_like(zeros)

    for_each_tail(_start)

    def build(slots_t, perm):
        perm[...] = _permutation(slots_t[0, 0:1, :], slots_t[0, 1:2, :], slots_axis=0)

    @pl.when(i == 0)
    def _():
        build(st_ref, perm_even)

    def permute(perm, next_perm):
        p = perm[...]
        build(st_next_ref, next_perm)
        buf[slot, :, 0:d] = _dot(p, m_ref[...]).astype(BF16)
        buf[slot, :, d:] = _dot(p, ext_ref[...]).astype(BF16)

    @pl.when(slot == 0)
    def _():
        permute(perm_even, perm_odd)

    @pl.when(slot == 1)
    def _():
        permute(perm_odd, perm_even)

    _for_each_piece(n8_ref, lrow_ref, grow_ref, i, seg_copy(slot), _start)

    @pl.when(i > 0)
    def _():
        _wait_rows(tot_ref[i - 1], seg_copy(1 - slot))

    for_each_tail(_wait)

    @pl.when(i == n_steps - 1)
    def _():
        _wait_rows(tot_ref[i], seg_copy(slot))


def _dispatch(m, ext, slots_t, tables, n_rows):
    n, d = m.shape
    tm = ROW_TILE
    assert n // tm >= N_EXPERTS, "each grid step zero-fills the tail of one expert"
    width = d + LANES
    last = n // tm - 1
    row = lambda i, *_: (i, 0)
    return pl.pallas_call(
        functools.partial(_dispatch_kernel, n // tm),
        grid_spec=pltpu.PrefetchScalarGridSpec(
            num_scalar_prefetch=len(tables),
            grid=(n // tm,),
            in_specs=[
                pl.BlockSpec((tm, d), row),
                pl.BlockSpec((tm, LANES), row),
                pl.BlockSpec((1, 8, tm), lambda i, *_: (i, 0, 0)),
                pl.BlockSpec((1, 8, tm), lambda i, *_: (jnp.minimum(i + 1, last), 0, 0)),
            ],
            out_specs=pl.BlockSpec(memory_space=pl.ANY),
            scratch_shapes=[
                pltpu.VMEM((2, LOCAL_ROWS, width), BF16),
                pltpu.VMEM((EXPERT_TILE, width), BF16),
                pltpu.VMEM((LOCAL_ROWS, tm), BF16),
                pltpu.VMEM((LOCAL_ROWS, tm), BF16),
                pltpu.SemaphoreType.DMA((2,)),
                pltpu.SemaphoreType.DMA(()),
            ],
        ),
        out_shape=jax.ShapeDtypeStruct((n_rows, width), BF16),
        compiler_params=pltpu.CompilerParams(
            dimension_semantics=("arbitrary",), vmem_limit_bytes=VMEM_LIMIT),
        name="dispatch",
    )(*tables, m, ext, slots_t, slots_t)


def _expert_kernel(first_ref, count_ref, nu_ref, wg_ref, wu_ref, wd_ref, xs_hbm, os_hbm,
                   xbuf, obuf, zeros, wgu_s, wd_s, in_sems, out_sems, zero_sem):
    e = pl.program_id(0)
    d = os_hbm.shape[1]
    f = wg_ref.shape[2]
    n = count_ref[e]
    t0 = first_ref[e]
    n_tiles = os_hbm.shape[0] // EXPERT_TILE

    def rows(t):
        return pl.ds(pl.multiple_of(t * EXPERT_TILE, EXPERT_TILE), EXPERT_TILE)

    def in_copy(k, slot, first=t0):
        return pltpu.make_async_copy(xs_hbm.at[rows(first + k), :], xbuf.at[slot],
                                     in_sems.at[slot])

    def out_copy(k, slot):
        return pltpu.make_async_copy(obuf.at[slot], os_hbm.at[rows(t0 + k), :], out_sems.at[slot])

    def for_each_unused(act):
        for j in range(-(-n_tiles // N_EXPERTS)):
            t = nu_ref[0] + e + j * N_EXPERTS

            @pl.when(t < n_tiles)
            def _():
                act(pltpu.make_async_copy(zeros, os_hbm.at[rows(t), :], zero_sem))

    @pl.when(e == 0)
    def _():
        zeros[...] = jnp.zeros_like(zeros)

    for_each_unused(_start)

    @pl.when((e == 0) & (n > 0))
    def _():
        in_copy(0, 0).start()

    @pl.when(n > 0)
    def _():
        wgu_s[:, 0:f] = wg_ref[0].astype(BF16)
        wgu_s[:, f:] = wu_ref[0].astype(BF16)
        wd_s[...] = wd_ref[0].astype(BF16)

    def tile(k, carry):
        slot = k % 2

        @pl.when(k + 1 < n)
        def _():
            in_copy(k + 1, 1 - slot).start()

        in_copy(k, slot).wait()

        @pl.when(k >= 2)
        def _():
            out_copy(k - 2, slot).wait()

        x = xbuf[slot, :, 0:d]
        rec = xbuf[slot, :, d:].astype(F32)
        lane = lax.broadcasted_iota(jnp.int32, (1, LANES), 1)

        def lanes_sum(first, count):
            keep = (lane >= first) & (lane < first + count)
            return jnp.sum(jnp.where(keep, rec, 0.0), axis=-1, keepdims=True)

        first_choice = lanes_sum(EXT_E0, 1) == e.astype(F32)
        w = jnp.where(first_choice, lanes_sum(EXT_W0, N_SPLIT), lanes_sum(EXT_W1, N_SPLIT))
        h = _dot(x, wgu_s[...])
        hg = h[:, 0:f]
        hu = h[:, f:]
        a = hg * jax.nn.sigmoid(hg) * hu * w
        obuf[slot] = _dot(a.astype(BF16), wd_s[...]).astype(BF16)
        out_copy(k, slot).start()
        return carry

    lax.fori_loop(0, n, tile, 0)

    for back in (2, 1):
        @pl.when(n >= back)
        def _():
            out_copy(n - back, (n - back) % 2).wait()


    nxt = jnp.minimum(e + 1, N_EXPERTS - 1)

    @pl.when((e + 1 < N_EXPERTS) & (count_ref[nxt] > 0))
    def _():
        in_copy(0, 0, first_ref[nxt]).start()

    for_each_unused(_wait)


def _experts(xs, first_tile, tile_count, n_used, w_gate, w_up, w_down):
    n_rows, width = xs.shape
    n_exp, d, f = w_gate.shape
    te = EXPERT_TILE
    weight = lambda e, *_: (e, 0, 0)
    return pl.pallas_call(
        _expert_kernel,
        grid_spec=pltpu.PrefetchScalarGridSpec(
            num_scalar_prefetch=3,
            grid=(n_exp,),
            in_specs=[
                pl.BlockSpec((1, d, f), weight),
                pl.BlockSpec((1, d, f), weight),
                pl.BlockSpec((1, f, d), weight),
                pl.BlockSpec(memory_space=pl.ANY),
            ],
            out_specs=pl.BlockSpec(memory_space=pl.ANY),
            scratch_shapes=[
                pltpu.VMEM((2, te, width), BF16),
                pltpu.VMEM((2, te, d), BF16),
                pltpu.VMEM((te, d), BF16),
                pltpu.VMEM((d, 2 * f), BF16), pltpu.VMEM((f, d), BF16),
                pltpu.SemaphoreType.DMA((2,)),
                pltpu.SemaphoreType.DMA((2,)),
                pltpu.SemaphoreType.DMA(()),
            ],
        ),
        out_shape=jax.ShapeDtypeStruct((n_rows, d), BF16),
        compiler_params=pltpu.CompilerParams(
            dimension_semantics=("arbitrary",), vmem_limit_bytes=VMEM_LIMIT),
        name="experts",
    )(first_tile, tile_count, n_used, w_gate, w_up, w_down, xs)


def _moe_layout(cnt, n_pairs):
    c = cnt[:, 0, :N_EXPERTS].astype(jnp.int32)
    n_tiles = c.shape[0]
    n8 = (c + SEG_ALIGN - 1) // SEG_ALIGN
    lrow = jnp.cumsum(n8, axis=1) - n8
    units_e = jnp.sum(n8, axis=0)
    per_tile = EXPERT_TILE // SEG_ALIGN
    tiles_e = (units_e + per_tile - 1) // per_tile
    e_end = jnp.cumsum(tiles_e)
    e_off = (e_end - tiles_e) * per_tile
    grow = e_off[None, :] + jnp.cumsum(n8, axis=0) - n8
    tail0 = e_off + units_e
    tailn = tiles_e * per_tile - units_e
    worst = n_pairs + n_tiles * N_EXPERTS * (SEG_ALIGN - 1) + N_EXPERTS * (EXPERT_TILE - SEG_ALIGN)
    n_rows = -(-worst // EXPERT_TILE) * EXPERT_TILE
    seg_tables = (n8.reshape(-1), lrow.reshape(-1), grow.reshape(-1), jnp.sum(n8, axis=1))
    expert_tables = (e_end - tiles_e, tiles_e, e_end[-1:])
    return seg_tables, (tail0, tailn), expert_tables, n_rows


def _combine_kernel(n8_ref, lrow_ref, grow_ref, tot_ref,
                    h_ref, slots_ref, slots_next_ref, p_ref, g_ref, wg_ref, wp_ref, gf_ref,
                    os_hbm, o_ref, buf, perm_even, perm_odd, y_s, sems):
    i = pl.program_id(0)
    nt = pl.num_programs(0)
    slot = i % 2

    def seg_copy(slot_):
        def make(l, g, rows):
            return pltpu.make_async_copy(os_hbm.at[pl.ds(g, rows), :],
                                         buf.at[slot_, pl.ds(l, rows), :], sems.at[slot_])
        return make

    @pl.when(i == 0)
    def _():
        buf[...] = jnp.zeros_like(buf)
        _for_each_piece(n8_ref, lrow_ref, grow_ref, 0, seg_copy(0), _start)

    @pl.when(i + 1 < nt)
    def _():
        _for_each_piece(n8_ref, lrow_ref, grow_ref, i + 1, seg_copy(1 - slot), _start)

    _wait_rows(tot_ref[i], seg_copy(slot))

    def build(slots_block, perm):
        slots = slots_block[...]
        perm[...] = _permutation(slots[:, 0:1], slots[:, 1:2], slots_axis=1)

    @pl.when(i == 0)
    def _():
        build(slots_ref, perm_even)

    def unsort(perm, next_perm):
        p = perm[...]
        build(slots_next_ref, next_perm)
        y_s[...] = _dot(p, buf[slot])

    @pl.when(slot == 0)
    def _():
        unsort(perm_even, perm_odd)

    @pl.when(slot == 1)
    def _():
        unsort(perm_odd, perm_even)

    parts = _row_parts(h_ref.shape[0])
    h = [h_ref[r, :] + y_s[r, :] for r in parts]
    normed = [_rms_norm(hp, g_ref[...]).astype(BF16) for hp in h]
    gate = [jax.nn.sigmoid(_dot(xp, wg_ref[...])) for xp in normed]
    emb = [_dot(p_ref[r, :].astype(BF16), wp_ref[...]) for r in parts]
    for r, hp, gp, ep in zip(parts, h, gate, emb):
        o_ref[r, :] = _rms_norm(hp + gp * ep, gf_ref[...])


def _combine(h1, slots, seg_tables, o_sorted, p2, g_ple, w_gate, w_proj, g_final):
    n, d = h1.shape
    tm = ROW_TILE
    row = lambda i, *_: (i, 0)
    full2 = lambda i, *_: (0, 0)
    return pl.pallas_call(
        _combine_kernel,
        grid_spec=pltpu.PrefetchScalarGridSpec(
            num_scalar_prefetch=len(seg_tables),
            grid=(n // tm,),
            in_specs=[
                pl.BlockSpec((tm, d), row),
                pl.BlockSpec((tm, LANES), row),
                pl.BlockSpec((tm, LANES), lambda i, *_: (jnp.minimum(i + 1, n // tm - 1), 0)),
                pl.BlockSpec((tm, p2.shape[1]), row),
                pl.BlockSpec((1, d), full2),
                pl.BlockSpec(w_gate.shape, full2),
                pl.BlockSpec(w_proj.shape, full2),
                pl.BlockSpec((1, d), full2),
                pl.BlockSpec(memory_space=pl.ANY),
            ],
            out_specs=pl.BlockSpec((tm, d), row),
            scratch_shapes=[
                pltpu.VMEM((2, LOCAL_ROWS, d), BF16),
                pltpu.VMEM((tm, LOCAL_ROWS), BF16),
                pltpu.VMEM((tm, LOCAL_ROWS), BF16),
                pltpu.VMEM((tm, d), F32),
                pltpu.SemaphoreType.DMA((2,)),
            ],
        ),
        out_shape=jax.ShapeDtypeStruct((n, d), F32),
        compiler_params=pltpu.CompilerParams(
            dimension_semantics=("arbitrary",), vmem_limit_bytes=VMEM_LIMIT),
        name="combine",
    )(*seg_tables, h1, slots, slots, p2, g_ple, w_gate, w_proj, g_final, o_sorted)


def _pad_lanes(a):
    return jnp.pad(a, ((0, 0), (0, LANES - a.shape[1])))


def kernel(x, p, g_mix, w_in, b_f, w_pool, s_pool, w_out, g_ffn, w_grp, b_grp, w_rt, b_rt,
           w_e_gate, w_e_up, w_e_down, g_ple, w_ple_gate, w_ple_proj, g_final):
    batch, seq_len, d = x.shape
    n = batch * seq_len
    assert w_in.shape[0] == 1, "single-layer stack only: the final norm is fused into the layer"
    i = 0
    pool_dim = s_pool.shape[1]
    attn_dim = N_HEADS * HEAD_DIM
    main = pool_dim + 3 * attn_dim
    h = x.reshape(n, d)
    w_f = _pad_lanes(jnp.tile(w_in[i, :, main:], (1, N_SPLIT)))
    w_main = jnp.concatenate([w_in[i, :, :main], w_f], axis=1).astype(BF16)
    b_f3 = _pad_lanes(jnp.tile(b_f[i], N_SPLIT)[None])
    pool, q, k, v = _inproj(h, g_mix[i][None], w_main, b_f3,
                            w_pool[i].astype(BF16), s_pool[i][None], seq_len)
    attn = _attention(q, k, v, batch, seq_len)
    w_router = _pad_lanes(jnp.concatenate([w_grp[i], w_rt[i]], axis=1))
    b_router = _pad_lanes(jnp.concatenate([b_grp[i], b_rt[i]])[None])
    h1, m, ext, slots, slots_t, cnt = _mix(h, pool, attn, w_out[i].astype(BF16), g_ffn[i][None],
                                          w_router, b_router)
    seg_tables, tail_tables, expert_tables, n_rows = _moe_layout(cnt, 2 * n)
    x_sorted = _dispatch(m, ext, slots_t, seg_tables + tail_tables + expert_tables[2:], n_rows)
    f = w_e_gate.shape[-1]
    o_sorted = _experts(x_sorted, *expert_tables,
                        w_e_gate[i].reshape(N_EXPERTS, d, f),
                        w_e_up[i].reshape(N_EXPERTS, d, f),
                        w_e_down[i].reshape(N_EXPERTS, f, d))
    out = _combine(h1, slots, seg_tables, o_sorted, p[i].reshape(n, -1), g_ple[i][None],
                   w_ple_gate[i].astype(BF16), w_ple_proj[i].astype(BF16), g_final[None])
    return out.reshape(batch, seq_len, d)
```

```python
import functools
import math

import numpy as np
import jax
import jax.numpy as jnp
from jax import lax
from jax.experimental import pallas as pl
from jax.experimental.pallas import tpu as pltpu

HEAD_DIM = 64
N_HEADS = 8
POOL_WINDOWS = (2, 4, 8, 16)
POOL_GROUP_DIM = 128
POOL_HISTORY = 16
N_GROUPS = 4
EXPERTS_PER_GROUP = 8
N_EXPERTS = N_GROUPS * EXPERTS_PER_GROUP
EPS = 1e-6
LANES = 128
ROUTER_LANE0 = N_GROUPS
NEG_INF = float("-inf")
LOG2E = math.log2(math.e)
N_SPLIT = 3

ROW_TILE = 512
ATTN_Q = 512
VMEM_LIMIT = 48 * 1024 * 1024

BF16 = jnp.bfloat16
F32 = jnp.float32


def _dot(a, b):
    return jnp.dot(a, b, preferred_element_type=F32)


def _split2(a):
    hi = a.astype(BF16)
    lo = (a - hi.astype(F32)).astype(BF16)
    return hi, lo


def _split3(a):
    hi = a.astype(BF16)
    r = a - hi.astype(F32)
    mid = r.astype(BF16)
    lo = (r - mid.astype(F32)).astype(BF16)
    return hi, mid, lo


def _dot_precise(a, w):
    a1, a2 = _split2(a)
    w1, w2 = _split2(w)
    n = w.shape[1]
    r = _dot(a1, jnp.concatenate([w1, w2], axis=1))
    return r[:, 0:n] + (r[:, n:] + _dot(a2, w1))


def _row_parts(rows, n_parts=2):
    step = rows // n_parts
    return [slice(k * step, (k + 1) * step) for k in range(n_parts)]


def _rms_norm(x, g):
    return x * lax.rsqrt(jnp.mean(x * x, axis=-1, keepdims=True) + EPS) * g


def _own_half_start(head):
    return 0 if head % 2 == 0 else HEAD_DIM


def _bias_placement():
    width = N_HEADS * LANES
    eq = np.zeros((LANES, width), np.float32)
    ek = np.zeros((LANES, width), np.float32)
    ones_q = np.zeros((1, width), np.float32)
    ones_k = np.zeros((1, width), np.float32)
    for h in range(N_HEADS):
        a0 = h * LANES + (HEAD_DIM - _own_half_start(h))
        for piece in range(N_SPLIT):
            eq[piece * N_HEADS + h, a0 + piece] = 1.0
            ones_q[0, a0 + N_SPLIT + piece] = 1.0
            ones_k[0, a0 + piece] = 1.0
            ek[piece * N_HEADS + h, a0 + N_SPLIT + piece] = -1.0
    return eq, ek, ones_q, ones_k


def _inproj_kernel(tiles_per_seq, x_ref, g_ref, w_ref, bf_ref, wp_ref, sp_ref,
                   eq_ref, ek_ref, oq_ref, ok_ref,
                   pool_ref, q_ref, k_ref, v_ref, carry_c, carry_u):
    i = pl.program_id(0)
    seq_tile = i % tiles_per_seq
    tm = x_ref.shape[0]
    pool_dim = pool_ref.shape[1]
    attn_dim = N_HEADS * HEAD_DIM

    @pl.when(seq_tile == 0)
    def _():
        carry_c[...] = jnp.zeros_like(carry_c)
        carry_u[...] = jnp.zeros_like(carry_u)

    a = _rms_norm(x_ref[...], g_ref[...])
    ab = a.astype(BF16)
    o0 = pool_dim
    u = _dot(ab, w_ref[:, 0:o0])
    qf = _dot(ab, w_ref[:, o0:o0 + attn_dim]) * (LOG2E * HEAD_DIM ** -0.5)
    kf = _dot(ab, w_ref[:, o0 + attn_dim:o0 + 2 * attn_dim])
    vf = _dot(ab, w_ref[:, o0 + 2 * attn_dim:o0 + 3 * attn_dim])

    fl = _dot(ab, w_ref[:, o0 + 3 * attn_dim:]) + bf_ref[...]
    lf = jnp.minimum(fl, 0.0) - jnp.log1p(jnp.exp(-jnp.abs(fl)))
    row = lax.broadcasted_iota(jnp.int32, (tm, tm), 0)
    col = lax.broadcasted_iota(jnp.int32, (tm, tm), 1)
    tril = (col <= row).astype(BF16)
    sums = _dot(tril, jnp.concatenate(_split3(lf), axis=1))
    c = carry_c[...] + (sums[:, 0:LANES] + (sums[:, LANES:2 * LANES] + sums[:, 2 * LANES:]))
    carry_c[...] = c[tm - 1:tm, :]
    c1, c2, c3 = _split3(c * LOG2E)
    lane = lax.broadcasted_iota(jnp.int32, (1, LANES), 1)
    pieces = jnp.where(lane < N_HEADS, c1, jnp.where(lane < 2 * N_HEADS, c2, c3))
    bias_q = _dot(pieces, eq_ref[...]) + oq_ref[...]
    bias_k = _dot(pieces, ek_ref[...]) + ok_ref[...]
    for h in range(N_HEADS):
        own = (lane >= _own_half_start(h)) & (lane < _own_half_start(h) + HEAD_DIM)
        pair = slice((h // 2) * LANES, (h // 2 + 1) * LANES)
        blk = slice(h * LANES, (h + 1) * LANES)
        q_ref[:, blk] = jnp.where(own, qf[:, pair], bias_q[:, blk]).astype(BF16)
        k_ref[:, blk] = jnp.where(own, kf[:, pair], bias_k[:, blk]).astype(BF16)
        one_col = (lane == HEAD_DIM - _own_half_start(h)).astype(F32)
        v_ref[:, blk] = jnp.where(own, vf[:, pair], one_col).astype(BF16)

    ext = jnp.concatenate([carry_u[...], u], axis=0)
    carry_u[...] = u[tm - POOL_HISTORY:, :]
    pos = (seq_tile * tm + 1 + lax.broadcasted_iota(jnp.int32, (tm, 1), 0)).astype(F32)
    for gi, w in enumerate(POOL_WINDOWS):
        lo, hi = gi * POOL_GROUP_DIM, (gi + 1) * POOL_GROUP_DIM
        s = ext[:, lo:hi]
        shift = 1
        while shift < w:
            s = s + pltpu.roll(s, shift, axis=0)
            shift *= 2
        mean = s[POOL_HISTORY:, :] / jnp.minimum(pos, float(w))
        d = mean - u[:, lo:hi]
        y = _dot(d.astype(BF16), wp_ref[gi]) * sp_ref[:, lo:hi]
        pool_ref[:, lo:hi] = y.astype(BF16)


def _inproj(x2, g_mix, w_main, b_f, w_pool, s_pool, seq_len):
    n, d = x2.shape
    tm = ROW_TILE
    pool_dim = w_pool.shape[0] * w_pool.shape[1]
    head_w = N_HEADS * LANES
    eq, ek, ones_q, ones_k = _bias_placement()
    row = lambda i: (i, 0)
    full2 = lambda i: (0, 0)
    return pl.pallas_call(
        functools.partial(_inproj_kernel, seq_len // tm),
        grid=(n // tm,),
        in_specs=[
            pl.BlockSpec((tm, d), row),
            pl.BlockSpec((1, d), full2),
            pl.BlockSpec(w_main.shape, full2),
            pl.BlockSpec((1, LANES), full2),
            pl.BlockSpec(w_pool.shape, lambda i: (0, 0, 0)),
            pl.BlockSpec((1, pool_dim), full2),
            pl.BlockSpec((LANES, head_w), full2),
            pl.BlockSpec((LANES, head_w), full2),
            pl.BlockSpec((1, head_w), full2),
            pl.BlockSpec((1, head_w), full2),
        ],
        out_specs=[
            pl.BlockSpec((tm, pool_dim), row),
            pl.BlockSpec((tm, head_w), row),
            pl.BlockSpec((tm, head_w), row),
            pl.BlockSpec((tm, head_w), row),
        ],
        out_shape=[
            jax.ShapeDtypeStruct((n, pool_dim), BF16),
            jax.ShapeDtypeStruct((n, head_w), BF16),
            jax.ShapeDtypeStruct((n, head_w), BF16),
            jax.ShapeDtypeStruct((n, head_w), BF16),
        ],
        scratch_shapes=[pltpu.VMEM((1, LANES), F32), pltpu.VMEM((POOL_HISTORY, pool_dim), F32)],
        compiler_params=pltpu.CompilerParams(
            dimension_semantics=("arbitrary",), vmem_limit_bytes=VMEM_LIMIT),
        name="inproj",
    )(x2, g_mix, w_main, b_f, w_pool, s_pool,
      jnp.asarray(eq, BF16), jnp.asarray(ek, BF16), jnp.asarray(ones_q), jnp.asarray(ones_k))


def _attend_pair(q_ref, k_ref, v_ref, n_keys):
    tq = q_ref.shape[0]
    past = n_keys - tq
    nt = (((1,), (1,)), ((), ()))
    blks = [slice(hh * LANES, (hh + 1) * LANES) for hh in range(2)]
    row = lax.broadcasted_iota(jnp.int32, (tq, tq), 0)
    col = lax.broadcasted_iota(jnp.int32, (tq, tq), 1)
    qs = [q_ref[:, blk] for blk in blks]
    s_diag = [jnp.where(col <= row,
                        lax.dot_general(q, k_ref[past:n_keys, blk], nt,
                                        preferred_element_type=F32), NEG_INF)
              for q, blk in zip(qs, blks)]
    m = [jnp.max(s, axis=-1, keepdims=True) for s in s_diag]
    if past:
        s_past = [lax.dot_general(q, k_ref[0:past, blk], nt, preferred_element_type=F32)
                  for q, blk in zip(qs, blks)]
        m = [jnp.maximum(mi, jnp.max(s, axis=-1, keepdims=True)) for mi, s in zip(m, s_past)]
    acc = [_dot(jnp.exp2(s - mi).astype(BF16), v_ref[past:n_keys, blk])
           for s, mi, blk in zip(s_diag, m, blks)]
    if past:
        acc = [a + _dot(jnp.exp2(s - mi).astype(BF16), v_ref[0:past, blk])
               for a, s, mi, blk in zip(acc, s_past, m, blks)]
    return acc


def _attn_kernel(q_ref, k_ref, v_ref, o_ref):
    qi = pl.program_id(2)
    tq = q_ref.shape[0]
    lane = lax.broadcasted_iota(jnp.int32, (1, LANES), 1)
    for tile in range(k_ref.shape[0] // tq):
        @pl.when(qi == tile)
        def _():
            outs = []
            for hh, acc in enumerate(_attend_pair(q_ref, k_ref, v_ref, (tile + 1) * tq)):
                sum_lane = HEAD_DIM - _own_half_start(hh)
                l = jnp.sum(jnp.where(lane == sum_lane, acc, 0.0), axis=-1, keepdims=True)
                outs.append(acc / l)
            o_ref[...] = jnp.where(lane < HEAD_DIM, outs[0], outs[1]).astype(BF16)


def _attention(q, k, v, batch, seq_len):
    n = q.shape[0]
    pairs = N_HEADS // 2
    tq = ATTN_Q
    qt = seq_len // tq
    pair_block = lambda b, hp, qi: (b, hp)
    return pl.pallas_call(
        _attn_kernel,
        grid=(batch, pairs, qt),
        in_specs=[
            pl.BlockSpec((tq, 2 * LANES), lambda b, hp, qi: (b * qt + qi, hp)),
            pl.BlockSpec((seq_len, 2 * LANES), pair_block),
            pl.BlockSpec((seq_len, 2 * LANES), pair_block),
        ],
        out_specs=pl.BlockSpec((tq, LANES), lambda b, hp, qi: (b * qt + qi, hp)),
        out_shape=jax.ShapeDtypeStruct((n, pairs * LANES), BF16),
        compiler_params=pltpu.CompilerParams(
            dimension_semantics=("arbitrary", "arbitrary", "arbitrary"),
            vmem_limit_bytes=VMEM_LIMIT),
        name="attn",
    )(q, k, v)


def _route(logits):
    lane = lax.broadcasted_iota(jnp.int32, logits.shape, 1)
    big = jnp.int32(LANES)
    gl = jnp.where(lane < N_GROUPS, logits, NEG_INF)
    gmax = jnp.max(gl, axis=-1, keepdims=True)
    g_w = 1.0 / jnp.sum(jnp.exp(gl - gmax), axis=-1, keepdims=True)
    g_idx = jnp.min(jnp.where(gl == gmax, lane, big), axis=-1, keepdims=True)
    e_lo = ROUTER_LANE0 + EXPERTS_PER_GROUP * g_idx
    el = jnp.where((lane >= e_lo) & (lane < e_lo + EXPERTS_PER_GROUP), logits, NEG_INF)
    v1 = jnp.max(el, axis=-1, keepdims=True)
    i1 = jnp.min(jnp.where(el == v1, lane, big), axis=-1, keepdims=True)
    el2 = jnp.where(lane == i1, NEG_INF, el)
    v2 = jnp.max(el2, axis=-1, keepdims=True)
    i2 = jnp.min(jnp.where(el2 == v2, lane, big), axis=-1, keepdims=True)
    e2 = jnp.exp(v2 - v1)
    w1 = g_w / (1.0 + e2)
    w2 = g_w * e2 / (1.0 + e2)
    return (i1 - ROUTER_LANE0).astype(F32), (i2 - ROUTER_LANE0).astype(F32), w1, w2


EXT_W0 = 0
EXT_W1 = N_SPLIT
EXT_E0 = 2 * N_SPLIT


def _routing_record(e0, e1, w0, w1):
    lane = lax.broadcasted_iota(jnp.int32, (1, LANES), 1)
    rec = jnp.where(lane == EXT_E0, e0, jnp.where(lane == EXT_E0 + 1, e1, 0.0))
    for first, w in ((EXT_W0, w0), (EXT_W1, w1)):
        for k, piece in enumerate(_split3(w)):
            rec = jnp.where(lane == first + k, piece.astype(F32), rec)
    return rec.astype(BF16)


def _local_slots(e0, e1):
    t = e0.shape[0]
    lane = lax.broadcasted_iota(jnp.int32, (t, LANES), 1).astype(F32)
    oh0 = lane == e0
    oh1 = lane == e1
    picked = (oh0 | oh1).astype(BF16)
    cnt = jnp.sum(picked.astype(F32), axis=0, keepdims=True)
    units = jnp.floor((cnt + (SEG_ALIGN - 1)) * (1.0 / SEG_ALIGN))
    r128 = lax.broadcasted_iota(jnp.int32, (LANES, LANES), 0)
    c128 = lax.broadcasted_iota(jnp.int32, (LANES, LANES), 1)
    before = (r128 < c128).astype(BF16)
    lstart = SEG_ALIGN * _dot(jnp.broadcast_to(units, (8, LANES)).astype(BF16), before)[0:1, :]
    row = lax.broadcasted_iota(jnp.int32, (t, t), 0)
    col = lax.broadcasted_iota(jnp.int32, (t, t), 1)
    earlier = (col < row).astype(BF16)
    base = _dot(earlier, picked) + lstart
    slot0 = jnp.sum(jnp.where(oh0, base, 0.0), axis=-1, keepdims=True)
    slot1 = jnp.sum(jnp.where(oh1, base, 0.0), axis=-1, keepdims=True)
    return slot0, slot1, cnt


def _mix_kernel(x_ref, pool_ref, attn_ref, wo_ref, g_ref, wr_ref, br_ref,
                h_ref, m_ref, ext_ref, slots_ref, slots_t_ref, cnt_ref, logits_s):
    pool_dim = pool_ref.shape[1]

    @pl.when(pl.program_id(0) == 0)
    def _():
        logits_s[...] = jnp.zeros_like(logits_s)

    prev_logits = logits_s[...]
    h = x_ref[...] + (_dot(pool_ref[...], wo_ref[0:pool_dim, :])
                      + _dot(attn_ref[...], wo_ref[pool_dim:, :]))
    h_ref[...] = h
    e0, e1, w0, w1 = _route(prev_logits)
    ext_ref[...] = _routing_record(e0, e1, w0, w1)
    slot0, slot1, cnt = _local_slots(e0, e1)
    cnt_ref[0] = cnt
    lane = lax.broadcasted_iota(jnp.int32, (1, LANES), 1)
    slots = jnp.where(lane == 0, slot0, jnp.where(lane == 1, slot1, 0.0))
    slots_ref[...] = slots
    slots_t_ref[0] = slots.T[0:8, :]
    m = _rms_norm(h, g_ref[...])
    m_ref[...] = m.astype(BF16)
    logits_s[...] = _dot_precise(m, wr_ref[...]) + br_ref[...]


def _mix(x2, pool, attn, w_out, g_ffn, w_router, b_router):
    n, d = x2.shape
    tm = ROW_TILE
    n_tiles = n // tm
    row = lambda i: (jnp.minimum(i, n_tiles - 1), 0)
    routed = lambda i: (jnp.maximum(i - 1, 0), 0)
    full2 = lambda i: (0, 0)
    return pl.pallas_call(
        _mix_kernel,
        grid=(n_tiles + 1,),
        in_specs=[
            pl.BlockSpec((tm, d), row),
            pl.BlockSpec((tm, pool.shape[1]), row),
            pl.BlockSpec((tm, attn.shape[1]), row),
            pl.BlockSpec(w_out.shape, full2),
            pl.BlockSpec((1, d), full2),
            pl.BlockSpec(w_router.shape, full2),
            pl.BlockSpec((1, LANES), full2),
        ],
        out_specs=[
            pl.BlockSpec((tm, d), row),
            pl.BlockSpec((tm, d), row),
            pl.BlockSpec((tm, LANES), routed),
            pl.BlockSpec((tm, LANES), routed),
            pl.BlockSpec((1, 8, tm), lambda i: (jnp.maximum(i - 1, 0), 0, 0)),
            pl.BlockSpec((1, 1, LANES), lambda i: (jnp.maximum(i - 1, 0), 0, 0)),
        ],
        out_shape=[
            jax.ShapeDtypeStruct((n, d), F32),
            jax.ShapeDtypeStruct((n, d), BF16),
            jax.ShapeDtypeStruct((n, LANES), BF16),
            jax.ShapeDtypeStruct((n, LANES), F32),
            jax.ShapeDtypeStruct((n_tiles, 8, tm), F32),
            jax.ShapeDtypeStruct((n_tiles, 1, LANES), F32),
        ],
        scratch_shapes=[pltpu.VMEM((tm, LANES), F32)],
        compiler_params=pltpu.CompilerParams(
            dimension_semantics=("arbitrary",), vmem_limit_bytes=VMEM_LIMIT),
        name="mix",
    )(x2, pool, attn, w_out, g_ffn, w_router, b_router)


SEG_ALIGN = 16
COPY_UNITS = 4
COPY_ROWS = COPY_UNITS * SEG_ALIGN
EXPERT_TILE = 512
LOCAL_ROWS = 2 * ROW_TILE + N_EXPERTS * SEG_ALIGN


def _permutation(slot0, slot1, slots_axis):
    shape = (1, LOCAL_ROWS) if slots_axis == 1 else (LOCAL_ROWS, 1)
    s = lax.broadcasted_iota(jnp.int32, shape, slots_axis)
    return ((s == slot0.astype(jnp.int32)) | (s == slot1.astype(jnp.int32))).astype(BF16)


def _for_each_piece(n8_ref, lrow_ref, grow_ref, tile, make_copy, act):
    base = tile * N_EXPERTS

    def per_expert(e, carry):
        n = n8_ref[base + e]
        l0 = lrow_ref[base + e] * SEG_ALIGN
        g0 = grow_ref[base + e] * SEG_ALIGN

        def piece(off, rows):
            act(make_copy(pl.multiple_of(l0 + off, SEG_ALIGN),
                          pl.multiple_of(g0 + off, SEG_ALIGN), rows))

        n_full = n // COPY_UNITS

        def per_block(c, carry):
            piece(c * COPY_ROWS, COPY_ROWS)
            return carry

        lax.fori_loop(0, n_full, per_block, 0)
        for b in reversed(range(COPY_UNITS.bit_length() - 1)):
            @pl.when(((n >> b) & 1) == 1)
            def _():
                higher = (n >> (b + 1)) << (b + 1)
                piece(higher * SEG_ALIGN, SEG_ALIGN << b)

        return carry

    lax.fori_loop(0, N_EXPERTS, per_expert, 0)


def _wait_rows(total_units, make_copy):
    for b in range((LOCAL_ROWS // SEG_ALIGN).bit_length()):
        @pl.when(((total_units >> b) & 1) == 1)
        def _():
            make_copy(0, 0, SEG_ALIGN << b).wait()


def _start(copy):
    copy.start()


def _wait(copy):
    copy.wait()


def _dispatch_kernel(n_steps, n8_ref, lrow_ref, grow_ref, tot_ref, tail0_ref, tailn_ref, nu_ref,
                     m_ref, ext_ref, st_ref, st_next_ref, xs_hbm,
                     buf, zeros, perm_even, perm_odd, sems, tail_sem):
    i = pl.program_id(0)
    slot = i % 2
    d = m_ref.shape[1]
    n_expert_tiles = xs_hbm.shape[0] // EXPERT_TILE

    def seg_copy(slot_):
        def make(l, g, rows):
            return pltpu.make_async_copy(buf.at[slot_, pl.ds(l, rows), :],
                                         xs_hbm.at[pl.ds(g, rows), :], sems.at[slot_])
        return make

    def for_each_tail(act):
        @pl.when(i < N_EXPERTS)
        def _():
            e = jnp.minimum(i, N_EXPERTS - 1)
            g0 = tail0_ref[e] * SEG_ALIGN

            def body(c, carry):
                g = pl.multiple_of(g0 + c * SEG_ALIGN, SEG_ALIGN)
                act(pltpu.make_async_copy(zeros.at[pl.ds(0, SEG_ALIGN), :],
                                          xs_hbm.at[pl.ds(g, SEG_ALIGN), :], tail_sem))
                return carry

            lax.fori_loop(0, tailn_ref[e], body, 0)

        for k in range(-(-n_expert_tiles // n_steps)):
            t = nu_ref[0] + i + k * n_steps

            @pl.when(t < n_expert_tiles)
            def _():
                g = pl.multiple_of(t * EXPERT_TILE, EXPERT_TILE)
                act(pltpu.make_async_copy(zeros, xs_hbm.at[pl.ds(g, EXPERT_TILE), :], tail_sem))

    @pl.when(i == 0)
    def _():
        zeros[...] = jnp.zeros_like(zeros)

    for_each_tail(_start)

    def build(slots_t, perm):
        perm[...] = _permutation(slots_t[0, 0:1, :], slots_t[0, 1:2, :], slots_axis=0)

    @pl.when(i == 0)
    def _():
        build(st_ref, perm_even)

    def permute(perm, next_perm):
        p = perm[...]
        build(st_next_ref, next_perm)
        buf[slot, :, 0:d] = _dot(p, m_ref[...]).astype(BF16)
        buf[slot, :, d:] = _dot(p, ext_ref[...]).astype(BF16)

    @pl.when(slot == 0)
    def _():
        permute(perm_even, perm_odd)

    @pl.when(slot == 1)
    def _():
        permute(perm_odd, perm_even)

    _for_each_piece(n8_ref, lrow_ref, grow_ref, i, seg_copy(slot), _start)

    @pl.when(i > 0)
    def _():
        _wait_rows(tot_ref[i - 1], seg_copy(1 - slot))

    for_each_tail(_wait)

    @pl.when(i == n_steps - 1)
    def _():
        _wait_rows(tot_ref[i], seg_copy(slot))


def _dispatch(m, ext, slots_t, tables, n_rows):
    n, d = m.shape
    tm = ROW_TILE
    assert n // tm >= N_EXPERTS, "each grid step zero-fills the tail of one expert"
    width = d + LANES
    last = n // tm - 1
    row = lambda i, *_: (i, 0)
    return pl.pallas_call(
        functools.partial(_dispatch_kernel, n // tm),
        grid_spec=pltpu.PrefetchScalarGridSpec(
            num_scalar_prefetch=len(tables),
            grid=(n // tm,),
            in_specs=[
                pl.BlockSpec((tm, d), row),
                pl.BlockSpec((tm, LANES), row),
                pl.BlockSpec((1, 8, tm), lambda i, *_: (i, 0, 0)),
                pl.BlockSpec((1, 8, tm), lambda i, *_: (jnp.minimum(i + 1, last), 0, 0)),
            ],
            out_specs=pl.BlockSpec(memory_space=pl.ANY),
            scratch_shapes=[
                pltpu.VMEM((2, LOCAL_ROWS, width), BF16),
                pltpu.VMEM((EXPERT_TILE, width), BF16),
                pltpu.VMEM((LOCAL_ROWS, tm), BF16),
                pltpu.VMEM((LOCAL_ROWS, tm), BF16),
                pltpu.SemaphoreType.DMA((2,)),
                pltpu.SemaphoreType.DMA(()),
            ],
        ),
        out_shape=jax.ShapeDtypeStruct((n_rows, width), BF16),
        compiler_params=pltpu.CompilerParams(
            dimension_semantics=("arbitrary",), vmem_limit_bytes=VMEM_LIMIT),
        name="dispatch",
    )(*tables, m, ext, slots_t, slots_t)


PART_ROWS = 128


def _expert_kernel(first_ref, count_ref, valid_ref, nu_ref, wg_ref, wu_ref, wd_ref, xs_hbm, os_hbm,
                   xbuf, obuf, zeros, wgu_s, wd_s, in_sems, out_sems, zero_sem):
    e = pl.program_id(0)
    d = os_hbm.shape[1]
    f = wg_ref.shape[2]
    n = count_ref[e]
    t0 = first_ref[e]
    n_tiles = os_hbm.shape[0] // EXPERT_TILE

    def rows(t):
        return pl.ds(pl.multiple_of(t * EXPERT_TILE, EXPERT_TILE), EXPERT_TILE)

    def in_copy(k, slot, first=t0):
        return pltpu.make_async_copy(xs_hbm.at[rows(first + k), :], xbuf.at[slot],
                                     in_sems.at[slot])

    def out_copy(k, slot):
        return pltpu.make_async_copy(obuf.at[slot], os_hbm.at[rows(t0 + k), :], out_sems.at[slot])

    def for_each_unused(act):
        for j in range(-(-n_tiles // N_EXPERTS)):
            t = nu_ref[0] + e + j * N_EXPERTS

            @pl.when(t < n_tiles)
            def _():
                act(pltpu.make_async_copy(zeros, os_hbm.at[rows(t), :], zero_sem))

    @pl.when(e == 0)
    def _():
        zeros[...] = jnp.zeros_like(zeros)

    for_each_unused(_start)

    @pl.when((e == 0) & (n > 0))
    def _():
        in_copy(0, 0).start()

    @pl.when(n > 0)
    def _():
        wgu_s[:, 0:f] = wg_ref[0].astype(BF16)
        wgu_s[:, f:] = wu_ref[0].astype(BF16)
        wd_s[...] = wd_ref[0].astype(BF16)

    lane = lax.broadcasted_iota(jnp.int32, (1, LANES), 1)

    def ffn(slot, m_rows):
        x = xbuf[slot, 0:m_rows, 0:d]
        rec = xbuf[slot, 0:m_rows, d:].astype(F32)

        def lanes_sum(first, count):
            keep = (lane >= first) & (lane < first + count)
            return jnp.sum(jnp.where(keep, rec, 0.0), axis=-1, keepdims=True)

        first_choice = lanes_sum(EXT_E0, 1) == e.astype(F32)
        w = jnp.where(first_choice, lanes_sum(EXT_W0, N_SPLIT), lanes_sum(EXT_W1, N_SPLIT))
        h = _dot(x, wgu_s[...])
        hg = h[:, 0:f]
        hu = h[:, f:]
        a = hg * jax.nn.sigmoid(hg) * hu * w
        obuf[slot, 0:m_rows, :] = _dot(a.astype(BF16), wd_s[...]).astype(BF16)
        if m_rows < EXPERT_TILE:
            obuf[slot, m_rows:, :] = jnp.zeros((EXPERT_TILE - m_rows, d), BF16)

    def tile(k, carry):
        slot = k % 2

        @pl.when(k + 1 < n)
        def _():
            in_copy(k + 1, 1 - slot).start()

        in_copy(k, slot).wait()

        @pl.when(k >= 2)
        def _():
            out_copy(k - 2, slot).wait()

        parts_used = jnp.minimum(
            (valid_ref[e] - k * EXPERT_TILE + PART_ROWS - 1) // PART_ROWS, EXPERT_TILE // PART_ROWS)
        for parts in range(1, EXPERT_TILE // PART_ROWS + 1):
            @pl.when(parts_used == parts)
            def _():
                ffn(slot, parts * PART_ROWS)

        out_copy(k, slot).start()
        return carry

    lax.fori_loop(0, n, tile, 0)

    for back in (2, 1):
        @pl.when(n >= back)
        def _():
            out_copy(n - back, (n - back) % 2).wait()

    nxt = jnp.minimum(e + 1, N_EXPERTS - 1)

    @pl.when((e + 1 < N_EXPERTS) & (count_ref[nxt] > 0))
    def _():
        in_copy(0, 0, first_ref[nxt]).start()

    for_each_unused(_wait)


def _experts(xs, first_tile, tile_count, valid_rows, n_used, w_gate, w_up, w_down):
    n_rows, width = xs.shape
    n_exp, d, f = w_gate.shape
    te = EXPERT_TILE
    weight = lambda e, *_: (e, 0, 0)
    return pl.pallas_call(
        _expert_kernel,
        grid_spec=pltpu.PrefetchScalarGridSpec(
            num_scalar_prefetch=4,
            grid=(n_exp,),
            in_specs=[
                pl.BlockSpec((1, d, f), weight),
                pl.BlockSpec((1, d, f), weight),
                pl.BlockSpec((1, f, d), weight),
                pl.BlockSpec(memory_space=pl.ANY),
            ],
            out_specs=pl.BlockSpec(memory_space=pl.ANY),
            scratch_shapes=[
                pltpu.VMEM((2, te, width), BF16),
                pltpu.VMEM((2, te, d), BF16),
                pltpu.VMEM((te, d), BF16),
                pltpu.VMEM((d, 2 * f), BF16), pltpu.VMEM((f, d), BF16),
                pltpu.SemaphoreType.DMA((2,)),
                pltpu.SemaphoreType.DMA((2,)),
                pltpu.SemaphoreType.DMA(()),
            ],
        ),
        out_shape=jax.ShapeDtypeStruct((n_rows, d), BF16),
        compiler_params=pltpu.CompilerParams(
            dimension_semantics=("arbitrary",), vmem_limit_bytes=VMEM_LIMIT),
        name="experts",
    )(first_tile, tile_count, valid_rows, n_used, w_gate, w_up, w_down, xs)


def _moe_layout(cnt, n_pairs):
    c = cnt[:, 0, :N_EXPERTS].astype(jnp.int32)
    n_tiles = c.shape[0]
    n8 = (c + SEG_ALIGN - 1) // SEG_ALIGN
    lrow = jnp.cumsum(n8, axis=1) - n8
    units_e = jnp.sum(n8, axis=0)
    per_tile = EXPERT_TILE // SEG_ALIGN
    tiles_e = (units_e + per_tile - 1) // per_tile
    e_end = jnp.cumsum(tiles_e)
    e_off = (e_end - tiles_e) * per_tile
    grow = e_off[None, :] + jnp.cumsum(n8, axis=0) - n8
    tail0 = e_off + units_e
    tailn = tiles_e * per_tile - units_e
    worst = n_pairs + n_tiles * N_EXPERTS * (SEG_ALIGN - 1) + N_EXPERTS * (EXPERT_TILE - SEG_ALIGN)
    n_rows = -(-worst // EXPERT_TILE) * EXPERT_TILE
    seg_tables = (n8.reshape(-1), lrow.reshape(-1), grow.reshape(-1), jnp.sum(n8, axis=1))
    expert_tables = (e_end - tiles_e, tiles_e, units_e * SEG_ALIGN, e_end[-1:])
    return seg_tables, (tail0, tailn), expert_tables, n_rows


def _combine_kernel(n8_ref, lrow_ref, grow_ref, tot_ref,
                    h_ref, slots_ref, p_ref, g_ref, wg_ref, wp_ref, gf_ref, os_hbm,
                    o_ref, buf, sems):
    i = pl.program_id(0)
    nt = pl.num_programs(0)
    slot = i % 2

    def seg_copy(slot_):
        def make(l, g, rows):
            return pltpu.make_async_copy(os_hbm.at[pl.ds(g, rows), :],
                                         buf.at[slot_, pl.ds(l, rows), :], sems.at[slot_])
        return make

    @pl.when(i == 0)
    def _():
        buf[...] = jnp.zeros_like(buf)
        _for_each_piece(n8_ref, lrow_ref, grow_ref, 0, seg_copy(0), _start)

    @pl.when(i + 1 < nt)
    def _():
        _for_each_piece(n8_ref, lrow_ref, grow_ref, i + 1, seg_copy(1 - slot), _start)

    _wait_rows(tot_ref[i], seg_copy(slot))

    slots = slots_ref[...]
    y = _dot(_permutation(slots[:, 0:1], slots[:, 1:2], slots_axis=1), buf[slot])
    h = h_ref[...] + y
    gate = jax.nn.sigmoid(_dot(_rms_norm(h, g_ref[...]).astype(BF16), wg_ref[...]))
    h = h + gate * _dot(p_ref[...].astype(BF16), wp_ref[...])
    o_ref[...] = _rms_norm(h, gf_ref[...])


def _combine(h1, slots, seg_tables, o_sorted, p2, g_ple, w_gate, w_proj, g_final):
    n, d = h1.shape
    tm = ROW_TILE
    row = lambda i, *_: (i, 0)
    full2 = lambda i, *_: (0, 0)
    return pl.pallas_call(
        _combine_kernel,
        grid_spec=pltpu.PrefetchScalarGridSpec(
            num_scalar_prefetch=len(seg_tables),
            grid=(n // tm,),
            in_specs=[
                pl.BlockSpec((tm, d), row),
                pl.BlockSpec((tm, LANES), row),
                pl.BlockSpec((tm, p2.shape[1]), row),
                pl.BlockSpec((1, d), full2),
                pl.BlockSpec(w_gate.shape, full2),
                pl.BlockSpec(w_proj.shape, full2),
                pl.BlockSpec((1, d), full2),
                pl.BlockSpec(memory_space=pl.ANY),
            ],
            out_specs=pl.BlockSpec((tm, d), row),
            scratch_shapes=[
                pltpu.VMEM((2, LOCAL_ROWS, d), BF16),
                pltpu.SemaphoreType.DMA((2,)),
            ],
        ),
        out_shape=jax.ShapeDtypeStruct((n, d), F32),
        compiler_params=pltpu.CompilerParams(
            dimension_semantics=("arbitrary",), vmem_limit_bytes=VMEM_LIMIT),
        name="combine",
    )(*seg_tables, h1, slots, p2, g_ple, w_gate, w_proj, g_final, o_sorted)


def _pad_lanes(a):
    return jnp.pad(a, ((0, 0), (0, LANES - a.shape[1])))


def kernel(x, p, g_mix, w_in, b_f, w_pool, s_pool, w_out, g_ffn, w_grp, b_grp, w_rt, b_rt,
           w_e_gate, w_e_up, w_e_down, g_ple, w_ple_gate, w_ple_proj, g_final):
    batch, seq_len, d = x.shape
    n = batch * seq_len
    assert w_in.shape[0] == 1, "single-layer stack only: the final norm is fused into the layer"
    i = 0
    pool_dim = s_pool.shape[1]
    attn_dim = N_HEADS * HEAD_DIM
    main = pool_dim + 3 * attn_dim
    h = x.reshape(n, d)
    w_f = _pad_lanes(jnp.tile(w_in[i, :, main:], (1, N_SPLIT)))
    w_main = jnp.concatenate([w_in[i, :, :main], w_f], axis=1).astype(BF16)
    b_f3 = _pad_lanes(jnp.tile(b_f[i], N_SPLIT)[None])
    pool, q, k, v = _inproj(h, g_mix[i][None], w_main, b_f3,
                            w_pool[i].astype(BF16), s_pool[i][None], seq_len)
    attn = _attention(q, k, v, batch, seq_len)
    w_router = _pad_lanes(jnp.concatenate([w_grp[i], w_rt[i]], axis=1))
    b_router = _pad_lanes(jnp.concatenate([b_grp[i], b_rt[i]])[None])
    h1, m, ext, slots, slots_t, cnt = _mix(h, pool, attn, w_out[i].astype(BF16), g_ffn[i][None],
                                          w_router, b_router)
    seg_tables, tail_tables, expert_tables, n_rows = _moe_layout(cnt, 2 * n)
    x_sorted = _dispatch(m, ext, slots_t, seg_tables + tail_tables + expert_tables[3:], n_rows)
    f = w_e_gate.shape[-1]
    o_sorted = _experts(x_sorted, *expert_tables,
                        w_e_gate[i].reshape(N_EXPERTS, d, f),
                        w_e_up[i].reshape(N_EXPERTS, d, f),
                        w_e_down[i].reshape(N_EXPERTS, f, d))
    out = _combine(h1, slots, seg_tables, o_sorted, p[i].reshape(n, -1), g_ple[i][None],
                   w_ple_gate[i].astype(BF16), w_ple_proj[i].astype(BF16), g_final[None])
    return out.reshape(batch, seq_len, d)
```

```python
import functools
import math

import jax
import jax.numpy as jnp
from jax import lax
from jax.experimental import pallas as pl
from jax.experimental.pallas import tpu as pltpu

HEAD_DIM = 64
N_HEADS = 8
POOL_WINDOWS = (2, 4, 8, 16)
POOL_GROUP_DIM = 128
POOL_HISTORY = 16
N_GROUPS = 4
EXPERTS_PER_GROUP = 8
N_EXPERTS = N_GROUPS * EXPERTS_PER_GROUP
EPS = 1e-6
LANES = 128
ROUTER_LANE0 = N_GROUPS
NEG_INF = float("-inf")
LOG2E = math.log2(math.e)
N_SPLIT = 3

ROW_TILE = 512
ATTN_Q = 512
VMEM_LIMIT = 48 * 1024 * 1024

BF16 = jnp.bfloat16
F32 = jnp.float32


def _dot(a, b):
    return jnp.dot(a, b, preferred_element_type=F32)


def _split2(a):
    hi = a.astype(BF16)
    lo = (a - hi.astype(F32)).astype(BF16)
    return hi, lo


def _split3(a):
    hi = a.astype(BF16)
    r = a - hi.astype(F32)
    mid = r.astype(BF16)
    lo = (r - mid.astype(F32)).astype(BF16)
    return hi, mid, lo


def _dot_precise(a, w):
    a1, a2 = _split2(a)
    w1, w2 = _split2(w)
    n = w.shape[1]
    r = _dot(a1, jnp.concatenate([w1, w2], axis=1))
    return r[:, 0:n] + (r[:, n:] + _dot(a2, w1))


def _rms_norm(x, g):
    return x * lax.rsqrt(jnp.mean(x * x, axis=-1, keepdims=True) + EPS) * g


def _own_half_start(head):
    return 0 if head % 2 == 0 else HEAD_DIM


def _inproj_kernel(tiles_per_seq, x_ref, g_ref, w_ref, bf_ref, wp_ref, sp_ref,
                   pool_ref, q_ref, k_ref, v_ref,
                   carry_c, carry_u, u_s, q_s, k_s, v_s, c_s):
    i = pl.program_id(0)
    tm = x_ref.shape[0]
    pool_dim = pool_ref.shape[1]
    attn_dim = N_HEADS * HEAD_DIM

    @pl.when(i == 0)
    def _():
        for ref in (u_s, q_s, k_s, v_s, c_s, carry_u):
            ref[...] = jnp.zeros_like(ref)

    @pl.when(i % tiles_per_seq == 0)
    def _():
        carry_c[...] = jnp.zeros_like(carry_c)

    prev_seq_tile = (i + tiles_per_seq - 1) % tiles_per_seq

    @pl.when(prev_seq_tile == 0)
    def _():
        carry_u[...] = jnp.zeros_like(carry_u)

    u_prev, c_prev = u_s[...], c_s[...]
    q_prev, k_prev, v_prev = q_s[...], k_s[...], v_s[...]

    a = _rms_norm(x_ref[...], g_ref[...])
    ab = a.astype(BF16)
    o0 = pool_dim
    u = _dot(ab, w_ref[:, 0:o0])
    qf = _dot(ab, w_ref[:, o0:o0 + attn_dim]) * (LOG2E * HEAD_DIM ** -0.5)
    kf = _dot(ab, w_ref[:, o0 + attn_dim:o0 + 2 * attn_dim])
    vf = _dot(ab, w_ref[:, o0 + 2 * attn_dim:o0 + 3 * attn_dim])
    fl = _dot(ab, w_ref[:, o0 + 3 * attn_dim:]) + bf_ref[...]

    c1, c2, c3 = _split3(c_prev * LOG2E)
    lane = lax.broadcasted_iota(jnp.int32, (1, LANES), 1)
    piece_id = lane % N_SPLIT
    pieces = jnp.where(piece_id == 0, c1, jnp.where(piece_id == 1, c2, c3)).astype(F32)
    for h in range(N_HEADS):
        own = (lane >= _own_half_start(h)) & (lane < _own_half_start(h) + HEAD_DIM)
        pair = slice((h // 2) * LANES, (h // 2 + 1) * LANES)
        blk = slice(h * LANES, (h + 1) * LANES)
        spare = HEAD_DIM - _own_half_start(h)
        first = (lane >= spare) & (lane < spare + N_SPLIT)
        second = (lane >= spare + N_SPLIT) & (lane < spare + 2 * N_SPLIT)
        to_first = pltpu.roll(pieces, (spare - h * N_SPLIT) % LANES, axis=1)
        to_second = pltpu.roll(pieces, (spare + N_SPLIT - h * N_SPLIT) % LANES, axis=1)
        bias_q = jnp.where(first, to_first, jnp.where(second, 1.0, 0.0))
        bias_k = jnp.where(second, -to_second, jnp.where(first, 1.0, 0.0))
        q_ref[:, blk] = jnp.where(own, q_prev[:, pair], bias_q).astype(BF16)
        k_ref[:, blk] = jnp.where(own, k_prev[:, pair], bias_k).astype(BF16)
        one_col = (lane == HEAD_DIM - _own_half_start(h)).astype(F32)
        v_ref[:, blk] = jnp.where(own, v_prev[:, pair], one_col).astype(BF16)

    ext = jnp.concatenate([carry_u[...], u_prev], axis=0)
    carry_u[...] = u_prev[tm - POOL_HISTORY:, :]
    pos = (prev_seq_tile * tm + 1 + lax.broadcasted_iota(jnp.int32, (tm, 1), 0)).astype(F32)
    for gi, w in enumerate(POOL_WINDOWS):
        lo, hi = gi * POOL_GROUP_DIM, (gi + 1) * POOL_GROUP_DIM
        s = ext[:, lo:hi]
        shift = 1
        while shift < w:
            s = s + pltpu.roll(s, shift, axis=0)
            shift *= 2
        mean = s[POOL_HISTORY:, :] / jnp.minimum(pos, float(w))
        d = mean - u_prev[:, lo:hi]
        y = _dot(d.astype(BF16), wp_ref[gi]) * sp_ref[:, lo:hi]
        pool_ref[:, lo:hi] = y.astype(BF16)

    lf = jnp.minimum(fl, 0.0) - jnp.log1p(jnp.exp(-jnp.abs(fl)))
    row = lax.broadcasted_iota(jnp.int32, (tm, tm), 0)
    col = lax.broadcasted_iota(jnp.int32, (tm, tm), 1)
    tril = (col <= row).astype(BF16)
    sums = _dot(tril, jnp.concatenate(_split3(lf), axis=1))
    c = carry_c[...] + (sums[:, 0:LANES] + (sums[:, LANES:2 * LANES] + sums[:, 2 * LANES:]))
    carry_c[...] = c[tm - 1:tm, :]
    u_s[...] = u
    q_s[...] = qf
    k_s[...] = kf
    v_s[...] = vf
    c_s[...] = c


def _inproj(x2, g_mix, w_main, b_f, w_pool, s_pool, seq_len):
    n, d = x2.shape
    tm = ROW_TILE
    pool_dim = w_pool.shape[0] * w_pool.shape[1]
    head_w = N_HEADS * LANES
    attn_dim = N_HEADS * HEAD_DIM
    n_tiles = n // tm
    projected = lambda i: (jnp.minimum(i, n_tiles - 1), 0)
    finished = lambda i: (jnp.maximum(i - 1, 0), 0)
    full2 = lambda i: (0, 0)
    return pl.pallas_call(
        functools.partial(_inproj_kernel, seq_len // tm),
        grid=(n_tiles + 1,),
        in_specs=[
            pl.BlockSpec((tm, d), projected),
            pl.BlockSpec((1, d), full2),
            pl.BlockSpec(w_main.shape, full2),
            pl.BlockSpec((1, LANES), full2),
            pl.BlockSpec(w_pool.shape, lambda i: (0, 0, 0)),
            pl.BlockSpec((1, pool_dim), full2),
        ],
        out_specs=[
            pl.BlockSpec((tm, pool_dim), finished),
            pl.BlockSpec((tm, head_w), finished),
            pl.BlockSpec((tm, head_w), finished),
            pl.BlockSpec((tm, head_w), finished),
        ],
        out_shape=[
            jax.ShapeDtypeStruct((n, pool_dim), BF16),
            jax.ShapeDtypeStruct((n, head_w), BF16),
            jax.ShapeDtypeStruct((n, head_w), BF16),
            jax.ShapeDtypeStruct((n, head_w), BF16),
        ],
        scratch_shapes=[
            pltpu.VMEM((1, LANES), F32), pltpu.VMEM((POOL_HISTORY, pool_dim), F32),
            pltpu.VMEM((tm, pool_dim), F32), pltpu.VMEM((tm, attn_dim), F32),
            pltpu.VMEM((tm, attn_dim), F32), pltpu.VMEM((tm, attn_dim), F32),
            pltpu.VMEM((tm, LANES), F32),
        ],
        compiler_params=pltpu.CompilerParams(
            dimension_semantics=("arbitrary",), vmem_limit_bytes=VMEM_LIMIT),
        name="inproj",
    )(x2, g_mix, w_main, b_f, w_pool, s_pool)


def _attend_pair(q_ref, k_ref, v_ref, n_keys):
    tq = q_ref.shape[0]
    past = n_keys - tq
    nt = (((1,), (1,)), ((), ()))
    blks = [slice(hh * LANES, (hh + 1) * LANES) for hh in range(2)]
    row = lax.broadcasted_iota(jnp.int32, (tq, tq), 0)
    col = lax.broadcasted_iota(jnp.int32, (tq, tq), 1)
    qs = [q_ref[:, blk] for blk in blks]
    s_diag = [jnp.where(col <= row,
                        lax.dot_general(q, k_ref[past:n_keys, blk], nt,
                                        preferred_element_type=F32), NEG_INF)
              for q, blk in zip(qs, blks)]
    m = [jnp.max(s, axis=-1, keepdims=True) for s in s_diag]
    if past:
        s_past = [lax.dot_general(q, k_ref[0:past, blk], nt, preferred_element_type=F32)
                  for q, blk in zip(qs, blks)]
        m = [jnp.maximum(mi, jnp.max(s, axis=-1, keepdims=True)) for mi, s in zip(m, s_past)]
    acc = [_dot(jnp.exp2(s - mi).astype(BF16), v_ref[past:n_keys, blk])
           for s, mi, blk in zip(s_diag, m, blks)]
    if past:
        acc = [a + _dot(jnp.exp2(s - mi).astype(BF16), v_ref[0:past, blk])
               for a, s, mi, blk in zip(acc, s_past, m, blks)]
    return acc


def _attn_kernel(q_ref, k_ref, v_ref, o_ref):
    qi = pl.program_id(2)
    tq = q_ref.shape[0]
    lane = lax.broadcasted_iota(jnp.int32, (1, LANES), 1)
    for tile in range(k_ref.shape[0] // tq):
        @pl.when(qi == tile)
        def _():
            outs = []
            for hh, acc in enumerate(_attend_pair(q_ref, k_ref, v_ref, (tile + 1) * tq)):
                sum_lane = HEAD_DIM - _own_half_start(hh)
                l = jnp.sum(jnp.where(lane == sum_lane, acc, 0.0), axis=-1, keepdims=True)
                outs.append(acc / l)
            o_ref[...] = jnp.where(lane < HEAD_DIM, outs[0], outs[1]).astype(BF16)


def _attention(q, k, v, batch, seq_len):
    n = q.shape[0]
    pairs = N_HEADS // 2
    tq = ATTN_Q
    qt = seq_len // tq
    pair_block = lambda b, hp, qi: (b, hp)
    return pl.pallas_call(
        _attn_kernel,
        grid=(batch, pairs, qt),
        in_specs=[
            pl.BlockSpec((tq, 2 * LANES), lambda b, hp, qi: (b * qt + qi, hp)),
            pl.BlockSpec((seq_len, 2 * LANES), pair_block),
            pl.BlockSpec((seq_len, 2 * LANES), pair_block),
        ],
        out_specs=pl.BlockSpec((tq, LANES), lambda b, hp, qi: (b * qt + qi, hp)),
        out_shape=jax.ShapeDtypeStruct((n, pairs * LANES), BF16),
        compiler_params=pltpu.CompilerParams(
            dimension_semantics=("arbitrary", "arbitrary", "arbitrary"),
            vmem_limit_bytes=VMEM_LIMIT),
        name="attn",
    )(q, k, v)


def _route(logits):
    lane = lax.broadcasted_iota(jnp.int32, logits.shape, 1)
    big = jnp.int32(LANES)
    gl = jnp.where(lane < N_GROUPS, logits, NEG_INF)
    gmax = jnp.max(gl, axis=-1, keepdims=True)
    g_w = 1.0 / jnp.sum(jnp.exp(gl - gmax), axis=-1, keepdims=True)
    g_idx = jnp.min(jnp.where(gl == gmax, lane, big), axis=-1, keepdims=True)
    e_lo = ROUTER_LANE0 + EXPERTS_PER_GROUP * g_idx
    el = jnp.where((lane >= e_lo) & (lane < e_lo + EXPERTS_PER_GROUP), logits, NEG_INF)
    v1 = jnp.max(el, axis=-1, keepdims=True)
    i1 = jnp.min(jnp.where(el == v1, lane, big), axis=-1, keepdims=True)
    el2 = jnp.where(lane == i1, NEG_INF, el)
    v2 = jnp.max(el2, axis=-1, keepdims=True)
    i2 = jnp.min(jnp.where(el2 == v2, lane, big), axis=-1, keepdims=True)
    e2 = jnp.exp(v2 - v1)
    w1 = g_w / (1.0 + e2)
    w2 = g_w * e2 / (1.0 + e2)
    return (i1 - ROUTER_LANE0).astype(F32), (i2 - ROUTER_LANE0).astype(F32), w1, w2


EXT_W0 = 0
EXT_W1 = N_SPLIT
EXT_E0 = 2 * N_SPLIT


def _routing_record(e0, e1, w0, w1):
    lane = lax.broadcasted_iota(jnp.int32, (1, LANES), 1)
    rec = jnp.where(lane == EXT_E0, e0, jnp.where(lane == EXT_E0 + 1, e1, 0.0))
    for first, w in ((EXT_W0, w0), (EXT_W1, w1)):
        for k, piece in enumerate(_split3(w)):
            rec = jnp.where(lane == first + k, piece.astype(F32), rec)
    return rec.astype(BF16)


def _local_slots(e0, e1):
    t = e0.shape[0]
    lane = lax.broadcasted_iota(jnp.int32, (t, LANES), 1).astype(F32)
    oh0 = lane == e0
    oh1 = lane == e1
    picked = (oh0 | oh1).astype(BF16)
    cnt = jnp.sum(picked.astype(F32), axis=0, keepdims=True)
    units = jnp.floor((cnt + (SEG_ALIGN - 1)) * (1.0 / SEG_ALIGN))
    r128 = lax.broadcasted_iota(jnp.int32, (LANES, LANES), 0)
    c128 = lax.broadcasted_iota(jnp.int32, (LANES, LANES), 1)
    before = (r128 < c128).astype(BF16)
    lstart = SEG_ALIGN * _dot(jnp.broadcast_to(units, (8, LANES)).astype(BF16), before)[0:1, :]
    row = lax.broadcasted_iota(jnp.int32, (t, t), 0)
    col = lax.broadcasted_iota(jnp.int32, (t, t), 1)
    earlier = (col < row).astype(BF16)
    base = _dot(earlier, picked) + lstart
    slot0 = jnp.sum(jnp.where(oh0, base, 0.0), axis=-1, keepdims=True)
    slot1 = jnp.sum(jnp.where(oh1, base, 0.0), axis=-1, keepdims=True)
    return slot0, slot1, cnt


def _mix_kernel(x_ref, pool_ref, attn_ref, wo_ref, g_ref, wr_ref, br_ref,
                h_ref, m_ref, ext_ref, slots_ref, slots_t_ref, cnt_ref, logits_s):
    pool_dim = pool_ref.shape[1]

    @pl.when(pl.program_id(0) == 0)
    def _():
        logits_s[...] = jnp.zeros_like(logits_s)

    prev_logits = logits_s[...]
    h = x_ref[...] + (_dot(pool_ref[...], wo_ref[0:pool_dim, :])
                      + _dot(attn_ref[...], wo_ref[pool_dim:, :]))
    h_ref[...] = h
    e0, e1, w0, w1 = _route(prev_logits)
    ext_ref[...] = _routing_record(e0, e1, w0, w1)
    slot0, slot1, cnt = _local_slots(e0, e1)
    cnt_ref[0] = cnt
    lane = lax.broadcasted_iota(jnp.int32, (1, LANES), 1)
    slots = jnp.where(lane == 0, slot0, jnp.where(lane == 1, slot1, 0.0))
    slots_ref[...] = slots
    slots_t_ref[0] = slots.T[0:8, :]
    m = _rms_norm(h, g_ref[...])
    m_ref[...] = m.astype(BF16)
    logits_s[...] = _dot_precise(m, wr_ref[...]) + br_ref[...]


def _mix(x2, pool, attn, w_out, g_ffn, w_router, b_router):
    n, d = x2.shape
    tm = ROW_TILE
    n_tiles = n // tm
    row = lambda i: (jnp.minimum(i, n_tiles - 1), 0)
    routed = lambda i: (jnp.maximum(i - 1, 0), 0)
    full2 = lambda i: (0, 0)
    return pl.pallas_call(
        _mix_kernel,
        grid=(n_tiles + 1,),
        in_specs=[
            pl.BlockSpec((tm, d), row),
            pl.BlockSpec((tm, pool.shape[1]), row),
            pl.BlockSpec((tm, attn.shape[1]), row),
            pl.BlockSpec(w_out.shape, full2),
            pl.BlockSpec((1, d), full2),
            pl.BlockSpec(w_router.shape, full2),
            pl.BlockSpec((1, LANES), full2),
        ],
        out_specs=[
            pl.BlockSpec((tm, d), row),
            pl.BlockSpec((tm, d), row),
            pl.BlockSpec((tm, LANES), routed),
            pl.BlockSpec((tm, LANES), routed),
            pl.BlockSpec((1, 8, tm), lambda i: (jnp.maximum(i - 1, 0), 0, 0)),
            pl.BlockSpec((1, 1, LANES), lambda i: (jnp.maximum(i - 1, 0), 0, 0)),
        ],
        out_shape=[
            jax.ShapeDtypeStruct((n, d), F32),
            jax.ShapeDtypeStruct((n, d), BF16),
            jax.ShapeDtypeStruct((n, LANES), BF16),
            jax.ShapeDtypeStruct((n, LANES), F32),
            jax.ShapeDtypeStruct((n_tiles, 8, tm), F32),
            jax.ShapeDtypeStruct((n_tiles, 1, LANES), F32),
        ],
        scratch_shapes=[pltpu.VMEM((tm, LANES), F32)],
        compiler_params=pltpu.CompilerParams(
            dimension_semantics=("arbitrary",), vmem_limit_bytes=VMEM_LIMIT),
        name="mix",
    )(x2, pool, attn, w_out, g_ffn, w_router, b_router)


SEG_ALIGN = 16
COPY_UNITS = 4
COPY_ROWS = COPY_UNITS * SEG_ALIGN
EXPERT_TILE = 512
LOCAL_ROWS = 2 * ROW_TILE + N_EXPERTS * SEG_ALIGN


TRIM_ROWS = 128
TRIM_OPTIONS = 3


def _used_row_options():
    return [LOCAL_ROWS - k * TRIM_ROWS for k in reversed(range(TRIM_OPTIONS))]


def _round_up_to_option(rows, options):
    out = jnp.int32(options[-1])
    for opt in reversed(options[:-1]):
        out = jnp.where(rows <= opt, jnp.int32(opt), out)
    return out


def _permutation(slot0, slot1, slots_axis, n_slots=None):
    n_slots = LOCAL_ROWS if n_slots is None else n_slots
    shape = (1, n_slots) if slots_axis == 1 else (n_slots, 1)
    s = lax.broadcasted_iota(jnp.int32, shape, slots_axis)
    return ((s == slot0.astype(jnp.int32)) | (s == slot1.astype(jnp.int32))).astype(BF16)


def _for_each_piece(n8_ref, lrow_ref, grow_ref, tile, make_copy, act):
    base = tile * N_EXPERTS

    def per_expert(e, carry):
        n = n8_ref[base + e]
        l0 = lrow_ref[base + e] * SEG_ALIGN
        g0 = grow_ref[base + e] * SEG_ALIGN

        def piece(off, rows):
            act(make_copy(pl.multiple_of(l0 + off, SEG_ALIGN),
                          pl.multiple_of(g0 + off, SEG_ALIGN), rows))

        n_full = n // COPY_UNITS

        def per_block(c, carry):
            piece(c * COPY_ROWS, COPY_ROWS)
            return carry

        lax.fori_loop(0, n_full, per_block, 0)
        for b in reversed(range(COPY_UNITS.bit_length() - 1)):
            @pl.when(((n >> b) & 1) == 1)
            def _():
                higher = (n >> (b + 1)) << (b + 1)
                piece(higher * SEG_ALIGN, SEG_ALIGN << b)

        return carry

    lax.fori_loop(0, N_EXPERTS, per_expert, 0)


def _wait_rows(total_units, make_copy):
    for b in range((LOCAL_ROWS // SEG_ALIGN).bit_length()):
        @pl.when(((total_units >> b) & 1) == 1)
        def _():
            make_copy(0, 0, SEG_ALIGN << b).wait()


def _start(copy):
    copy.start()


def _wait(copy):
    copy.wait()


def _dispatch_kernel(n_steps, n8_ref, lrow_ref, grow_ref, tot_ref, tail0_ref, tailn_ref, nu_ref,
                     m_ref, ext_ref, st_ref, st_next_ref, xs_hbm,
                     buf, zeros, perm_even, perm_odd, sems, tail_sem):
    i = pl.program_id(0)
    slot = i % 2
    d = m_ref.shape[1]
    n_expert_tiles = xs_hbm.shape[0] // EXPERT_TILE

    def seg_copy(slot_):
        def make(l, g, rows):
            return pltpu.make_async_copy(buf.at[slot_, pl.ds(l, rows), :],
                                         xs_hbm.at[pl.ds(g, rows), :], sems.at[slot_])
        return make

    def for_each_tail(act):
        @pl.when(i < N_EXPERTS)
        def _():
            e = jnp.minimum(i, N_EXPERTS - 1)
            g0 = tail0_ref[e] * SEG_ALIGN

            def body(c, carry):
                g = pl.multiple_of(g0 + c * SEG_ALIGN, SEG_ALIGN)
                act(pltpu.make_async_copy(zeros.at[pl.ds(0, SEG_ALIGN), :],
                                          xs_hbm.at[pl.ds(g, SEG_ALIGN), :], tail_sem))
                return carry

            lax.fori_loop(0, tailn_ref[e], body, 0)

        for k in range(-(-n_expert_tiles // n_steps)):
            t = nu_ref[0] + i + k * n_steps

            @pl.when(t < n_expert_tiles)
            def _():
                g = pl.multiple_of(t * EXPERT_TILE, EXPERT_TILE)
                act(pltpu.make_async_copy(zeros, xs_hbm.at[pl.ds(g, EXPERT_TILE), :], tail_sem))

    @pl.when(i == 0)
    def _():
        zeros[...] = jnp.zeros_like(zeros)

    for_each_tail(_start)

    def build(slots_t, perm):
        perm[...] = _permutation(slots_t[0, 0:1, :], slots_t[0, 1:2, :], slots_axis=0)

    @pl.when(i == 0)
    def _():
        build(st_ref, perm_even)

    def permute(perm, next_perm, rows):
        p = perm[0:rows, :]
        build(st_next_ref, next_perm)
        buf[slot, 0:rows, 0:d] = _dot(p, m_ref[...]).astype(BF16)
        buf[slot, 0:rows, d:] = _dot(p, ext_ref[...]).astype(BF16)

    row_options = _used_row_options()
    used = _round_up_to_option(tot_ref[i] * SEG_ALIGN, row_options)
    for parity, (perm, next_perm) in enumerate(((perm_even, perm_odd), (perm_odd, perm_even))):
        for rows in row_options:
            @pl.when((slot == parity) & (used == rows))
            def _():
                permute(perm, next_perm, rows)

    _for_each_piece(n8_ref, lrow_ref, grow_ref, i, seg_copy(slot), _start)

    @pl.when(i > 0)
    def _():
        _wait_rows(tot_ref[i - 1], seg_copy(1 - slot))

    for_each_tail(_wait)

    @pl.when(i == n_steps - 1)
    def _():
        _wait_rows(tot_ref[i], seg_copy(slot))


def _dispatch(m, ext, slots_t, tables, n_rows):
    n, d = m.shape
    tm = ROW_TILE
    assert n // tm >= N_EXPERTS, "each grid step zero-fills the tail of one expert"
    width = d + LANES
    last = n // tm - 1
    row = lambda i, *_: (i, 0)
    return pl.pallas_call(
        functools.partial(_dispatch_kernel, n // tm),
        grid_spec=pltpu.PrefetchScalarGridSpec(
            num_scalar_prefetch=len(tables),
            grid=(n // tm,),
            in_specs=[
                pl.BlockSpec((tm, d), row),
                pl.BlockSpec((tm, LANES), row),
                pl.BlockSpec((1, 8, tm), lambda i, *_: (i, 0, 0)),
                pl.BlockSpec((1, 8, tm), lambda i, *_: (jnp.minimum(i + 1, last), 0, 0)),
            ],
            out_specs=pl.BlockSpec(memory_space=pl.ANY),
            scratch_shapes=[
                pltpu.VMEM((2, LOCAL_ROWS, width), BF16),
                pltpu.VMEM((EXPERT_TILE, width), BF16),
                pltpu.VMEM((LOCAL_ROWS, tm), BF16),
                pltpu.VMEM((LOCAL_ROWS, tm), BF16),
                pltpu.SemaphoreType.DMA((2,)),
                pltpu.SemaphoreType.DMA(()),
            ],
        ),
        out_shape=jax.ShapeDtypeStruct((n_rows, width), BF16),
        compiler_params=pltpu.CompilerParams(
            dimension_semantics=("arbitrary",), vmem_limit_bytes=VMEM_LIMIT),
        name="dispatch",
    )(*tables, m, ext, slots_t, slots_t)


PART_ROWS = 128


def _expert_kernel(first_ref, count_ref, valid_ref, nu_ref, wg_ref, wu_ref, wd_ref, xs_hbm, os_hbm,
                   xbuf, obuf, zeros, wgu_s, wd_s, in_sems, out_sems, zero_sem):
    e = pl.program_id(0)
    d = os_hbm.shape[1]
    f = wg_ref.shape[2]
    n = count_ref[e]
    t0 = first_ref[e]
    n_tiles = os_hbm.shape[0] // EXPERT_TILE

    def rows(t):
        return pl.ds(pl.multiple_of(t * EXPERT_TILE, EXPERT_TILE), EXPERT_TILE)

    def in_copy(k, slot, first=t0):
        return pltpu.make_async_copy(xs_hbm.at[rows(first + k), :], xbuf.at[slot],
                                     in_sems.at[slot])

    def out_copy(k, slot):
        return pltpu.make_async_copy(obuf.at[slot], os_hbm.at[rows(t0 + k), :], out_sems.at[slot])

    def for_each_unused(act):
        for j in range(-(-n_tiles // N_EXPERTS)):
            t = nu_ref[0] + e + j * N_EXPERTS

            @pl.when(t < n_tiles)
            def _():
                act(pltpu.make_async_copy(zeros, os_hbm.at[rows(t), :], zero_sem))

    @pl.when(e == 0)
    def _():
        zeros[...] = jnp.zeros_like(zeros)

    for_each_unused(_start)

    @pl.when((e == 0) & (n > 0))
    def _():
        in_copy(0, 0).start()

    @pl.when(n > 0)
    def _():
        wgu_s[:, 0:f] = wg_ref[0].astype(BF16)
        wgu_s[:, f:] = wu_ref[0].astype(BF16)
        wd_s[...] = wd_ref[0].astype(BF16)

    lane = lax.broadcasted_iota(jnp.int32, (1, LANES), 1)

    def ffn(slot, m_rows):
        x = xbuf[slot, 0:m_rows, 0:d]
        rec = xbuf[slot, 0:m_rows, d:].astype(F32)

        def lanes_sum(first, count):
            keep = (lane >= first) & (lane < first + count)
            return jnp.sum(jnp.where(keep, rec, 0.0), axis=-1, keepdims=True)

        first_choice = lanes_sum(EXT_E0, 1) == e.astype(F32)
        w = jnp.where(first_choice, lanes_sum(EXT_W0, N_SPLIT), lanes_sum(EXT_W1, N_SPLIT))
        h = _dot(x, wgu_s[...])
        hg = h[:, 0:f]
        hu = h[:, f:]
        a = hg * jax.nn.sigmoid(hg) * hu * w
        obuf[slot, 0:m_rows, :] = _dot(a.astype(BF16), wd_s[...]).astype(BF16)
        if m_rows < EXPERT_TILE:
            obuf[slot, m_rows:, :] = jnp.zeros((EXPERT_TILE - m_rows, d), BF16)

    def tile(k, carry):
        slot = k % 2

        @pl.when(k + 1 < n)
        def _():
            in_copy(k + 1, 1 - slot).start()

        in_copy(k, slot).wait()

        @pl.when(k >= 2)
        def _():
            out_copy(k - 2, slot).wait()

        parts_used = jnp.minimum(
            (valid_ref[e] - k * EXPERT_TILE + PART_ROWS - 1) // PART_ROWS, EXPERT_TILE // PART_ROWS)
        for parts in range(1, EXPERT_TILE // PART_ROWS + 1):
            @pl.when(parts_used == parts)
            def _():
                ffn(slot, parts * PART_ROWS)

        out_copy(k, slot).start()
        return carry

    lax.fori_loop(0, n, tile, 0)

    for back in (2, 1):
        @pl.when(n >= back)
        def _():
            out_copy(n - back, (n - back) % 2).wait()

    nxt = jnp.minimum(e + 1, N_EXPERTS - 1)

    @pl.when((e + 1 < N_EXPERTS) & (count_ref[nxt] > 0))
    def _():
        in_copy(0, 0, first_ref[nxt]).start()

    for_each_unused(_wait)


def _experts(xs, first_tile, tile_count, valid_rows, n_used, w_gate, w_up, w_down):
    n_rows, width = xs.shape
    n_exp, d, f = w_gate.shape
    te = EXPERT_TILE
    weight = lambda e, *_: (e, 0, 0)
    return pl.pallas_call(
        _expert_kernel,
        grid_spec=pltpu.PrefetchScalarGridSpec(
            num_scalar_prefetch=4,
            grid=(n_exp,),
            in_specs=[
                pl.BlockSpec((1, d, f), weight),
                pl.BlockSpec((1, d, f), weight),
                pl.BlockSpec((1, f, d), weight),
                pl.BlockSpec(memory_space=pl.ANY),
            ],
            out_specs=pl.BlockSpec(memory_space=pl.ANY),
            scratch_shapes=[
                pltpu.VMEM((2, te, width), BF16),
                pltpu.VMEM((2, te, d), BF16),
                pltpu.VMEM((te, d), BF16),
                pltpu.VMEM((d, 2 * f), BF16), pltpu.VMEM((f, d), BF16),
                pltpu.SemaphoreType.DMA((2,)),
                pltpu.SemaphoreType.DMA((2,)),
                pltpu.SemaphoreType.DMA(()),
            ],
        ),
        out_shape=jax.ShapeDtypeStruct((n_rows, d), BF16),
        compiler_params=pltpu.CompilerParams(
            dimension_semantics=("arbitrary",), vmem_limit_bytes=VMEM_LIMIT),
        name="experts",
    )(first_tile, tile_count, valid_rows, n_used, w_gate, w_up, w_down, xs)


def _moe_layout(cnt, n_pairs):
    c = cnt[:, 0, :N_EXPERTS].astype(jnp.int32)
    n_tiles = c.shape[0]
    n8 = (c + SEG_ALIGN - 1) // SEG_ALIGN
    lrow = jnp.cumsum(n8, axis=1) - n8
    units_e = jnp.sum(n8, axis=0)
    per_tile = EXPERT_TILE // SEG_ALIGN
    tiles_e = (units_e + per_tile - 1) // per_tile
    e_end = jnp.cumsum(tiles_e)
    e_off = (e_end - tiles_e) * per_tile
    grow = e_off[None, :] + jnp.cumsum(n8, axis=0) - n8
    tail0 = e_off + units_e
    tailn = tiles_e * per_tile - units_e
    worst = n_pairs + n_tiles * N_EXPERTS * (SEG_ALIGN - 1) + N_EXPERTS * (EXPERT_TILE - SEG_ALIGN)
    n_rows = -(-worst // EXPERT_TILE) * EXPERT_TILE
    seg_tables = (n8.reshape(-1), lrow.reshape(-1), grow.reshape(-1), jnp.sum(n8, axis=1))
    expert_tables = (e_end - tiles_e, tiles_e, units_e * SEG_ALIGN, e_end[-1:])
    return seg_tables, (tail0, tailn), expert_tables, n_rows


def _combine_kernel(n8_ref, lrow_ref, grow_ref, tot_ref,
                    h_ref, slots_ref, p_ref, g_ref, wg_ref, wp_ref, gf_ref, os_hbm,
                    o_ref, buf, sems):
    i = pl.program_id(0)
    nt = pl.num_programs(0)
    slot = i % 2

    def seg_copy(slot_):
        def make(l, g, rows):
            return pltpu.make_async_copy(os_hbm.at[pl.ds(g, rows), :],
                                         buf.at[slot_, pl.ds(l, rows), :], sems.at[slot_])
        return make

    @pl.when(i == 0)
    def _():
        buf[...] = jnp.zeros_like(buf)
        _for_each_piece(n8_ref, lrow_ref, grow_ref, 0, seg_copy(0), _start)

    @pl.when(i + 1 < nt)
    def _():
        _for_each_piece(n8_ref, lrow_ref, grow_ref, i + 1, seg_copy(1 - slot), _start)

    _wait_rows(tot_ref[i], seg_copy(slot))

    def finish(rows):
        slots = slots_ref[...]
        perm = _permutation(slots[:, 0:1], slots[:, 1:2], slots_axis=1, n_slots=rows)
        h = h_ref[...] + _dot(perm, buf[slot, 0:rows, :])
        gate = jax.nn.sigmoid(_dot(_rms_norm(h, g_ref[...]).astype(BF16), wg_ref[...]))
        h = h + gate * _dot(p_ref[...].astype(BF16), wp_ref[...])
        o_ref[...] = _rms_norm(h, gf_ref[...])

    row_options = _used_row_options()
    used = _round_up_to_option(tot_ref[i] * SEG_ALIGN, row_options)
    for rows in row_options:
        @pl.when(used == rows)
        def _():
            finish(rows)


def _combine(h1, slots, seg_tables, o_sorted, p2, g_ple, w_gate, w_proj, g_final):
    n, d = h1.shape
    tm = ROW_TILE
    row = lambda i, *_: (i, 0)
    full2 = lambda i, *_: (0, 0)
    return pl.pallas_call(
        _combine_kernel,
        grid_spec=pltpu.PrefetchScalarGridSpec(
            num_scalar_prefetch=len(seg_tables),
            grid=(n // tm,),
            in_specs=[
                pl.BlockSpec((tm, d), row),
                pl.BlockSpec((tm, LANES), row),
                pl.BlockSpec((tm, p2.shape[1]), row),
                pl.BlockSpec((1, d), full2),
                pl.BlockSpec(w_gate.shape, full2),
                pl.BlockSpec(w_proj.shape, full2),
                pl.BlockSpec((1, d), full2),
                pl.BlockSpec(memory_space=pl.ANY),
            ],
            out_specs=pl.BlockSpec((tm, d), row),
            scratch_shapes=[
                pltpu.VMEM((2, LOCAL_ROWS, d), BF16),
                pltpu.SemaphoreType.DMA((2,)),
            ],
        ),
        out_shape=jax.ShapeDtypeStruct((n, d), F32),
        compiler_params=pltpu.CompilerParams(
            dimension_semantics=("arbitrary",), vmem_limit_bytes=VMEM_LIMIT),
        name="combine",
    )(*seg_tables, h1, slots, p2, g_ple, w_gate, w_proj, g_final, o_sorted)


def _pad_lanes(a):
    return jnp.pad(a, ((0, 0), (0, LANES - a.shape[1])))


def kernel(x, p, g_mix, w_in, b_f, w_pool, s_pool, w_out, g_ffn, w_grp, b_grp, w_rt, b_rt,
           w_e_gate, w_e_up, w_e_down, g_ple, w_ple_gate, w_ple_proj, g_final):
    batch, seq_len, d = x.shape
    n = batch * seq_len
    assert w_in.shape[0] == 1, "single-layer stack only: the final norm is fused into the layer"
    i = 0
    pool_dim = s_pool.shape[1]
    attn_dim = N_HEADS * HEAD_DIM
    main = pool_dim + 3 * attn_dim
    h = x.reshape(n, d)
    w_f = _pad_lanes(jnp.repeat(w_in[i, :, main:], N_SPLIT, axis=1))
    w_main = jnp.concatenate([w_in[i, :, :main], w_f], axis=1).astype(BF16)
    b_f3 = _pad_lanes(jnp.repeat(b_f[i], N_SPLIT)[None])
    pool, q, k, v = _inproj(h, g_mix[i][None], w_main, b_f3,
                            w_pool[i].astype(BF16), s_pool[i][None], seq_len)
    attn = _attention(q, k, v, batch, seq_len)
    w_router = _pad_lanes(jnp.concatenate([w_grp[i], w_rt[i]], axis=1))
    b_router = _pad_lanes(jnp.concatenate([b_grp[i], b_rt[i]])[None])
    h1, m, ext, slots, slots_t, cnt = _mix(h, pool, attn, w_out[i].astype(BF16), g_ffn[i][None],
                                          w_router, b_router)
    seg_tables, tail_tables, expert_tables, n_rows = _moe_layout(cnt, 2 * n)
    x_sorted = _dispatch(m, ext, slots_t, seg_tables + tail_tables + expert_tables[3:], n_rows)
    f = w_e_gate.shape[-1]
    o_sorted = _experts(x_sorted, *expert_tables,
                        w_e_gate[i].reshape(N_EXPERTS, d, f),
                        w_e_up[i].reshape(N_EXPERTS, d, f),
                        w_e_down[i].reshape(N_EXPERTS, f, d))
    out = _combine(h1, slots, seg_tables, o_sorted, p[i].reshape(n, -1), g_ple[i][None],
                   w_ple_gate[i].astype(BF16), w_ple_proj[i].astype(BF16), g_final[None])
    return out.reshape(batch, seq_len, d)
```

```python
import functools
import math

import jax
import jax.numpy as jnp
from jax import lax
from jax.experimental import pallas as pl
from jax.experimental.pallas import tpu as pltpu

HEAD_DIM = 64
N_HEADS = 8
POOL_WINDOWS = (2, 4, 8, 16)
POOL_GROUP_DIM = 128
POOL_HISTORY = 16
N_GROUPS = 4
EXPERTS_PER_GROUP = 8
N_EXPERTS = N_GROUPS * EXPERTS_PER_GROUP
EPS = 1e-6
LANES = 128
ROUTER_LANE0 = N_GROUPS
NEG_INF = float("-inf")
LOG2E = math.log2(math.e)
N_SPLIT = 3

ROW_TILE = 512
ATTN_Q = 512
VMEM_LIMIT = 48 * 1024 * 1024

BF16 = jnp.bfloat16
F32 = jnp.float32


def _dot(a, b):
    return jnp.dot(a, b, preferred_element_type=F32)


def _split2(a):
    hi = a.astype(BF16)
    lo = (a - hi.astype(F32)).astype(BF16)
    return hi, lo


def _split3(a):
    hi = a.astype(BF16)
    r = a - hi.astype(F32)
    mid = r.astype(BF16)
    lo = (r - mid.astype(F32)).astype(BF16)
    return hi, mid, lo


def _dot_precise(a, w):
    a1, a2 = _split2(a)
    w1, w2 = _split2(w)
    n = w.shape[1]
    r = _dot(a1, jnp.concatenate([w1, w2], axis=1))
    return r[:, 0:n] + (r[:, n:] + _dot(a2, w1))


def _rms_norm(x, g):
    return x * lax.rsqrt(jnp.mean(x * x, axis=-1, keepdims=True) + EPS) * g


def _own_half_start(head):
    return 0 if head % 2 == 0 else HEAD_DIM


def _inproj_kernel(tiles_per_seq, x_ref, g_ref, w_ref, bf_ref, wp_ref, sp_ref,
                   pool_ref, q_ref, k_ref, v_ref,
                   carry_c, carry_u, u_s, q_s, k_s, v_s, c_s):
    i = pl.program_id(0)
    tm = x_ref.shape[0]
    pool_dim = pool_ref.shape[1]
    attn_dim = N_HEADS * HEAD_DIM

    @pl.when(i == 0)
    def _():
        for ref in (u_s, q_s, k_s, v_s, c_s, carry_u):
            ref[...] = jnp.zeros_like(ref)

    @pl.when(i % tiles_per_seq == 0)
    def _():
        carry_c[...] = jnp.zeros_like(carry_c)

    prev_seq_tile = (i + tiles_per_seq - 1) % tiles_per_seq

    @pl.when(prev_seq_tile == 0)
    def _():
        carry_u[...] = jnp.zeros_like(carry_u)

    u_prev, c_prev = u_s[...], c_s[...]
    q_prev, k_prev, v_prev = q_s[...], k_s[...], v_s[...]

    a = _rms_norm(x_ref[...], g_ref[...])
    ab = a.astype(BF16)
    o0 = pool_dim
    u = _dot(ab, w_ref[:, 0:o0])
    qf = _dot(ab, w_ref[:, o0:o0 + attn_dim]) * (LOG2E * HEAD_DIM ** -0.5)
    kf = _dot(ab, w_ref[:, o0 + attn_dim:o0 + 2 * attn_dim])
    vf = _dot(ab, w_ref[:, o0 + 2 * attn_dim:o0 + 3 * attn_dim])
    fl = _dot(ab, w_ref[:, o0 + 3 * attn_dim:]) + bf_ref[...]

    c1, c2, c3 = _split3(c_prev * LOG2E)
    lane = lax.broadcasted_iota(jnp.int32, (1, LANES), 1)
    piece_id = lane % N_SPLIT
    pieces = jnp.where(piece_id == 0, c1, jnp.where(piece_id == 1, c2, c3)).astype(F32)
    for h in range(N_HEADS):
        own = (lane >= _own_half_start(h)) & (lane < _own_half_start(h) + HEAD_DIM)
        pair = slice((h // 2) * LANES, (h // 2 + 1) * LANES)
        blk = slice(h * LANES, (h + 1) * LANES)
        spare = HEAD_DIM - _own_half_start(h)
        first = (lane >= spare) & (lane < spare + N_SPLIT)
        second = (lane >= spare + N_SPLIT) & (lane < spare + 2 * N_SPLIT)
        to_first = pltpu.roll(pieces, (spare - h * N_SPLIT) % LANES, axis=1)
        to_second = pltpu.roll(pieces, (spare + N_SPLIT - h * N_SPLIT) % LANES, axis=1)
        bias_q = jnp.where(first, to_first, jnp.where(second, 1.0, 0.0))
        bias_k = jnp.where(second, -to_second, jnp.where(first, 1.0, 0.0))
        q_ref[:, blk] = jnp.where(own, q_prev[:, pair], bias_q).astype(BF16)
        k_ref[:, blk] = jnp.where(own, k_prev[:, pair], bias_k).astype(BF16)
        one_col = (lane == HEAD_DIM - _own_half_start(h)).astype(F32)
        v_ref[:, blk] = jnp.where(own, v_prev[:, pair], one_col).astype(BF16)

    ext = jnp.concatenate([carry_u[...], u_prev], axis=0)
    carry_u[...] = u_prev[tm - POOL_HISTORY:, :]
    pos = (prev_seq_tile * tm + 1 + lax.broadcasted_iota(jnp.int32, (tm, 1), 0)).astype(F32)
    for gi, w in enumerate(POOL_WINDOWS):
        lo, hi = gi * POOL_GROUP_DIM, (gi + 1) * POOL_GROUP_DIM
        s = ext[:, lo:hi]
        shift = 1
        while shift < w:
            s = s + pltpu.roll(s, shift, axis=0)
            shift *= 2
        mean = s[POOL_HISTORY:, :] / jnp.minimum(pos, float(w))
        d = mean - u_prev[:, lo:hi]
        y = _dot(d.astype(BF16), wp_ref[gi]) * sp_ref[:, lo:hi]
        pool_ref[:, lo:hi] = y.astype(BF16)

    lf = jnp.minimum(fl, 0.0) - jnp.log1p(jnp.exp(-jnp.abs(fl)))
    row = lax.broadcasted_iota(jnp.int32, (tm, tm), 0)
    col = lax.broadcasted_iota(jnp.int32, (tm, tm), 1)
    tril = (col <= row).astype(BF16)
    sums = _dot(tril, jnp.concatenate(_split3(lf), axis=1))
    c = carry_c[...] + (sums[:, 0:LANES] + (sums[:, LANES:2 * LANES] + sums[:, 2 * LANES:]))
    carry_c[...] = c[tm - 1:tm, :]
    u_s[...] = u
    q_s[...] = qf
    k_s[...] = kf
    v_s[...] = vf
    c_s[...] = c


def _inproj(x2, g_mix, w_main, b_f, w_pool, s_pool, seq_len):
    n, d = x2.shape
    tm = ROW_TILE
    pool_dim = w_pool.shape[0] * w_pool.shape[1]
    head_w = N_HEADS * LANES
    attn_dim = N_HEADS * HEAD_DIM
    n_tiles = n // tm
    projected = lambda i: (jnp.minimum(i, n_tiles - 1), 0)
    finished = lambda i: (jnp.maximum(i - 1, 0), 0)
    full2 = lambda i: (0, 0)
    return pl.pallas_call(
        functools.partial(_inproj_kernel, seq_len // tm),
        grid=(n_tiles + 1,),
        in_specs=[
            pl.BlockSpec((tm, d), projected),
            pl.BlockSpec((1, d), full2),
            pl.BlockSpec(w_main.shape, full2),
            pl.BlockSpec((1, LANES), full2),
            pl.BlockSpec(w_pool.shape, lambda i: (0, 0, 0)),
            pl.BlockSpec((1, pool_dim), full2),
        ],
        out_specs=[
            pl.BlockSpec((tm, pool_dim), finished),
            pl.BlockSpec((tm, head_w), finished),
            pl.BlockSpec((tm, head_w), finished),
            pl.BlockSpec((tm, head_w), finished),
        ],
        out_shape=[
            jax.ShapeDtypeStruct((n, pool_dim), BF16),
            jax.ShapeDtypeStruct((n, head_w), BF16),
            jax.ShapeDtypeStruct((n, head_w), BF16),
            jax.ShapeDtypeStruct((n, head_w), BF16),
        ],
        scratch_shapes=[
            pltpu.VMEM((1, LANES), F32), pltpu.VMEM((POOL_HISTORY, pool_dim), F32),
            pltpu.VMEM((tm, pool_dim), F32), pltpu.VMEM((tm, attn_dim), F32),
            pltpu.VMEM((tm, attn_dim), F32), pltpu.VMEM((tm, attn_dim), F32),
            pltpu.VMEM((tm, LANES), F32),
        ],
        compiler_params=pltpu.CompilerParams(
            dimension_semantics=("arbitrary",), vmem_limit_bytes=VMEM_LIMIT),
        name="inproj",
    )(x2, g_mix, w_main, b_f, w_pool, s_pool)


def _attend_pair(q_ref, k_ref, v_ref, n_keys):
    tq = q_ref.shape[0]
    past = n_keys - tq
    nt = (((1,), (1,)), ((), ()))
    blks = [slice(hh * LANES, (hh + 1) * LANES) for hh in range(2)]
    row = lax.broadcasted_iota(jnp.int32, (tq, tq), 0)
    col = lax.broadcasted_iota(jnp.int32, (tq, tq), 1)
    qs = [q_ref[:, blk] for blk in blks]
    s_diag = [jnp.where(col <= row,
                        lax.dot_general(q, k_ref[past:n_keys, blk], nt,
                                        preferred_element_type=F32), NEG_INF)
              for q, blk in zip(qs, blks)]
    m = [jnp.max(s, axis=-1, keepdims=True) for s in s_diag]
    if past:
        s_past = [lax.dot_general(q, k_ref[0:past, blk], nt, preferred_element_type=F32)
                  for q, blk in zip(qs, blks)]
        m = [jnp.maximum(mi, jnp.max(s, axis=-1, keepdims=True)) for mi, s in zip(m, s_past)]
    acc = [_dot(jnp.exp2(s - mi).astype(BF16), v_ref[past:n_keys, blk])
           for s, mi, blk in zip(s_diag, m, blks)]
    if past:
        acc = [a + _dot(jnp.exp2(s - mi).astype(BF16), v_ref[0:past, blk])
               for a, s, mi, blk in zip(acc, s_past, m, blks)]
    return acc


def _attn_kernel(q_ref, k_ref, v_ref, o_ref):
    qi = pl.program_id(2)
    tq = q_ref.shape[0]
    lane = lax.broadcasted_iota(jnp.int32, (1, LANES), 1)
    for tile in range(k_ref.shape[0] // tq):
        @pl.when(qi == tile)
        def _():
            outs = []
            for hh, acc in enumerate(_attend_pair(q_ref, k_ref, v_ref, (tile + 1) * tq)):
                sum_lane = HEAD_DIM - _own_half_start(hh)
                l = jnp.sum(jnp.where(lane == sum_lane, acc, 0.0), axis=-1, keepdims=True)
                outs.append(acc / l)
            o_ref[...] = jnp.where(lane < HEAD_DIM, outs[0], outs[1]).astype(BF16)


def _attention(q, k, v, batch, seq_len):
    n = q.shape[0]
    pairs = N_HEADS // 2
    tq = ATTN_Q
    qt = seq_len // tq
    pair_block = lambda b, hp, qi: (b, hp)
    return pl.pallas_call(
        _attn_kernel,
        grid=(batch, pairs, qt),
        in_specs=[
            pl.BlockSpec((tq, 2 * LANES), lambda b, hp, qi: (b * qt + qi, hp)),
            pl.BlockSpec((seq_len, 2 * LANES), pair_block),
            pl.BlockSpec((seq_len, 2 * LANES), pair_block),
        ],
        out_specs=pl.BlockSpec((tq, LANES), lambda b, hp, qi: (b * qt + qi, hp)),
        out_shape=jax.ShapeDtypeStruct((n, pairs * LANES), BF16),
        compiler_params=pltpu.CompilerParams(
            dimension_semantics=("arbitrary", "arbitrary", "arbitrary"),
            vmem_limit_bytes=VMEM_LIMIT),
        name="attn",
    )(q, k, v)


def _route(logits):
    lane = lax.broadcasted_iota(jnp.int32, logits.shape, 1)
    big = jnp.int32(LANES)
    gl = jnp.where(lane < N_GROUPS, logits, NEG_INF)
    gmax = jnp.max(gl, axis=-1, keepdims=True)
    g_w = 1.0 / jnp.sum(jnp.exp(gl - gmax), axis=-1, keepdims=True)
    g_idx = jnp.min(jnp.where(gl == gmax, lane, big), axis=-1, keepdims=True)
    e_lo = ROUTER_LANE0 + EXPERTS_PER_GROUP * g_idx
    el = jnp.where((lane >= e_lo) & (lane < e_lo + EXPERTS_PER_GROUP), logits, NEG_INF)
    v1 = jnp.max(el, axis=-1, keepdims=True)
    i1 = jnp.min(jnp.where(el == v1, lane, big), axis=-1, keepdims=True)
    el2 = jnp.where(lane == i1, NEG_INF, el)
    v2 = jnp.max(el2, axis=-1, keepdims=True)
    i2 = jnp.min(jnp.where(el2 == v2, lane, big), axis=-1, keepdims=True)
    e2 = jnp.exp(v2 - v1)
    w1 = g_w / (1.0 + e2)
    w2 = g_w * e2 / (1.0 + e2)
    return (i1 - ROUTER_LANE0).astype(F32), (i2 - ROUTER_LANE0).astype(F32), w1, w2


EXT_W0 = 0
EXT_W1 = N_SPLIT
EXT_E0 = 2 * N_SPLIT


def _routing_record(e0, e1, w0, w1):
    lane = lax.broadcasted_iota(jnp.int32, (1, LANES), 1)
    rec = jnp.where(lane == EXT_E0, e0, jnp.where(lane == EXT_E0 + 1, e1, 0.0))
    for first, w in ((EXT_W0, w0), (EXT_W1, w1)):
        for k, piece in enumerate(_split3(w)):
            rec = jnp.where(lane == first + k, piece.astype(F32), rec)
    return rec.astype(BF16)


def _local_slots(e0, e1):
    t = e0.shape[0]
    lane = lax.broadcasted_iota(jnp.int32, (t, LANES), 1).astype(F32)
    oh0 = lane == e0
    oh1 = lane == e1
    picked = (oh0 | oh1).astype(BF16)
    cnt = jnp.sum(picked.astype(F32), axis=0, keepdims=True)
    units = jnp.floor((cnt + (SEG_ALIGN - 1)) * (1.0 / SEG_ALIGN))
    r128 = lax.broadcasted_iota(jnp.int32, (LANES, LANES), 0)
    c128 = lax.broadcasted_iota(jnp.int32, (LANES, LANES), 1)
    before = (r128 < c128).astype(BF16)
    lstart = SEG_ALIGN * _dot(jnp.broadcast_to(units, (8, LANES)).astype(BF16), before)[0:1, :]
    row = lax.broadcasted_iota(jnp.int32, (t, t), 0)
    col = lax.broadcasted_iota(jnp.int32, (t, t), 1)
    earlier = (col < row).astype(BF16)
    base = _dot(earlier, picked) + lstart
    slot0 = jnp.sum(jnp.where(oh0, base, 0.0), axis=-1, keepdims=True)
    slot1 = jnp.sum(jnp.where(oh1, base, 0.0), axis=-1, keepdims=True)
    return slot0, slot1, cnt


def _mix_kernel(x_ref, pool_ref, attn_ref, wo_ref, g_ref, wr_ref, br_ref,
                h_ref, m_ref, ext_ref, slots_ref, slots_t_ref, cnt_ref, logits_s):
    pool_dim = pool_ref.shape[1]

    @pl.when(pl.program_id(0) == 0)
    def _():
        logits_s[...] = jnp.zeros_like(logits_s)

    prev_logits = logits_s[...]
    h = x_ref[...] + (_dot(pool_ref[...], wo_ref[0:pool_dim, :])
                      + _dot(attn_ref[...], wo_ref[pool_dim:, :]))
    h_ref[...] = h
    e0, e1, w0, w1 = _route(prev_logits)
    ext_ref[...] = _routing_record(e0, e1, w0, w1)
    slot0, slot1, cnt = _local_slots(e0, e1)
    cnt_ref[0] = cnt
    lane = lax.broadcasted_iota(jnp.int32, (1, LANES), 1)
    slots = jnp.where(lane == 0, slot0, jnp.where(lane == 1, slot1, 0.0))
    slots_ref[...] = slots
    slots_t_ref[0] = slots.T[0:8, :]
    m = _rms_norm(h, g_ref[...])
    m_ref[...] = m.astype(BF16)
    logits_s[...] = _dot_precise(m, wr_ref[...]) + br_ref[...]


def _mix(x2, pool, attn, w_out, g_ffn, w_router, b_router):
    n, d = x2.shape
    tm = ROW_TILE
    n_tiles = n // tm
    row = lambda i: (jnp.minimum(i, n_tiles - 1), 0)
    routed = lambda i: (jnp.maximum(i - 1, 0), 0)
    full2 = lambda i: (0, 0)
    return pl.pallas_call(
        _mix_kernel,
        grid=(n_tiles + 1,),
        in_specs=[
            pl.BlockSpec((tm, d), row),
            pl.BlockSpec((tm, pool.shape[1]), row),
            pl.BlockSpec((tm, attn.shape[1]), row),
            pl.BlockSpec(w_out.shape, full2),
            pl.BlockSpec((1, d), full2),
            pl.BlockSpec(w_router.shape, full2),
            pl.BlockSpec((1, LANES), full2),
        ],
        out_specs=[
            pl.BlockSpec((tm, d), row),
            pl.BlockSpec((tm, d), row),
            pl.BlockSpec((tm, LANES), routed),
            pl.BlockSpec((tm, LANES), routed),
            pl.BlockSpec((1, 8, tm), lambda i: (jnp.maximum(i - 1, 0), 0, 0)),
            pl.BlockSpec((1, 1, LANES), lambda i: (jnp.maximum(i - 1, 0), 0, 0)),
        ],
        out_shape=[
            jax.ShapeDtypeStruct((n, d), F32),
            jax.ShapeDtypeStruct((n, d), BF16),
            jax.ShapeDtypeStruct((n, LANES), BF16),
            jax.ShapeDtypeStruct((n, LANES), F32),
            jax.ShapeDtypeStruct((n_tiles, 8, tm), F32),
            jax.ShapeDtypeStruct((n_tiles, 1, LANES), F32),
        ],
        scratch_shapes=[pltpu.VMEM((tm, LANES), F32)],
        compiler_params=pltpu.CompilerParams(
            dimension_semantics=("arbitrary",), vmem_limit_bytes=VMEM_LIMIT),
        name="mix",
    )(x2, pool, attn, w_out, g_ffn, w_router, b_router)


SEG_ALIGN = 16
COPY_UNITS = 4
COPY_ROWS = COPY_UNITS * SEG_ALIGN
EXPERT_TILE = 512
LOCAL_ROWS = 2 * ROW_TILE + N_EXPERTS * SEG_ALIGN


TRIM_ROWS = 128
TRIM_OPTIONS = 3


def _used_row_options():
    return [LOCAL_ROWS - k * TRIM_ROWS for k in reversed(range(TRIM_OPTIONS))]


def _round_up_to_option(rows, options):
    out = jnp.int32(options[-1])
    for opt in reversed(options[:-1]):
        out = jnp.where(rows <= opt, jnp.int32(opt), out)
    return out


def _permutation(slot0, slot1, slots_axis, n_slots=None):
    n_slots = LOCAL_ROWS if n_slots is None else n_slots
    shape = (1, n_slots) if slots_axis == 1 else (n_slots, 1)
    s = lax.broadcasted_iota(jnp.int32, shape, slots_axis)
    return ((s == slot0.astype(jnp.int32)) | (s == slot1.astype(jnp.int32))).astype(BF16)


def _copy_classes():
    classes = [(COPY_UNITS, LOCAL_ROWS // COPY_ROWS)]
    units = COPY_UNITS // 2
    while units >= 1:
        classes.append((units, N_EXPERTS))
        units //= 2
    return classes


def _for_each_piece(list_refs, tile, make_copy, act):
    for (units, k_max), (local_ref, hbm_ref, count_ref) in zip(_copy_classes(), list_refs):
        def body(k, carry, units=units, k_max=k_max, local_ref=local_ref, hbm_ref=hbm_ref):
            idx = tile * k_max + k
            act(make_copy(pl.multiple_of(local_ref[idx] * SEG_ALIGN, SEG_ALIGN),
                          pl.multiple_of(hbm_ref[idx] * SEG_ALIGN, SEG_ALIGN),
                          units * SEG_ALIGN))
            return carry

        lax.fori_loop(0, count_ref[tile], body, 0)


def _group_lists(refs):
    n = len(_copy_classes())
    return [tuple(refs[3 * c:3 * c + 3]) for c in range(n)], refs[3 * n], refs[3 * n + 1:]


def _wait_rows(total_units, make_copy):
    for b in range((LOCAL_ROWS // SEG_ALIGN).bit_length()):
        @pl.when(((total_units >> b) & 1) == 1)
        def _():
            make_copy(0, 0, SEG_ALIGN << b).wait()


def _start(copy):
    copy.start()


def _wait(copy):
    copy.wait()


def _dispatch_kernel(n_steps, *refs):
    copy_lists, tot_ref, rest = _group_lists(refs)
    (tail0_ref, tailn_ref, nu_ref, m_ref, ext_ref, st_ref, st_next_ref, xs_hbm,
     buf, zeros, perm_even, perm_odd, sems, tail_sem) = rest
    i = pl.program_id(0)
    slot = i % 2
    d = m_ref.shape[1]
    n_expert_tiles = xs_hbm.shape[0] // EXPERT_TILE

    def seg_copy(slot_):
        def make(l, g, rows):
            return pltpu.make_async_copy(buf.at[slot_, pl.ds(l, rows), :],
                                         xs_hbm.at[pl.ds(g, rows), :], sems.at[slot_])
        return make

    def for_each_tail(act):
        @pl.when(i < N_EXPERTS)
        def _():
            e = jnp.minimum(i, N_EXPERTS - 1)
            g0 = tail0_ref[e] * SEG_ALIGN

            def body(c, carry):
                g = pl.multiple_of(g0 + c * SEG_ALIGN, SEG_ALIGN)
                act(pltpu.make_async_copy(zeros.at[pl.ds(0, SEG_ALIGN), :],
                                          xs_hbm.at[pl.ds(g, SEG_ALIGN), :], tail_sem))
                return carry

            lax.fori_loop(0, tailn_ref[e], body, 0)

        for k in range(-(-n_expert_tiles // n_steps)):
            t = nu_ref[0] + i + k * n_steps

            @pl.when(t < n_expert_tiles)
            def _():
                g = pl.multiple_of(t * EXPERT_TILE, EXPERT_TILE)
                act(pltpu.make_async_copy(zeros, xs_hbm.at[pl.ds(g, EXPERT_TILE), :], tail_sem))

    @pl.when(i == 0)
    def _():
        zeros[...] = jnp.zeros_like(zeros)

    for_each_tail(_start)

    def build(slots_t, perm):
        perm[...] = _permutation(slots_t[0, 0:1, :], slots_t[0, 1:2, :], slots_axis=0)

    @pl.when(i == 0)
    def _():
        build(st_ref, perm_even)

    def permute(perm, next_perm, rows):
        p = perm[0:rows, :]
        build(st_next_ref, next_perm)
        buf[slot, 0:rows, 0:d] = _dot(p, m_ref[...]).astype(BF16)
        buf[slot, 0:rows, d:] = _dot(p, ext_ref[...]).astype(BF16)

    row_options = _used_row_options()
    used = _round_up_to_option(tot_ref[i] * SEG_ALIGN, row_options)
    for parity, (perm, next_perm) in enumerate(((perm_even, perm_odd), (perm_odd, perm_even))):
        for rows in row_options:
            @pl.when((slot == parity) & (used == rows))
            def _():
                permute(perm, next_perm, rows)

    _for_each_piece(copy_lists, i, seg_copy(slot), _start)

    @pl.when(i > 0)
    def _():
        _wait_rows(tot_ref[i - 1], seg_copy(1 - slot))

    for_each_tail(_wait)

    @pl.when(i == n_steps - 1)
    def _():
        _wait_rows(tot_ref[i], seg_copy(slot))


def _dispatch(m, ext, slots_t, tables, n_rows):
    n, d = m.shape
    tm = ROW_TILE
    assert n // tm >= N_EXPERTS, "each grid step zero-fills the tail of one expert"
    width = d + LANES
    last = n // tm - 1
    row = lambda i, *_: (i, 0)
    return pl.pallas_call(
        functools.partial(_dispatch_kernel, n // tm),
        grid_spec=pltpu.PrefetchScalarGridSpec(
            num_scalar_prefetch=len(tables),
            grid=(n // tm,),
            in_specs=[
                pl.BlockSpec((tm, d), row),
                pl.BlockSpec((tm, LANES), row),
                pl.BlockSpec((1, 8, tm), lambda i, *_: (i, 0, 0)),
                pl.BlockSpec((1, 8, tm), lambda i, *_: (jnp.minimum(i + 1, last), 0, 0)),
            ],
            out_specs=pl.BlockSpec(memory_space=pl.ANY),
            scratch_shapes=[
                pltpu.VMEM((2, LOCAL_ROWS, width), BF16),
                pltpu.VMEM((EXPERT_TILE, width), BF16),
                pltpu.VMEM((LOCAL_ROWS, tm), BF16),
                pltpu.VMEM((LOCAL_ROWS, tm), BF16),
                pltpu.SemaphoreType.DMA((2,)),
                pltpu.SemaphoreType.DMA(()),
            ],
        ),
        out_shape=jax.ShapeDtypeStruct((n_rows, width), BF16),
        compiler_params=pltpu.CompilerParams(
            dimension_semantics=("arbitrary",), vmem_limit_bytes=VMEM_LIMIT),
        name="dispatch",
    )(*tables, m, ext, slots_t, slots_t)


PART_ROWS = 128


def _expert_kernel(first_ref, count_ref, valid_ref, nu_ref, wg_ref, wu_ref, wd_ref, xs_hbm, os_hbm,
                   xbuf, obuf, zeros, wgu_s, wd_s, in_sems, out_sems, zero_sem):
    e = pl.program_id(0)
    d = os_hbm.shape[1]
    f = wg_ref.shape[2]
    n = count_ref[e]
    t0 = first_ref[e]
    n_tiles = os_hbm.shape[0] // EXPERT_TILE

    def rows(t):
        return pl.ds(pl.multiple_of(t * EXPERT_TILE, EXPERT_TILE), EXPERT_TILE)

    def in_copy(k, slot, first=t0):
        return pltpu.make_async_copy(xs_hbm.at[rows(first + k), :], xbuf.at[slot],
                                     in_sems.at[slot])

    def out_copy(k, slot):
        return pltpu.make_async_copy(obuf.at[slot], os_hbm.at[rows(t0 + k), :], out_sems.at[slot])

    def for_each_unused(act):
        for j in range(-(-n_tiles // N_EXPERTS)):
            t = nu_ref[0] + e + j * N_EXPERTS

            @pl.when(t < n_tiles)
            def _():
                act(pltpu.make_async_copy(zeros, os_hbm.at[rows(t), :], zero_sem))

    @pl.when(e == 0)
    def _():
        zeros[...] = jnp.zeros_like(zeros)

    for_each_unused(_start)

    @pl.when((e == 0) & (n > 0))
    def _():
        in_copy(0, 0).start()

    @pl.when(n > 0)
    def _():
        wgu_s[:, 0:f] = wg_ref[0].astype(BF16)
        wgu_s[:, f:] = wu_ref[0].astype(BF16)
        wd_s[...] = wd_ref[0].astype(BF16)

    lane = lax.broadcasted_iota(jnp.int32, (1, LANES), 1)

    def ffn(slot, m_rows):
        x = xbuf[slot, 0:m_rows, 0:d]
        rec = xbuf[slot, 0:m_rows, d:].astype(F32)

        def lanes_sum(first, count):
            keep = (lane >= first) & (lane < first + count)
            return jnp.sum(jnp.where(keep, rec, 0.0), axis=-1, keepdims=True)

        first_choice = lanes_sum(EXT_E0, 1) == e.astype(F32)
        w = jnp.where(first_choice, lanes_sum(EXT_W0, N_SPLIT), lanes_sum(EXT_W1, N_SPLIT))
        h = _dot(x, wgu_s[...])
        hg = h[:, 0:f]
        hu = h[:, f:]
        a = hg * jax.nn.sigmoid(hg) * hu * w
        obuf[slot, 0:m_rows, :] = _dot(a.astype(BF16), wd_s[...]).astype(BF16)
        if m_rows < EXPERT_TILE:
            obuf[slot, m_rows:, :] = jnp.zeros((EXPERT_TILE - m_rows, d), BF16)

    def tile(k, carry):
        slot = k % 2

        @pl.when(k + 1 < n)
        def _():
            in_copy(k + 1, 1 - slot).start()

        in_copy(k, slot).wait()

        @pl.when(k >= 2)
        def _():
            out_copy(k - 2, slot).wait()

        parts_used = jnp.minimum(
            (valid_ref[e] - k * EXPERT_TILE + PART_ROWS - 1) // PART_ROWS, EXPERT_TILE // PART_ROWS)
        for parts in range(1, EXPERT_TILE // PART_ROWS + 1):
            @pl.when(parts_used == parts)
            def _():
                ffn(slot, parts * PART_ROWS)

        out_copy(k, slot).start()
        return carry

    lax.fori_loop(0, n, tile, 0)

    for back in (2, 1):
        @pl.when(n >= back)
        def _():
            out_copy(n - back, (n - back) % 2).wait()

    nxt = jnp.minimum(e + 1, N_EXPERTS - 1)

    @pl.when((e + 1 < N_EXPERTS) & (count_ref[nxt] > 0))
    def _():
        in_copy(0, 0, first_ref[nxt]).start()

    for_each_unused(_wait)


def _experts(xs, first_tile, tile_count, valid_rows, n_used, w_gate, w_up, w_down):
    n_rows, width = xs.shape
    n_exp, d, f = w_gate.shape
    te = EXPERT_TILE
    weight = lambda e, *_: (e, 0, 0)
    return pl.pallas_call(
        _expert_kernel,
        grid_spec=pltpu.PrefetchScalarGridSpec(
            num_scalar_prefetch=4,
            grid=(n_exp,),
            in_specs=[
                pl.BlockSpec((1, d, f), weight),
                pl.BlockSpec((1, d, f), weight),
                pl.BlockSpec((1, f, d), weight),
                pl.BlockSpec(memory_space=pl.ANY),
            ],
            out_specs=pl.BlockSpec(memory_space=pl.ANY),
            scratch_shapes=[
                pltpu.VMEM((2, te, width), BF16),
                pltpu.VMEM((2, te, d), BF16),
                pltpu.VMEM((te, d), BF16),
                pltpu.VMEM((d, 2 * f), BF16), pltpu.VMEM((f, d), BF16),
                pltpu.SemaphoreType.DMA((2,)),
                pltpu.SemaphoreType.DMA((2,)),
                pltpu.SemaphoreType.DMA(()),
            ],
        ),
        out_shape=jax.ShapeDtypeStruct((n_rows, d), BF16),
        compiler_params=pltpu.CompilerParams(
            dimension_semantics=("arbitrary",), vmem_limit_bytes=VMEM_LIMIT),
        name="experts",
    )(first_tile, tile_count, valid_rows, n_used, w_gate, w_up, w_down, xs)


def _compact(valid, local, hbm, k_max):
    pos = jnp.cumsum(valid, axis=1) - valid
    k = jnp.arange(k_max, dtype=jnp.int32)[None, :, None]
    hit = (pos[:, None, :] == k) & (valid[:, None, :] == 1)
    pick = lambda a: jnp.sum(jnp.where(hit, a[:, None, :], 0), axis=2).reshape(-1)
    return pick(local), pick(hbm), jnp.sum(valid, axis=1)


def _copy_lists(n_units, lrow, grow):
    tables = ()
    for units, k_max in _copy_classes():
        if units == COPY_UNITS:
            per_seg = ROW_TILE // COPY_ROWS
            c = jnp.arange(per_seg, dtype=jnp.int32)[None, None, :]
            valid = (c < (n_units // COPY_UNITS)[:, :, None]).astype(jnp.int32)
            off = jnp.broadcast_to(c * COPY_UNITS, valid.shape)
        else:
            valid = ((n_units // units) % 2)[:, :, None]
            off = ((n_units // (2 * units)) * (2 * units))[:, :, None]
        flat = lambda a: a.reshape(a.shape[0], -1)
        tables += _compact(flat(valid), flat(lrow[:, :, None] + off), flat(grow[:, :, None] + off),
                           k_max)
    return tables


def _moe_layout(cnt, n_pairs):
    c = cnt[:, 0, :N_EXPERTS].astype(jnp.int32)
    n_tiles = c.shape[0]
    n8 = (c + SEG_ALIGN - 1) // SEG_ALIGN
    lrow = jnp.cumsum(n8, axis=1) - n8
    units_e = jnp.sum(n8, axis=0)
    per_tile = EXPERT_TILE // SEG_ALIGN
    tiles_e = (units_e + per_tile - 1) // per_tile
    e_end = jnp.cumsum(tiles_e)
    e_off = (e_end - tiles_e) * per_tile
    grow = e_off[None, :] + jnp.cumsum(n8, axis=0) - n8
    tail0 = e_off + units_e
    tailn = tiles_e * per_tile - units_e
    worst = n_pairs + n_tiles * N_EXPERTS * (SEG_ALIGN - 1) + N_EXPERTS * (EXPERT_TILE - SEG_ALIGN)
    n_rows = -(-worst // EXPERT_TILE) * EXPERT_TILE
    seg_tables = _copy_lists(n8, lrow, grow) + (jnp.sum(n8, axis=1),)
    expert_tables = (e_end - tiles_e, tiles_e, units_e * SEG_ALIGN, e_end[-1:])
    return seg_tables, (tail0, tailn), expert_tables, n_rows


def _combine_kernel(*refs):
    copy_lists, tot_ref, rest = _group_lists(refs)
    h_ref, slots_ref, p_ref, g_ref, wg_ref, wp_ref, gf_ref, os_hbm, o_ref, buf, sems = rest
    i = pl.program_id(0)
    nt = pl.num_programs(0)
    slot = i % 2

    def seg_copy(slot_):
        def make(l, g, rows):
            return pltpu.make_async_copy(os_hbm.at[pl.ds(g, rows), :],
                                         buf.at[slot_, pl.ds(l, rows), :], sems.at[slot_])
        return make

    @pl.when(i == 0)
    def _():
        buf[...] = jnp.zeros_like(buf)
        _for_each_piece(copy_lists, 0, seg_copy(0), _start)

    @pl.when(i + 1 < nt)
    def _():
        _for_each_piece(copy_lists, i + 1, seg_copy(1 - slot), _start)

    _wait_rows(tot_ref[i], seg_copy(slot))

    def finish(rows):
        slots = slots_ref[...]
        perm = _permutation(slots[:, 0:1], slots[:, 1:2], slots_axis=1, n_slots=rows)
        h = h_ref[...] + _dot(perm, buf[slot, 0:rows, :])
        gate = jax.nn.sigmoid(_dot(_rms_norm(h, g_ref[...]).astype(BF16), wg_ref[...]))
        h = h + gate * _dot(p_ref[...].astype(BF16), wp_ref[...])
        o_ref[...] = _rms_norm(h, gf_ref[...])

    row_options = _used_row_options()
    used = _round_up_to_option(tot_ref[i] * SEG_ALIGN, row_options)
    for rows in row_options:
        @pl.when(used == rows)
        def _():
            finish(rows)


def _combine(h1, slots, seg_tables, o_sorted, p2, g_ple, w_gate, w_proj, g_final):
    n, d = h1.shape
    tm = ROW_TILE
    row = lambda i, *_: (i, 0)
    full2 = lambda i, *_: (0, 0)
    return pl.pallas_call(
        _combine_kernel,
        grid_spec=pltpu.PrefetchScalarGridSpec(
            num_scalar_prefetch=len(seg_tables),
            grid=(n // tm,),
            in_specs=[
                pl.BlockSpec((tm, d), row),
                pl.BlockSpec((tm, LANES), row),
                pl.BlockSpec((tm, p2.shape[1]), row),
                pl.BlockSpec((1, d), full2),
                pl.BlockSpec(w_gate.shape, full2),
                pl.BlockSpec(w_proj.shape, full2),
                pl.BlockSpec((1, d), full2),
                pl.BlockSpec(memory_space=pl.ANY),
            ],
            out_specs=pl.BlockSpec((tm, d), row),
            scratch_shapes=[
                pltpu.VMEM((2, LOCAL_ROWS, d), BF16),
                pltpu.SemaphoreType.DMA((2,)),
            ],
        ),
        out_shape=jax.ShapeDtypeStruct((n, d), F32),
        compiler_params=pltpu.CompilerParams(
            dimension_semantics=("arbitrary",), vmem_limit_bytes=VMEM_LIMIT),
        name="combine",
    )(*seg_tables, h1, slots, p2, g_ple, w_gate, w_proj, g_final, o_sorted)


def _pad_lanes(a):
    return jnp.pad(a, ((0, 0), (0, LANES - a.shape[1])))


def kernel(x, p, g_mix, w_in, b_f, w_pool, s_pool, w_out, g_ffn, w_grp, b_grp, w_rt, b_rt,
           w_e_gate, w_e_up, w_e_down, g_ple, w_ple_gate, w_ple_proj, g_final):
    batch, seq_len, d = x.shape
    n = batch * seq_len
    assert w_in.shape[0] == 1, "single-layer stack only: the final norm is fused into the layer"
    i = 0
    pool_dim = s_pool.shape[1]
    attn_dim = N_HEADS * HEAD_DIM
    main = pool_dim + 3 * attn_dim
    h = x.reshape(n, d)
    w_f = _pad_lanes(jnp.repeat(w_in[i, :, main:], N_SPLIT, axis=1))
    w_main = jnp.concatenate([w_in[i, :, :main], w_f], axis=1).astype(BF16)
    b_f3 = _pad_lanes(jnp.repeat(b_f[i], N_SPLIT)[None])
    pool, q, k, v = _inproj(h, g_mix[i][None], w_main, b_f3,
                            w_pool[i].astype(BF16), s_pool[i][None], seq_len)
    attn = _attention(q, k, v, batch, seq_len)
    w_router = _pad_lanes(jnp.concatenate([w_grp[i], w_rt[i]], axis=1))
    b_router = _pad_lanes(jnp.concatenate([b_grp[i], b_rt[i]])[None])
    h1, m, ext, slots, slots_t, cnt = _mix(h, pool, attn, w_out[i].astype(BF16), g_ffn[i][None],
                                          w_router, b_router)
    seg_tables, tail_tables, expert_tables, n_rows = _moe_layout(cnt, 2 * n)
    x_sorted = _dispatch(m, ext, slots_t, seg_tables + tail_tables + expert_tables[3:], n_rows)
    f = w_e_gate.shape[-1]
    o_sorted = _experts(x_sorted, *expert_tables,
                        w_e_gate[i].reshape(N_EXPERTS, d, f),
                        w_e_up[i].reshape(N_EXPERTS, d, f),
                        w_e_down[i].reshape(N_EXPERTS, f, d))
    out = _combine(h1, slots, seg_tables, o_sorted, p[i].reshape(n, -1), g_ple[i][None],
                   w_ple_gate[i].astype(BF16), w_ple_proj[i].astype(BF16), g_final[None])
    return out.reshape(batch, seq_len, d)
```

```python
import functools
import math

import jax
import jax.numpy as jnp
from jax import lax
from jax.experimental import pallas as pl
from jax.experimental.pallas import tpu as pltpu

HEAD_DIM = 64
N_HEADS = 8
POOL_WINDOWS = (2, 4, 8, 16)
POOL_GROUP_DIM = 128
POOL_HISTORY = 16
N_GROUPS = 4
EXPERTS_PER_GROUP = 8
N_EXPERTS = N_GROUPS * EXPERTS_PER_GROUP
EPS = 1e-6
LANES = 128
ROUTER_LANE0 = N_GROUPS
NEG_INF = float("-inf")
LOG2E = math.log2(math.e)
N_SPLIT = 3

ROW_TILE = 512
ATTN_Q = 512
VMEM_LIMIT = 48 * 1024 * 1024

BF16 = jnp.bfloat16
F32 = jnp.float32


def _dot(a, b):
    return jnp.dot(a, b, preferred_element_type=F32)


def _split2(a):
    hi = a.astype(BF16)
    lo = (a - hi.astype(F32)).astype(BF16)
    return hi, lo


def _split3(a):
    hi = a.astype(BF16)
    r = a - hi.astype(F32)
    mid = r.astype(BF16)
    lo = (r - mid.astype(F32)).astype(BF16)
    return hi, mid, lo


def _dot_precise(a, w):
    a1, a2 = _split2(a)
    w1, w2 = _split2(w)
    n = w.shape[1]
    r = _dot(a1, jnp.concatenate([w1, w2], axis=1))
    return r[:, 0:n] + (r[:, n:] + _dot(a2, w1))


def _rms_norm(x, g):
    return x * lax.rsqrt(jnp.mean(x * x, axis=-1, keepdims=True) + EPS) * g


def _own_half_start(head):
    return 0 if head % 2 == 0 else HEAD_DIM


def _inproj_kernel(tiles_per_seq, x_ref, g_ref, w_ref, bf_ref, wp_ref, sp_ref,
                   pool_ref, q_ref, k_ref, v_ref,
                   carry_c, carry_u, u_s, q_s, k_s, v_s, c_s):
    i = pl.program_id(0)
    tm = x_ref.shape[0]
    pool_dim = pool_ref.shape[1]
    attn_dim = N_HEADS * HEAD_DIM

    @pl.when(i == 0)
    def _():
        for ref in (u_s, q_s, k_s, v_s, c_s, carry_u):
            ref[...] = jnp.zeros_like(ref)

    @pl.when(i % tiles_per_seq == 0)
    def _():
        carry_c[...] = jnp.zeros_like(carry_c)

    prev_seq_tile = (i + tiles_per_seq - 1) % tiles_per_seq

    @pl.when(prev_seq_tile == 0)
    def _():
        carry_u[...] = jnp.zeros_like(carry_u)

    u_prev, c_prev = u_s[...], c_s[...]
    q_prev, k_prev, v_prev = q_s[...], k_s[...], v_s[...]

    a = _rms_norm(x_ref[...], g_ref[...])
    ab = a.astype(BF16)
    o0 = pool_dim
    u = _dot(ab, w_ref[:, 0:o0])
    qf = _dot(ab, w_ref[:, o0:o0 + attn_dim]) * (LOG2E * HEAD_DIM ** -0.5)
    kf = _dot(ab, w_ref[:, o0 + attn_dim:o0 + 2 * attn_dim])
    vf = _dot(ab, w_ref[:, o0 + 2 * attn_dim:o0 + 3 * attn_dim])
    fl = _dot(ab, w_ref[:, o0 + 3 * attn_dim:]) + bf_ref[...]

    c1, c2, c3 = _split3(c_prev * LOG2E)
    lane = lax.broadcasted_iota(jnp.int32, (1, LANES), 1)
    piece_id = lane % N_SPLIT
    pieces = jnp.where(piece_id == 0, c1, jnp.where(piece_id == 1, c2, c3)).astype(F32)
    for h in range(N_HEADS):
        own = (lane >= _own_half_start(h)) & (lane < _own_half_start(h) + HEAD_DIM)
        pair = slice((h // 2) * LANES, (h // 2 + 1) * LANES)
        blk = slice(h * LANES, (h + 1) * LANES)
        spare = HEAD_DIM - _own_half_start(h)
        first = (lane >= spare) & (lane < spare + N_SPLIT)
        second = (lane >= spare + N_SPLIT) & (lane < spare + 2 * N_SPLIT)
        to_first = pltpu.roll(pieces, (spare - h * N_SPLIT) % LANES, axis=1)
        to_second = pltpu.roll(pieces, (spare + N_SPLIT - h * N_SPLIT) % LANES, axis=1)
        bias_q = jnp.where(first, to_first, jnp.where(second, 1.0, 0.0))
        bias_k = jnp.where(second, -to_second, jnp.where(first, 1.0, 0.0))
        q_ref[:, blk] = jnp.where(own, q_prev[:, pair], bias_q).astype(BF16)
        k_ref[:, blk] = jnp.where(own, k_prev[:, pair], bias_k).astype(BF16)
        one_col = (lane == HEAD_DIM - _own_half_start(h)).astype(F32)
        v_ref[:, blk] = jnp.where(own, v_prev[:, pair], one_col).astype(BF16)

    ext = jnp.concatenate([carry_u[...], u_prev], axis=0)
    carry_u[...] = u_prev[tm - POOL_HISTORY:, :]
    pos = (prev_seq_tile * tm + 1 + lax.broadcasted_iota(jnp.int32, (tm, 1), 0)).astype(F32)
    for gi, w in enumerate(POOL_WINDOWS):
        lo, hi = gi * POOL_GROUP_DIM, (gi + 1) * POOL_GROUP_DIM
        s = ext[:, lo:hi]
        shift = 1
        while shift < w:
            s = s + pltpu.roll(s, shift, axis=0)
            shift *= 2
        mean = s[POOL_HISTORY:, :] / jnp.minimum(pos, float(w))
        d = mean - u_prev[:, lo:hi]
        y = _dot(d.astype(BF16), wp_ref[gi]) * sp_ref[:, lo:hi]
        pool_ref[:, lo:hi] = y.astype(BF16)

    lf = jnp.minimum(fl, 0.0) - jnp.log1p(jnp.exp(-jnp.abs(fl)))
    row = lax.broadcasted_iota(jnp.int32, (tm, tm), 0)
    col = lax.broadcasted_iota(jnp.int32, (tm, tm), 1)
    tril = (col <= row).astype(BF16)
    sums = _dot(tril, jnp.concatenate(_split3(lf), axis=1))
    c = carry_c[...] + (sums[:, 0:LANES] + (sums[:, LANES:2 * LANES] + sums[:, 2 * LANES:]))
    carry_c[...] = c[tm - 1:tm, :]
    u_s[...] = u
    q_s[...] = qf
    k_s[...] = kf
    v_s[...] = vf
    c_s[...] = c


def _inproj(x2, g_mix, w_main, b_f, w_pool, s_pool, seq_len):
    n, d = x2.shape
    tm = ROW_TILE
    pool_dim = w_pool.shape[0] * w_pool.shape[1]
    head_w = N_HEADS * LANES
    attn_dim = N_HEADS * HEAD_DIM
    n_tiles = n // tm
    projected = lambda i: (jnp.minimum(i, n_tiles - 1), 0)
    finished = lambda i: (jnp.maximum(i - 1, 0), 0)
    full2 = lambda i: (0, 0)
    return pl.pallas_call(
        functools.partial(_inproj_kernel, seq_len // tm),
        grid=(n_tiles + 1,),
        in_specs=[
            pl.BlockSpec((tm, d), projected),
            pl.BlockSpec((1, d), full2),
            pl.BlockSpec(w_main.shape, full2),
            pl.BlockSpec((1, LANES), full2),
            pl.BlockSpec(w_pool.shape, lambda i: (0, 0, 0)),
            pl.BlockSpec((1, pool_dim), full2),
        ],
        out_specs=[
            pl.BlockSpec((tm, pool_dim), finished),
            pl.BlockSpec((tm, head_w), finished),
            pl.BlockSpec((tm, head_w), finished),
            pl.BlockSpec((tm, head_w), finished),
        ],
        out_shape=[
            jax.ShapeDtypeStruct((n, pool_dim), BF16),
            jax.ShapeDtypeStruct((n, head_w), BF16),
            jax.ShapeDtypeStruct((n, head_w), BF16),
            jax.ShapeDtypeStruct((n, head_w), BF16),
        ],
        scratch_shapes=[
            pltpu.VMEM((1, LANES), F32), pltpu.VMEM((POOL_HISTORY, pool_dim), F32),
            pltpu.VMEM((tm, pool_dim), F32), pltpu.VMEM((tm, attn_dim), F32),
            pltpu.VMEM((tm, attn_dim), F32), pltpu.VMEM((tm, attn_dim), F32),
            pltpu.VMEM((tm, LANES), F32),
        ],
        compiler_params=pltpu.CompilerParams(
            dimension_semantics=("arbitrary",), vmem_limit_bytes=VMEM_LIMIT),
        name="inproj",
    )(x2, g_mix, w_main, b_f, w_pool, s_pool)


def _attend_pair(q_ref, k_ref, v_ref, n_keys):
    tq = q_ref.shape[0]
    past = n_keys - tq
    nt = (((1,), (1,)), ((), ()))
    blks = [slice(hh * LANES, (hh + 1) * LANES) for hh in range(2)]
    row = lax.broadcasted_iota(jnp.int32, (tq, tq), 0)
    col = lax.broadcasted_iota(jnp.int32, (tq, tq), 1)
    qs = [q_ref[:, blk] for blk in blks]
    s_diag = [jnp.where(col <= row,
                        lax.dot_general(q, k_ref[past:n_keys, blk], nt,
                                        preferred_element_type=F32), NEG_INF)
              for q, blk in zip(qs, blks)]
    m = [jnp.max(s, axis=-1, keepdims=True) for s in s_diag]
    if past:
        s_past = [lax.dot_general(q, k_ref[0:past, blk], nt, preferred_element_type=F32)
                  for q, blk in zip(qs, blks)]
        m = [jnp.maximum(mi, jnp.max(s, axis=-1, keepdims=True)) for mi, s in zip(m, s_past)]
    acc = [_dot(jnp.exp2(s - mi).astype(BF16), v_ref[past:n_keys, blk])
           for s, mi, blk in zip(s_diag, m, blks)]
    if past:
        acc = [a + _dot(jnp.exp2(s - mi).astype(BF16), v_ref[0:past, blk])
               for a, s, mi, blk in zip(acc, s_past, m, blks)]
    return acc


def _attn_kernel(q_ref, k_ref, v_ref, o_ref):
    qi = pl.program_id(2)
    tq = q_ref.shape[0]
    lane = lax.broadcasted_iota(jnp.int32, (1, LANES), 1)
    for tile in range(k_ref.shape[0] // tq):
        @pl.when(qi == tile)
        def _():
            outs = []
            for hh, acc in enumerate(_attend_pair(q_ref, k_ref, v_ref, (tile + 1) * tq)):
                sum_lane = HEAD_DIM - _own_half_start(hh)
                l = jnp.sum(jnp.where(lane == sum_lane, acc, 0.0), axis=-1, keepdims=True)
                outs.append(acc / l)
            o_ref[...] = jnp.where(lane < HEAD_DIM, outs[0], outs[1]).astype(BF16)


def _attention(q, k, v, batch, seq_len):
    n = q.shape[0]
    pairs = N_HEADS // 2
    tq = ATTN_Q
    qt = seq_len // tq
    pair_block = lambda b, hp, qi: (b, hp)
    return pl.pallas_call(
        _attn_kernel,
        grid=(batch, pairs, qt),
        in_specs=[
            pl.BlockSpec((tq, 2 * LANES), lambda b, hp, qi: (b * qt + qi, hp)),
            pl.BlockSpec((seq_len, 2 * LANES), pair_block),
            pl.BlockSpec((seq_len, 2 * LANES), pair_block),
        ],
        out_specs=pl.BlockSpec((tq, LANES), lambda b, hp, qi: (b * qt + qi, hp)),
        out_shape=jax.ShapeDtypeStruct((n, pairs * LANES), BF16),
        compiler_params=pltpu.CompilerParams(
            dimension_semantics=("arbitrary", "arbitrary", "arbitrary"),
            vmem_limit_bytes=VMEM_LIMIT),
        name="attn",
    )(q, k, v)


def _route(logits):
    lane = lax.broadcasted_iota(jnp.int32, logits.shape, 1)
    big = jnp.int32(LANES)
    gl = jnp.where(lane < N_GROUPS, logits, NEG_INF)
    gmax = jnp.max(gl, axis=-1, keepdims=True)
    g_w = 1.0 / jnp.sum(jnp.exp(gl - gmax), axis=-1, keepdims=True)
    g_idx = jnp.min(jnp.where(gl == gmax, lane, big), axis=-1, keepdims=True)
    e_lo = ROUTER_LANE0 + EXPERTS_PER_GROUP * g_idx
    el = jnp.where((lane >= e_lo) & (lane < e_lo + EXPERTS_PER_GROUP), logits, NEG_INF)
    v1 = jnp.max(el, axis=-1, keepdims=True)
    i1 = jnp.min(jnp.where(el == v1, lane, big), axis=-1, keepdims=True)
    el2 = jnp.where(lane == i1, NEG_INF, el)
    v2 = jnp.max(el2, axis=-1, keepdims=True)
    i2 = jnp.min(jnp.where(el2 == v2, lane, big), axis=-1, keepdims=True)
    e2 = jnp.exp(v2 - v1)
    w1 = g_w / (1.0 + e2)
    w2 = g_w * e2 / (1.0 + e2)
    return (i1 - ROUTER_LANE0).astype(F32), (i2 - ROUTER_LANE0).astype(F32), w1, w2


EXT_W0 = 0
EXT_W1 = N_SPLIT
EXT_E0 = 2 * N_SPLIT


def _routing_record(e0, e1, w0, w1):
    lane = lax.broadcasted_iota(jnp.int32, (1, LANES), 1)
    rec = jnp.where(lane == EXT_E0, e0, jnp.where(lane == EXT_E0 + 1, e1, 0.0))
    for first, w in ((EXT_W0, w0), (EXT_W1, w1)):
        for k, piece in enumerate(_split3(w)):
            rec = jnp.where(lane == first + k, piece.astype(F32), rec)
    return rec.astype(BF16)


def _local_slots(e0, e1):
    t = e0.shape[0]
    lane = lax.broadcasted_iota(jnp.int32, (t, LANES), 1).astype(F32)
    oh0 = lane == e0
    oh1 = lane == e1
    picked = (oh0 | oh1).astype(BF16)
    cnt = jnp.sum(picked.astype(F32), axis=0, keepdims=True)
    units = jnp.floor((cnt + (SEG_ALIGN - 1)) * (1.0 / SEG_ALIGN))
    r128 = lax.broadcasted_iota(jnp.int32, (LANES, LANES), 0)
    c128 = lax.broadcasted_iota(jnp.int32, (LANES, LANES), 1)
    before = (r128 < c128).astype(BF16)
    lstart = SEG_ALIGN * _dot(jnp.broadcast_to(units, (8, LANES)).astype(BF16), before)[0:1, :]
    row = lax.broadcasted_iota(jnp.int32, (t, t), 0)
    col = lax.broadcasted_iota(jnp.int32, (t, t), 1)
    earlier = (col < row).astype(BF16)
    base = _dot(earlier, picked) + lstart
    slot0 = jnp.sum(jnp.where(oh0, base, 0.0), axis=-1, keepdims=True)
    slot1 = jnp.sum(jnp.where(oh1, base, 0.0), axis=-1, keepdims=True)
    return slot0, slot1, cnt


def _mix_kernel(x_ref, pool_ref, attn_ref, wo_ref, g_ref, wr_ref, br_ref,
                h_ref, m_ref, ext_ref, slots_ref, slots_t_ref, cnt_ref, logits_s):
    pool_dim = pool_ref.shape[1]

    @pl.when(pl.program_id(0) == 0)
    def _():
        logits_s[...] = jnp.zeros_like(logits_s)

    prev_logits = logits_s[...]
    h = x_ref[...] + (_dot(pool_ref[...], wo_ref[0:pool_dim, :])
                      + _dot(attn_ref[...], wo_ref[pool_dim:, :]))
    h_ref[...] = h
    e0, e1, w0, w1 = _route(prev_logits)
    ext_ref[...] = _routing_record(e0, e1, w0, w1)
    slot0, slot1, cnt = _local_slots(e0, e1)
    cnt_ref[0] = cnt
    lane = lax.broadcasted_iota(jnp.int32, (1, LANES), 1)
    slots = jnp.where(lane == 0, slot0, jnp.where(lane == 1, slot1, 0.0))
    slots_ref[...] = slots
    slots_t_ref[0] = slots.T[0:8, :]
    m = _rms_norm(h, g_ref[...])
    m_ref[...] = m.astype(BF16)
    logits_s[...] = _dot_precise(m, wr_ref[...]) + br_ref[...]


def _mix(x2, pool, attn, w_out, g_ffn, w_router, b_router):
    n, d = x2.shape
    tm = ROW_TILE
    n_tiles = n // tm
    row = lambda i: (jnp.minimum(i, n_tiles - 1), 0)
    routed = lambda i: (jnp.maximum(i - 1, 0), 0)
    full2 = lambda i: (0, 0)
    return pl.pallas_call(
        _mix_kernel,
        grid=(n_tiles + 1,),
        in_specs=[
            pl.BlockSpec((tm, d), row),
            pl.BlockSpec((tm, pool.shape[1]), row),
            pl.BlockSpec((tm, attn.shape[1]), row),
            pl.BlockSpec(w_out.shape, full2),
            pl.BlockSpec((1, d), full2),
            pl.BlockSpec(w_router.shape, full2),
            pl.BlockSpec((1, LANES), full2),
        ],
        out_specs=[
            pl.BlockSpec((tm, d), row),
            pl.BlockSpec((tm, d), row),
            pl.BlockSpec((tm, LANES), routed),
            pl.BlockSpec((tm, LANES), routed),
            pl.BlockSpec((1, 8, tm), lambda i: (jnp.maximum(i - 1, 0), 0, 0)),
            pl.BlockSpec((1, 1, LANES), lambda i: (jnp.maximum(i - 1, 0), 0, 0)),
        ],
        out_shape=[
            jax.ShapeDtypeStruct((n, d), F32),
            jax.ShapeDtypeStruct((n, d), BF16),
            jax.ShapeDtypeStruct((n, LANES), BF16),
            jax.ShapeDtypeStruct((n, LANES), F32),
            jax.ShapeDtypeStruct((n_tiles, 8, tm), F32),
            jax.ShapeDtypeStruct((n_tiles, 1, LANES), F32),
        ],
        scratch_shapes=[pltpu.VMEM((tm, LANES), F32)],
        compiler_params=pltpu.CompilerParams(
            dimension_semantics=("arbitrary",), vmem_limit_bytes=VMEM_LIMIT),
        name="mix",
    )(x2, pool, attn, w_out, g_ffn, w_router, b_router)


SEG_ALIGN = 16
COPY_UNITS = 4
COPY_ROWS = COPY_UNITS * SEG_ALIGN
EXPERT_TILE = 512
LOCAL_ROWS = 2 * ROW_TILE + N_EXPERTS * SEG_ALIGN


TRIM_ROWS = 128
TRIM_OPTIONS = 3


def _used_row_options():
    return [LOCAL_ROWS - k * TRIM_ROWS for k in reversed(range(TRIM_OPTIONS))]


def _round_up_to_option(rows, options):
    out = jnp.int32(options[-1])
    for opt in reversed(options[:-1]):
        out = jnp.where(rows <= opt, jnp.int32(opt), out)
    return out


def _permutation(slot0, slot1, slots_axis, n_slots=None):
    n_slots = LOCAL_ROWS if n_slots is None else n_slots
    shape = (1, n_slots) if slots_axis == 1 else (n_slots, 1)
    s = lax.broadcasted_iota(jnp.int32, shape, slots_axis)
    return ((s == slot0.astype(jnp.int32)) | (s == slot1.astype(jnp.int32))).astype(BF16)


def _copy_classes():
    classes = [(COPY_UNITS, LOCAL_ROWS // COPY_ROWS)]
    units = COPY_UNITS // 2
    while units >= 1:
        classes.append((units, N_EXPERTS))
        units //= 2
    return classes


def _for_each_piece(list_refs, tile, make_copy, act):
    for (units, k_max), (local_ref, hbm_ref, count_ref) in zip(_copy_classes(), list_refs):
        def body(k, carry, units=units, k_max=k_max, local_ref=local_ref, hbm_ref=hbm_ref):
            idx = tile * k_max + k
            act(make_copy(pl.multiple_of(local_ref[idx] * SEG_ALIGN, SEG_ALIGN),
                          pl.multiple_of(hbm_ref[idx] * SEG_ALIGN, SEG_ALIGN),
                          units * SEG_ALIGN))
            return carry

        lax.fori_loop(0, count_ref[tile], body, 0)


def _group_lists(refs):
    n = len(_copy_classes())
    return [tuple(refs[3 * c:3 * c + 3]) for c in range(n)], refs[3 * n], refs[3 * n + 1:]


def _wait_rows(total_units, make_copy):
    for b in range((LOCAL_ROWS // SEG_ALIGN).bit_length()):
        @pl.when(((total_units >> b) & 1) == 1)
        def _():
            make_copy(0, 0, SEG_ALIGN << b).wait()


def _start(copy):
    copy.start()


def _wait(copy):
    copy.wait()


def _dispatch_kernel(n_steps, *refs):
    copy_lists, tot_ref, rest = _group_lists(refs)
    (tail0_ref, tailn_ref, nu_ref, m_ref, ext_ref, st_ref, st_next_ref, xs_hbm,
     buf, zeros, perm_even, perm_odd, sems, tail_sem) = rest
    i = pl.program_id(0)
    slot = i % 2
    d = m_ref.shape[1]
    n_expert_tiles = xs_hbm.shape[0] // EXPERT_TILE

    def seg_copy(slot_):
        def make(l, g, rows):
            return pltpu.make_async_copy(buf.at[slot_, pl.ds(l, rows), :],
                                         xs_hbm.at[pl.ds(g, rows), :], sems.at[slot_])
        return make

    def for_each_tail(act):
        @pl.when(i < N_EXPERTS)
        def _():
            e = jnp.minimum(i, N_EXPERTS - 1)
            g0 = tail0_ref[e] * SEG_ALIGN
            n = tailn_ref[e]
            for b in range((EXPERT_TILE // SEG_ALIGN - 1).bit_length()):
                @pl.when(((n >> b) & 1) == 1)
                def _():
                    rows = SEG_ALIGN << b
                    higher = (n >> (b + 1)) << (b + 1)
                    g = pl.multiple_of(g0 + higher * SEG_ALIGN, SEG_ALIGN)
                    act(pltpu.make_async_copy(zeros.at[pl.ds(0, rows), :],
                                              xs_hbm.at[pl.ds(g, rows), :], tail_sem))

        for k in range(-(-n_expert_tiles // n_steps)):
            t = nu_ref[0] + i + k * n_steps

            @pl.when(t < n_expert_tiles)
            def _():
                g = pl.multiple_of(t * EXPERT_TILE, EXPERT_TILE)
                act(pltpu.make_async_copy(zeros, xs_hbm.at[pl.ds(g, EXPERT_TILE), :], tail_sem))

    @pl.when(i == 0)
    def _():
        zeros[...] = jnp.zeros_like(zeros)

    for_each_tail(_start)

    def build(slots_t, perm):
        perm[...] = _permutation(slots_t[0, 0:1, :], slots_t[0, 1:2, :], slots_axis=0)

    @pl.when(i == 0)
    def _():
        build(st_ref, perm_even)

    def permute(perm, next_perm, rows):
        p = perm[0:rows, :]
        build(st_next_ref, next_perm)
        buf[slot, 0:rows, 0:d] = _dot(p, m_ref[...]).astype(BF16)
        buf[slot, 0:rows, d:] = _dot(p, ext_ref[...]).astype(BF16)

    row_options = _used_row_options()
    used = _round_up_to_option(tot_ref[i] * SEG_ALIGN, row_options)
    for parity, (perm, next_perm) in enumerate(((perm_even, perm_odd), (perm_odd, perm_even))):
        for rows in row_options:
            @pl.when((slot == parity) & (used == rows))
            def _():
                permute(perm, next_perm, rows)

    _for_each_piece(copy_lists, i, seg_copy(slot), _start)

    @pl.when(i > 0)
    def _():
        _wait_rows(tot_ref[i - 1], seg_copy(1 - slot))

    for_each_tail(_wait)

    @pl.when(i == n_steps - 1)
    def _():
        _wait_rows(tot_ref[i], seg_copy(slot))


def _dispatch(m, ext, slots_t, tables, n_rows):
    n, d = m.shape
    tm = ROW_TILE
    assert n // tm >= N_EXPERTS, "each grid step zero-fills the tail of one expert"
    width = d + LANES
    last = n // tm - 1
    row = lambda i, *_: (i, 0)
    return pl.pallas_call(
        functools.partial(_dispatch_kernel, n // tm),
        grid_spec=pltpu.PrefetchScalarGridSpec(
            num_scalar_prefetch=len(tables),
            grid=(n // tm,),
            in_specs=[
                pl.BlockSpec((tm, d), row),
                pl.BlockSpec((tm, LANES), row),
                pl.BlockSpec((1, 8, tm), lambda i, *_: (i, 0, 0)),
                pl.BlockSpec((1, 8, tm), lambda i, *_: (jnp.minimum(i + 1, last), 0, 0)),
            ],
            out_specs=pl.BlockSpec(memory_space=pl.ANY),
            scratch_shapes=[
                pltpu.VMEM((2, LOCAL_ROWS, width), BF16),
                pltpu.VMEM((EXPERT_TILE, width), BF16),
                pltpu.VMEM((LOCAL_ROWS, tm), BF16),
                pltpu.VMEM((LOCAL_ROWS, tm), BF16),
                pltpu.SemaphoreType.DMA((2,)),
                pltpu.SemaphoreType.DMA(()),
            ],
        ),
        out_shape=jax.ShapeDtypeStruct((n_rows, width), BF16),
        compiler_params=pltpu.CompilerParams(
            dimension_semantics=("arbitrary",), vmem_limit_bytes=VMEM_LIMIT),
        name="dispatch",
    )(*tables, m, ext, slots_t, slots_t)


PART_ROWS = 128


def _expert_kernel(first_ref, count_ref, valid_ref, nu_ref, wg_ref, wu_ref, wd_ref, xs_hbm, os_hbm,
                   xbuf, obuf, zeros, wgu_s, wd_s, in_sems, out_sems, zero_sem):
    e = pl.program_id(0)
    d = os_hbm.shape[1]
    f = wg_ref.shape[2]
    n = count_ref[e]
    t0 = first_ref[e]
    n_tiles = os_hbm.shape[0] // EXPERT_TILE

    def rows(t):
        return pl.ds(pl.multiple_of(t * EXPERT_TILE, EXPERT_TILE), EXPERT_TILE)

    def in_copy(k, slot, first=t0):
        return pltpu.make_async_copy(xs_hbm.at[rows(first + k), :], xbuf.at[slot],
                                     in_sems.at[slot])

    def out_copy(k, slot):
        return pltpu.make_async_copy(obuf.at[slot], os_hbm.at[rows(t0 + k), :], out_sems.at[slot])

    def for_each_unused(act):
        for j in range(-(-n_tiles // N_EXPERTS)):
            t = nu_ref[0] + e + j * N_EXPERTS

            @pl.when(t < n_tiles)
            def _():
                act(pltpu.make_async_copy(zeros, os_hbm.at[rows(t), :], zero_sem))

    @pl.when(e == 0)
    def _():
        zeros[...] = jnp.zeros_like(zeros)

    for_each_unused(_start)

    @pl.when((e == 0) & (n > 0))
    def _():
        in_copy(0, 0).start()

    @pl.when(n > 0)
    def _():
        wgu_s[:, 0:f] = wg_ref[0].astype(BF16)
        wgu_s[:, f:] = wu_ref[0].astype(BF16)
        wd_s[...] = wd_ref[0].astype(BF16)

    lane = lax.broadcasted_iota(jnp.int32, (1, LANES), 1)

    def ffn(slot, m_rows):
        x = xbuf[slot, 0:m_rows, 0:d]
        rec = xbuf[slot, 0:m_rows, d:].astype(F32)

        def lanes_sum(first, count):
            keep = (lane >= first) & (lane < first + count)
            return jnp.sum(jnp.where(keep, rec, 0.0), axis=-1, keepdims=True)

        first_choice = lanes_sum(EXT_E0, 1) == e.astype(F32)
        w = jnp.where(first_choice, lanes_sum(EXT_W0, N_SPLIT), lanes_sum(EXT_W1, N_SPLIT))
        h = _dot(x, wgu_s[...])
        hg = h[:, 0:f]
        hu = h[:, f:]
        a = hg * jax.nn.sigmoid(hg) * hu * w
        obuf[slot, 0:m_rows, :] = _dot(a.astype(BF16), wd_s[...]).astype(BF16)
        if m_rows < EXPERT_TILE:
            obuf[slot, m_rows:, :] = jnp.zeros((EXPERT_TILE - m_rows, d), BF16)

    def tile(k, carry):
        slot = k % 2

        @pl.when(k + 1 < n)
        def _():
            in_copy(k + 1, 1 - slot).start()

        in_copy(k, slot).wait()

        @pl.when(k >= 2)
        def _():
            out_copy(k - 2, slot).wait()

        parts_used = jnp.minimum(
            (valid_ref[e] - k * EXPERT_TILE + PART_ROWS - 1) // PART_ROWS, EXPERT_TILE // PART_ROWS)
        for parts in range(1, EXPERT_TILE // PART_ROWS + 1):
            @pl.when(parts_used == parts)
            def _():
                ffn(slot, parts * PART_ROWS)

        out_copy(k, slot).start()
        return carry

    lax.fori_loop(0, n, tile, 0)

    for back in (2, 1):
        @pl.when(n >= back)
        def _():
            out_copy(n - back, (n - back) % 2).wait()

    nxt = jnp.minimum(e + 1, N_EXPERTS - 1)

    @pl.when((e + 1 < N_EXPERTS) & (count_ref[nxt] > 0))
    def _():
        in_copy(0, 0, first_ref[nxt]).start()

    for_each_unused(_wait)


def _experts(xs, first_tile, tile_count, valid_rows, n_used, w_gate, w_up, w_down):
    n_rows, width = xs.shape
    n_exp, d, f = w_gate.shape
    te = EXPERT_TILE
    weight = lambda e, *_: (e, 0, 0)
    return pl.pallas_call(
        _expert_kernel,
        grid_spec=pltpu.PrefetchScalarGridSpec(
            num_scalar_prefetch=4,
            grid=(n_exp,),
            in_specs=[
                pl.BlockSpec((1, d, f), weight),
                pl.BlockSpec((1, d, f), weight),
                pl.BlockSpec((1, f, d), weight),
                pl.BlockSpec(memory_space=pl.ANY),
            ],
            out_specs=pl.BlockSpec(memory_space=pl.ANY),
            scratch_shapes=[
                pltpu.VMEM((2, te, width), BF16),
                pltpu.VMEM((2, te, d), BF16),
                pltpu.VMEM((te, d), BF16),
                pltpu.VMEM((d, 2 * f), BF16), pltpu.VMEM((f, d), BF16),
                pltpu.SemaphoreType.DMA((2,)),
                pltpu.SemaphoreType.DMA((2,)),
                pltpu.SemaphoreType.DMA(()),
            ],
        ),
        out_shape=jax.ShapeDtypeStruct((n_rows, d), BF16),
        compiler_params=pltpu.CompilerParams(
            dimension_semantics=("arbitrary",), vmem_limit_bytes=VMEM_LIMIT),
        name="experts",
    )(first_tile, tile_count, valid_rows, n_used, w_gate, w_up, w_down, xs)


def _compact(valid, local, hbm, k_max):
    pos = jnp.cumsum(valid, axis=1) - valid
    k = jnp.arange(k_max, dtype=jnp.int32)[None, :, None]
    hit = (pos[:, None, :] == k) & (valid[:, None, :] == 1)
    pick = lambda a: jnp.sum(jnp.where(hit, a[:, None, :], 0), axis=2).reshape(-1)
    return pick(local), pick(hbm), jnp.sum(valid, axis=1)


def _copy_lists(n_units, lrow, grow):
    tables = ()
    for units, k_max in _copy_classes():
        if units == COPY_UNITS:
            per_seg = ROW_TILE // COPY_ROWS
            c = jnp.arange(per_seg, dtype=jnp.int32)[None, None, :]
            valid = (c < (n_units // COPY_UNITS)[:, :, None]).astype(jnp.int32)
            off = jnp.broadcast_to(c * COPY_UNITS, valid.shape)
        else:
            valid = ((n_units // units) % 2)[:, :, None]
            off = ((n_units // (2 * units)) * (2 * units))[:, :, None]
        flat = lambda a: a.reshape(a.shape[0], -1)
        tables += _compact(flat(valid), flat(lrow[:, :, None] + off), flat(grow[:, :, None] + off),
                           k_max)
    return tables


def _moe_layout(cnt, n_pairs):
    c = cnt[:, 0, :N_EXPERTS].astype(jnp.int32)
    n_tiles = c.shape[0]
    n8 = (c + SEG_ALIGN - 1) // SEG_ALIGN
    lrow = jnp.cumsum(n8, axis=1) - n8
    units_e = jnp.sum(n8, axis=0)
    per_tile = EXPERT_TILE // SEG_ALIGN
    tiles_e = (units_e + per_tile - 1) // per_tile
    e_end = jnp.cumsum(tiles_e)
    e_off = (e_end - tiles_e) * per_tile
    grow = e_off[None, :] + jnp.cumsum(n8, axis=0) - n8
    tail0 = e_off + units_e
    tailn = tiles_e * per_tile - units_e
    worst = n_pairs + n_tiles * N_EXPERTS * (SEG_ALIGN - 1) + N_EXPERTS * (EXPERT_TILE - SEG_ALIGN)
    n_rows = -(-worst // EXPERT_TILE) * EXPERT_TILE
    seg_tables = _copy_lists(n8, lrow, grow) + (jnp.sum(n8, axis=1),)
    expert_tables = (e_end - tiles_e, tiles_e, units_e * SEG_ALIGN, e_end[-1:])
    return seg_tables, (tail0, tailn), expert_tables, n_rows


def _combine_kernel(*refs):
    copy_lists, tot_ref, rest = _group_lists(refs)
    h_ref, slots_ref, p_ref, g_ref, wg_ref, wp_ref, gf_ref, os_hbm, o_ref, buf, sems = rest
    i = pl.program_id(0)
    nt = pl.num_programs(0)
    slot = i % 2

    def seg_copy(slot_):
        def make(l, g, rows):
            return pltpu.make_async_copy(os_hbm.at[pl.ds(g, rows), :],
                                         buf.at[slot_, pl.ds(l, rows), :], sems.at[slot_])
        return make

    @pl.when(i == 0)
    def _():
        buf[...] = jnp.zeros_like(buf)
        _for_each_piece(copy_lists, 0, seg_copy(0), _start)

    @pl.when(i + 1 < nt)
    def _():
        _for_each_piece(copy_lists, i + 1, seg_copy(1 - slot), _start)

    _wait_rows(tot_ref[i], seg_copy(slot))

    def finish(rows):
        slots = slots_ref[...]
        perm = _permutation(slots[:, 0:1], slots[:, 1:2], slots_axis=1, n_slots=rows)
        h = h_ref[...] + _dot(perm, buf[slot, 0:rows, :])
        gate = jax.nn.sigmoid(_dot(_rms_norm(h, g_ref[...]).astype(BF16), wg_ref[...]))
        h = h + gate * _dot(p_ref[...].astype(BF16), wp_ref[...])
        o_ref[...] = _rms_norm(h, gf_ref[...])

    row_options = _used_row_options()
    used = _round_up_to_option(tot_ref[i] * SEG_ALIGN, row_options)
    for rows in row_options:
        @pl.when(used == rows)
        def _():
            finish(rows)


def _combine(h1, slots, seg_tables, o_sorted, p2, g_ple, w_gate, w_proj, g_final):
    n, d = h1.shape
    tm = ROW_TILE
    row = lambda i, *_: (i, 0)
    full2 = lambda i, *_: (0, 0)
    return pl.pallas_call(
        _combine_kernel,
        grid_spec=pltpu.PrefetchScalarGridSpec(
            num_scalar_prefetch=len(seg_tables),
            grid=(n // tm,),
            in_specs=[
                pl.BlockSpec((tm, d), row),
                pl.BlockSpec((tm, LANES), row),
                pl.BlockSpec((tm, p2.shape[1]), row),
                pl.BlockSpec((1, d), full2),
                pl.BlockSpec(w_gate.shape, full2),
                pl.BlockSpec(w_proj.shape, full2),
                pl.BlockSpec((1, d), full2),
                pl.BlockSpec(memory_space=pl.ANY),
            ],
            out_specs=pl.BlockSpec((tm, d), row),
            scratch_shapes=[
                pltpu.VMEM((2, LOCAL_ROWS, d), BF16),
                pltpu.SemaphoreType.DMA((2,)),
            ],
        ),
        out_shape=jax.ShapeDtypeStruct((n, d), F32),
        compiler_params=pltpu.CompilerParams(
            dimension_semantics=("arbitrary",), vmem_limit_bytes=VMEM_LIMIT),
        name="combine",
    )(*seg_tables, h1, slots, p2, g_ple, w_gate, w_proj, g_final, o_sorted)


def _pad_lanes(a):
    return jnp.pad(a, ((0, 0), (0, LANES - a.shape[1])))


def kernel(x, p, g_mix, w_in, b_f, w_pool, s_pool, w_out, g_ffn, w_grp, b_grp, w_rt, b_rt,
           w_e_gate, w_e_up, w_e_down, g_ple, w_ple_gate, w_ple_proj, g_final):
    batch, seq_len, d = x.shape
    n = batch * seq_len
    assert w_in.shape[0] == 1, "single-layer stack only: the final norm is fused into the layer"
    i = 0
    pool_dim = s_pool.shape[1]
    attn_dim = N_HEADS * HEAD_DIM
    main = pool_dim + 3 * attn_dim
    h = x.reshape(n, d)
    w_f = _pad_lanes(jnp.repeat(w_in[i, :, main:], N_SPLIT, axis=1))
    w_main = jnp.concatenate([w_in[i, :, :main], w_f], axis=1).astype(BF16)
    b_f3 = _pad_lanes(jnp.repeat(b_f[i], N_SPLIT)[None])
    pool, q, k, v = _inproj(h, g_mix[i][None], w_main, b_f3,
                            w_pool[i].astype(BF16), s_pool[i][None], seq_len)
    attn = _attention(q, k, v, batch, seq_len)
    w_router = _pad_lanes(jnp.concatenate([w_grp[i], w_rt[i]], axis=1))
    b_router = _pad_lanes(jnp.concatenate([b_grp[i], b_rt[i]])[None])
    h1, m, ext, slots, slots_t, cnt = _mix(h, pool, attn, w_out[i].astype(BF16), g_ffn[i][None],
                                          w_router, b_router)
    seg_tables, tail_tables, expert_tables, n_rows = _moe_layout(cnt, 2 * n)
    x_sorted = _dispatch(m, ext, slots_t, seg_tables + tail_tables + expert_tables[3:], n_rows)
    f = w_e_gate.shape[-1]
    o_sorted = _experts(x_sorted, *expert_tables,
                        w_e_gate[i].reshape(N_EXPERTS, d, f),
                        w_e_up[i].reshape(N_EXPERTS, d, f),
                        w_e_down[i].reshape(N_EXPERTS, f, d))
    out = _combine(h1, slots, seg_tables, o_sorted, p[i].reshape(n, -1), g_ple[i][None],
                   w_ple_gate[i].astype(BF16), w_ple_proj[i].astype(BF16), g_final[None])
    return out.reshape(batch, seq_len, d)
```

```python
import functools
import math

import jax
import jax.numpy as jnp
from jax import lax
from jax.experimental import pallas as pl
from jax.experimental.pallas import tpu as pltpu

HEAD_DIM = 64
N_HEADS = 8
POOL_WINDOWS = (2, 4, 8, 16)
POOL_GROUP_DIM = 128
POOL_HISTORY = 16
N_GROUPS = 4
EXPERTS_PER_GROUP = 8
N_EXPERTS = N_GROUPS * EXPERTS_PER_GROUP
EPS = 1e-6
LANES = 128
ROUTER_LANE0 = N_GROUPS
NEG_INF = float("-inf")
LOG2E = math.log2(math.e)
N_SPLIT = 3

ROW_TILE = 512
ATTN_Q = 512
VMEM_LIMIT = 48 * 1024 * 1024

BF16 = jnp.bfloat16
F32 = jnp.float32


def _dot(a, b):
    return jnp.dot(a, b, preferred_element_type=F32)


def _split2(a):
    hi = a.astype(BF16)
    lo = (a - hi.astype(F32)).astype(BF16)
    return hi, lo


def _split3(a):
    hi = a.astype(BF16)
    r = a - hi.astype(F32)
    mid = r.astype(BF16)
    lo = (r - mid.astype(F32)).astype(BF16)
    return hi, mid, lo


def _dot_precise(a, w):
    a1, a2 = _split2(a)
    w1, w2 = _split2(w)
    n = w.shape[1]
    r = _dot(a1, jnp.concatenate([w1, w2], axis=1))
    return r[:, 0:n] + (r[:, n:] + _dot(a2, w1))


def _rms_norm(x, g):
    return x * lax.rsqrt(jnp.mean(x * x, axis=-1, keepdims=True) + EPS) * g


def _own_half_start(head):
    return 0 if head % 2 == 0 else HEAD_DIM


def _inproj_kernel(tiles_per_seq, x_ref, g_ref, w_ref, bf_ref, wp_ref, sp_ref,
                   pool_ref, q_ref, k_ref, v_ref,
                   carry_c, carry_u, u_s, q_s, k_s, v_s, c_s):
    i = pl.program_id(0)
    tm = x_ref.shape[0]
    pool_dim = pool_ref.shape[1]
    attn_dim = N_HEADS * HEAD_DIM

    @pl.when(i == 0)
    def _():
        for ref in (u_s, q_s, k_s, v_s, c_s, carry_u):
            ref[...] = jnp.zeros_like(ref)

    @pl.when(i % tiles_per_seq == 0)
    def _():
        carry_c[...] = jnp.zeros_like(carry_c)

    prev_seq_tile = (i + tiles_per_seq - 1) % tiles_per_seq

    @pl.when(prev_seq_tile == 0)
    def _():
        carry_u[...] = jnp.zeros_like(carry_u)

    u_prev, c_prev = u_s[...], c_s[...]
    q_prev, k_prev, v_prev = q_s[...], k_s[...], v_s[...]

    a = _rms_norm(x_ref[...], g_ref[...])
    ab = a.astype(BF16)
    o0 = pool_dim
    u = _dot(ab, w_ref[:, 0:o0])
    qf = _dot(ab, w_ref[:, o0:o0 + attn_dim]) * (LOG2E * HEAD_DIM ** -0.5)
    kf = _dot(ab, w_ref[:, o0 + attn_dim:o0 + 2 * attn_dim])
    vf = _dot(ab, w_ref[:, o0 + 2 * attn_dim:o0 + 3 * attn_dim])
    fl = _dot(ab, w_ref[:, o0 + 3 * attn_dim:]) + bf_ref[...]

    c1, c2, c3 = _split3(c_prev * LOG2E)
    lane = lax.broadcasted_iota(jnp.int32, (1, LANES), 1)
    piece_id = lane % N_SPLIT
    pieces = jnp.where(piece_id == 0, c1, jnp.where(piece_id == 1, c2, c3)).astype(F32)
    for h in range(N_HEADS):
        own = (lane >= _own_half_start(h)) & (lane < _own_half_start(h) + HEAD_DIM)
        pair = slice((h // 2) * LANES, (h // 2 + 1) * LANES)
        blk = slice(h * LANES, (h + 1) * LANES)
        spare = HEAD_DIM - _own_half_start(h)
        first = (lane >= spare) & (lane < spare + N_SPLIT)
        second = (lane >= spare + N_SPLIT) & (lane < spare + 2 * N_SPLIT)
        to_first = pltpu.roll(pieces, (spare - h * N_SPLIT) % LANES, axis=1)
        to_second = pltpu.roll(pieces, (spare + N_SPLIT - h * N_SPLIT) % LANES, axis=1)
        bias_q = jnp.where(first, to_first, jnp.where(second, 1.0, 0.0))
        bias_k = jnp.where(second, -to_second, jnp.where(first, 1.0, 0.0))
        q_ref[:, blk] = jnp.where(own, q_prev[:, pair], bias_q).astype(BF16)
        k_ref[:, blk] = jnp.where(own, k_prev[:, pair], bias_k).astype(BF16)
        one_col = (lane == HEAD_DIM - _own_half_start(h)).astype(F32)
        v_ref[:, blk] = jnp.where(own, v_prev[:, pair], one_col).astype(BF16)

    ext = jnp.concatenate([carry_u[...], u_prev], axis=0)
    carry_u[...] = u_prev[tm - POOL_HISTORY:, :]
    pos = (prev_seq_tile * tm + 1 + lax.broadcasted_iota(jnp.int32, (tm, 1), 0)).astype(F32)
    for gi, w in enumerate(POOL_WINDOWS):
        lo, hi = gi * POOL_GROUP_DIM, (gi + 1) * POOL_GROUP_DIM
        s = ext[:, lo:hi]
        shift = 1
        while shift < w:
            s = s + pltpu.roll(s, shift, axis=0)
            shift *= 2
        mean = s[POOL_HISTORY:, :] / jnp.minimum(pos, float(w))
        d = mean - u_prev[:, lo:hi]
        y = _dot(d.astype(BF16), wp_ref[gi]) * sp_ref[:, lo:hi]
        pool_ref[:, lo:hi] = y.astype(BF16)

    lf = jnp.minimum(fl, 0.0) - jnp.log1p(jnp.exp(-jnp.abs(fl)))
    row = lax.broadcasted_iota(jnp.int32, (tm, tm), 0)
    col = lax.broadcasted_iota(jnp.int32, (tm, tm), 1)
    tril = (col <= row).astype(BF16)
    sums = _dot(tril, jnp.concatenate(_split3(lf), axis=1))
    c = carry_c[...] + (sums[:, 0:LANES] + (sums[:, LANES:2 * LANES] + sums[:, 2 * LANES:]))
    carry_c[...] = c[tm - 1:tm, :]
    u_s[...] = u
    q_s[...] = qf
    k_s[...] = kf
    v_s[...] = vf
    c_s[...] = c


def _inproj(x2, g_mix, w_main, b_f, w_pool, s_pool, seq_len):
    n, d = x2.shape
    tm = ROW_TILE
    pool_dim = w_pool.shape[0] * w_pool.shape[1]
    head_w = N_HEADS * LANES
    attn_dim = N_HEADS * HEAD_DIM
    n_tiles = n // tm
    projected = lambda i: (jnp.minimum(i, n_tiles - 1), 0)
    finished = lambda i: (jnp.maximum(i - 1, 0), 0)
    full2 = lambda i: (0, 0)
    return pl.pallas_call(
        functools.partial(_inproj_kernel, seq_len // tm),
        grid=(n_tiles + 1,),
        in_specs=[
            pl.BlockSpec((tm, d), projected),
            pl.BlockSpec((1, d), full2),
            pl.BlockSpec(w_main.shape, full2),
            pl.BlockSpec((1, LANES), full2),
            pl.BlockSpec(w_pool.shape, lambda i: (0, 0, 0)),
            pl.BlockSpec((1, pool_dim), full2),
        ],
        out_specs=[
            pl.BlockSpec((tm, pool_dim), finished),
            pl.BlockSpec((tm, head_w), finished),
            pl.BlockSpec((tm, head_w), finished),
            pl.BlockSpec((tm, head_w), finished),
        ],
        out_shape=[
            jax.ShapeDtypeStruct((n, pool_dim), BF16),
            jax.ShapeDtypeStruct((n, head_w), BF16),
            jax.ShapeDtypeStruct((n, head_w), BF16),
            jax.ShapeDtypeStruct((n, head_w), BF16),
        ],
        scratch_shapes=[
            pltpu.VMEM((1, LANES), F32), pltpu.VMEM((POOL_HISTORY, pool_dim), F32),
            pltpu.VMEM((tm, pool_dim), F32), pltpu.VMEM((tm, attn_dim), F32),
            pltpu.VMEM((tm, attn_dim), F32), pltpu.VMEM((tm, attn_dim), F32),
            pltpu.VMEM((tm, LANES), F32),
        ],
        compiler_params=pltpu.CompilerParams(
            dimension_semantics=("arbitrary",), vmem_limit_bytes=VMEM_LIMIT),
        name="inproj",
    )(x2, g_mix, w_main, b_f, w_pool, s_pool)


def _attend_pair(q_ref, k_ref, v_ref, n_keys):
    tq = q_ref.shape[0]
    past = n_keys - tq
    nt = (((1,), (1,)), ((), ()))
    blks = [slice(hh * LANES, (hh + 1) * LANES) for hh in range(2)]
    row = lax.broadcasted_iota(jnp.int32, (tq, tq), 0)
    col = lax.broadcasted_iota(jnp.int32, (tq, tq), 1)
    qs = [q_ref[:, blk] for blk in blks]
    s_diag = [jnp.where(col <= row,
                        lax.dot_general(q, k_ref[past:n_keys, blk], nt,
                                        preferred_element_type=F32), NEG_INF)
              for q, blk in zip(qs, blks)]
    m = [jnp.max(s, axis=-1, keepdims=True) for s in s_diag]
    if past:
        s_past = [lax.dot_general(q, k_ref[0:past, blk], nt, preferred_element_type=F32)
                  for q, blk in zip(qs, blks)]
        m = [jnp.maximum(mi, jnp.max(s, axis=-1, keepdims=True)) for mi, s in zip(m, s_past)]
    acc = [_dot(jnp.exp2(s - mi).astype(BF16), v_ref[past:n_keys, blk])
           for s, mi, blk in zip(s_diag, m, blks)]
    if past:
        acc = [a + _dot(jnp.exp2(s - mi).astype(BF16), v_ref[0:past, blk])
               for a, s, mi, blk in zip(acc, s_past, m, blks)]
    return acc


def _attn_kernel(q_ref, k_ref, v_ref, o_ref):
    qi = pl.program_id(2)
    tq = q_ref.shape[0]
    lane = lax.broadcasted_iota(jnp.int32, (1, LANES), 1)
    for tile in range(k_ref.shape[0] // tq):
        @pl.when(qi == tile)
        def _():
            outs = []
            for hh, acc in enumerate(_attend_pair(q_ref, k_ref, v_ref, (tile + 1) * tq)):
                sum_lane = HEAD_DIM - _own_half_start(hh)
                l = jnp.sum(jnp.where(lane == sum_lane, acc, 0.0), axis=-1, keepdims=True)
                outs.append(acc / l)
            o_ref[...] = jnp.where(lane < HEAD_DIM, outs[0], outs[1]).astype(BF16)


def _attention(q, k, v, batch, seq_len):
    n = q.shape[0]
    pairs = N_HEADS // 2
    tq = ATTN_Q
    qt = seq_len // tq
    pair_block = lambda b, hp, qi: (b, hp)
    return pl.pallas_call(
        _attn_kernel,
        grid=(batch, pairs, qt),
        in_specs=[
            pl.BlockSpec((tq, 2 * LANES), lambda b, hp, qi: (b * qt + qi, hp)),
            pl.BlockSpec((seq_len, 2 * LANES), pair_block),
            pl.BlockSpec((seq_len, 2 * LANES), pair_block),
        ],
        out_specs=pl.BlockSpec((tq, LANES), lambda b, hp, qi: (b * qt + qi, hp)),
        out_shape=jax.ShapeDtypeStruct((n, pairs * LANES), BF16),
        compiler_params=pltpu.CompilerParams(
            dimension_semantics=("arbitrary", "arbitrary", "arbitrary"),
            vmem_limit_bytes=VMEM_LIMIT),
        name="attn",
    )(q, k, v)


def _route(logits):
    lane = lax.broadcasted_iota(jnp.int32, logits.shape, 1)
    big = jnp.int32(LANES)
    gl = jnp.where(lane < N_GROUPS, logits, NEG_INF)
    gmax = jnp.max(gl, axis=-1, keepdims=True)
    g_w = 1.0 / jnp.sum(jnp.exp(gl - gmax), axis=-1, keepdims=True)
    g_idx = jnp.min(jnp.where(gl == gmax, lane, big), axis=-1, keepdims=True)
    e_lo = ROUTER_LANE0 + EXPERTS_PER_GROUP * g_idx
    el = jnp.where((lane >= e_lo) & (lane < e_lo + EXPERTS_PER_GROUP), logits, NEG_INF)
    v1 = jnp.max(el, axis=-1, keepdims=True)
    i1 = jnp.min(jnp.where(el == v1, lane, big), axis=-1, keepdims=True)
    el2 = jnp.where(lane == i1, NEG_INF, el)
    v2 = jnp.max(el2, axis=-1, keepdims=True)
    i2 = jnp.min(jnp.where(el2 == v2, lane, big), axis=-1, keepdims=True)
    e2 = jnp.exp(v2 - v1)
    w1 = g_w / (1.0 + e2)
    w2 = g_w * e2 / (1.0 + e2)
    return (i1 - ROUTER_LANE0).astype(F32), (i2 - ROUTER_LANE0).astype(F32), w1, w2


EXT_W0 = 0
EXT_W1 = N_SPLIT
EXT_E0 = 2 * N_SPLIT


def _routing_record(e0, e1, w0, w1):
    lane = lax.broadcasted_iota(jnp.int32, (1, LANES), 1)
    rec = jnp.where(lane == EXT_E0, e0, jnp.where(lane == EXT_E0 + 1, e1, 0.0))
    for first, w in ((EXT_W0, w0), (EXT_W1, w1)):
        for k, piece in enumerate(_split3(w)):
            rec = jnp.where(lane == first + k, piece.astype(F32), rec)
    return rec.astype(BF16)


def _local_slots(e0, e1):
    t = e0.shape[0]
    lane = lax.broadcasted_iota(jnp.int32, (t, LANES), 1).astype(F32)
    oh0 = lane == e0
    oh1 = lane == e1
    picked = (oh0 | oh1).astype(BF16)
    cnt = jnp.sum(picked.astype(F32), axis=0, keepdims=True)
    units = jnp.floor((cnt + (SEG_ALIGN - 1)) * (1.0 / SEG_ALIGN))
    r128 = lax.broadcasted_iota(jnp.int32, (LANES, LANES), 0)
    c128 = lax.broadcasted_iota(jnp.int32, (LANES, LANES), 1)
    before = (r128 < c128).astype(BF16)
    lstart = SEG_ALIGN * _dot(jnp.broadcast_to(units, (8, LANES)).astype(BF16), before)[0:1, :]
    row = lax.broadcasted_iota(jnp.int32, (t, t), 0)
    col = lax.broadcasted_iota(jnp.int32, (t, t), 1)
    earlier = (col < row).astype(BF16)
    base = _dot(earlier, picked) + lstart
    slot0 = jnp.sum(jnp.where(oh0, base, 0.0), axis=-1, keepdims=True)
    slot1 = jnp.sum(jnp.where(oh1, base, 0.0), axis=-1, keepdims=True)
    return slot0, slot1, cnt


def _mix_kernel(x_ref, pool_ref, attn_ref, wo_ref, g_ref, wr_ref, br_ref,
                h_ref, m_ref, ext_ref, slots_ref, slots_t_ref, cnt_ref, logits_s):
    pool_dim = pool_ref.shape[1]

    @pl.when(pl.program_id(0) == 0)
    def _():
        logits_s[...] = jnp.zeros_like(logits_s)

    prev_logits = logits_s[...]
    h = x_ref[...] + (_dot(pool_ref[...], wo_ref[0:pool_dim, :])
                      + _dot(attn_ref[...], wo_ref[pool_dim:, :]))
    h_ref[...] = h
    e0, e1, w0, w1 = _route(prev_logits)
    ext_ref[...] = _routing_record(e0, e1, w0, w1)
    slot0, slot1, cnt = _local_slots(e0, e1)
    cnt_ref[0] = cnt
    lane = lax.broadcasted_iota(jnp.int32, (1, LANES), 1)
    slots = jnp.where(lane == 0, slot0, jnp.where(lane == 1, slot1, 0.0))
    slots_ref[...] = slots
    slots_t_ref[0] = slots.T[0:8, :]
    m = _rms_norm(h, g_ref[...])
    m_ref[...] = m.astype(BF16)
    logits_s[...] = _dot_precise(m, wr_ref[...]) + br_ref[...]


def _mix(x2, pool, attn, w_out, g_ffn, w_router, b_router):
    n, d = x2.shape
    tm = ROW_TILE
    n_tiles = n // tm
    row = lambda i: (jnp.minimum(i, n_tiles - 1), 0)
    routed = lambda i: (jnp.maximum(i - 1, 0), 0)
    full2 = lambda i: (0, 0)
    return pl.pallas_call(
        _mix_kernel,
        grid=(n_tiles + 1,),
        in_specs=[
            pl.BlockSpec((tm, d), row),
            pl.BlockSpec((tm, pool.shape[1]), row),
            pl.BlockSpec((tm, attn.shape[1]), row),
            pl.BlockSpec(w_out.shape, full2),
            pl.BlockSpec((1, d), full2),
            pl.BlockSpec(w_router.shape, full2),
            pl.BlockSpec((1, LANES), full2),
        ],
        out_specs=[
            pl.BlockSpec((tm, d), row),
            pl.BlockSpec((tm, d), row),
            pl.BlockSpec((tm, LANES), routed),
            pl.BlockSpec((tm, LANES), routed),
            pl.BlockSpec((1, 8, tm), lambda i: (jnp.maximum(i - 1, 0), 0, 0)),
            pl.BlockSpec((1, 1, LANES), lambda i: (jnp.maximum(i - 1, 0), 0, 0)),
        ],
        out_shape=[
            jax.ShapeDtypeStruct((n, d), F32),
            jax.ShapeDtypeStruct((n, d), BF16),
            jax.ShapeDtypeStruct((n, LANES), BF16),
            jax.ShapeDtypeStruct((n, LANES), F32),
            jax.ShapeDtypeStruct((n_tiles, 8, tm), F32),
            jax.ShapeDtypeStruct((n_tiles, 1, LANES), F32),
        ],
        scratch_shapes=[pltpu.VMEM((tm, LANES), F32)],
        compiler_params=pltpu.CompilerParams(
            dimension_semantics=("arbitrary",), vmem_limit_bytes=VMEM_LIMIT),
        name="mix",
    )(x2, pool, attn, w_out, g_ffn, w_router, b_router)


SEG_ALIGN = 16
COPY_UNITS = 4
COPY_ROWS = COPY_UNITS * SEG_ALIGN
EXPERT_TILE = 768
LOCAL_ROWS = 2 * ROW_TILE + N_EXPERTS * SEG_ALIGN


TRIM_ROWS = 128
TRIM_OPTIONS = 3


def _used_row_options():
    return [LOCAL_ROWS - k * TRIM_ROWS for k in reversed(range(TRIM_OPTIONS))]


def _round_up_to_option(rows, options):
    out = jnp.int32(options[-1])
    for opt in reversed(options[:-1]):
        out = jnp.where(rows <= opt, jnp.int32(opt), out)
    return out


def _permutation(slot0, slot1, slots_axis, n_slots=None):
    n_slots = LOCAL_ROWS if n_slots is None else n_slots
    shape = (1, n_slots) if slots_axis == 1 else (n_slots, 1)
    s = lax.broadcasted_iota(jnp.int32, shape, slots_axis)
    return ((s == slot0.astype(jnp.int32)) | (s == slot1.astype(jnp.int32))).astype(BF16)


def _copy_classes():
    classes = [(COPY_UNITS, LOCAL_ROWS // COPY_ROWS)]
    units = COPY_UNITS // 2
    while units >= 1:
        classes.append((units, N_EXPERTS))
        units //= 2
    return classes


def _for_each_piece(list_refs, tile, make_copy, act):
    for (units, k_max), (local_ref, hbm_ref, count_ref) in zip(_copy_classes(), list_refs):
        def body(k, carry, units=units, k_max=k_max, local_ref=local_ref, hbm_ref=hbm_ref):
            idx = tile * k_max + k
            act(make_copy(pl.multiple_of(local_ref[idx] * SEG_ALIGN, SEG_ALIGN),
                          pl.multiple_of(hbm_ref[idx] * SEG_ALIGN, SEG_ALIGN),
                          units * SEG_ALIGN))
            return carry

        lax.fori_loop(0, count_ref[tile], body, 0)


def _group_lists(refs):
    n = len(_copy_classes())
    return [tuple(refs[3 * c:3 * c + 3]) for c in range(n)], refs[3 * n], refs[3 * n + 1:]


def _wait_rows(total_units, make_copy):
    for b in range((LOCAL_ROWS // SEG_ALIGN).bit_length()):
        @pl.when(((total_units >> b) & 1) == 1)
        def _():
            make_copy(0, 0, SEG_ALIGN << b).wait()


def _start(copy):
    copy.start()


def _wait(copy):
    copy.wait()


def _dispatch_kernel(n_steps, *refs):
    copy_lists, tot_ref, rest = _group_lists(refs)
    (tail0_ref, tailn_ref, nu_ref, m_ref, ext_ref, st_ref, st_next_ref, xs_hbm,
     buf, zeros, perm_even, perm_odd, sems, tail_sem) = rest
    i = pl.program_id(0)
    slot = i % 2
    d = m_ref.shape[1]
    n_expert_tiles = xs_hbm.shape[0] // EXPERT_TILE

    def seg_copy(slot_):
        def make(l, g, rows):
            return pltpu.make_async_copy(buf.at[slot_, pl.ds(l, rows), :],
                                         xs_hbm.at[pl.ds(g, rows), :], sems.at[slot_])
        return make

    def for_each_tail(act):
        @pl.when(i < N_EXPERTS)
        def _():
            e = jnp.minimum(i, N_EXPERTS - 1)
            g0 = tail0_ref[e] * SEG_ALIGN
            n = tailn_ref[e]
            for b in range((EXPERT_TILE // SEG_ALIGN - 1).bit_length()):
                @pl.when(((n >> b) & 1) == 1)
                def _():
                    rows = SEG_ALIGN << b
                    higher = (n >> (b + 1)) << (b + 1)
                    g = pl.multiple_of(g0 + higher * SEG_ALIGN, SEG_ALIGN)
                    act(pltpu.make_async_copy(zeros.at[pl.ds(0, rows), :],
                                              xs_hbm.at[pl.ds(g, rows), :], tail_sem))

        for k in range(-(-n_expert_tiles // n_steps)):
            t = nu_ref[0] + i + k * n_steps

            @pl.when(t < n_expert_tiles)
            def _():
                g = pl.multiple_of(t * EXPERT_TILE, EXPERT_TILE)
                act(pltpu.make_async_copy(zeros, xs_hbm.at[pl.ds(g, EXPERT_TILE), :], tail_sem))

    @pl.when(i == 0)
    def _():
        zeros[...] = jnp.zeros_like(zeros)

    for_each_tail(_start)

    def build(slots_t, perm):
        perm[...] = _permutation(slots_t[0, 0:1, :], slots_t[0, 1:2, :], slots_axis=0)

    @pl.when(i == 0)
    def _():
        build(st_ref, perm_even)

    def permute(perm, next_perm, rows):
        p = perm[0:rows, :]
        build(st_next_ref, next_perm)
        buf[slot, 0:rows, 0:d] = _dot(p, m_ref[...]).astype(BF16)
        buf[slot, 0:rows, d:] = _dot(p, ext_ref[...]).astype(BF16)

    row_options = _used_row_options()
    used = _round_up_to_option(tot_ref[i] * SEG_ALIGN, row_options)
    for parity, (perm, next_perm) in enumerate(((perm_even, perm_odd), (perm_odd, perm_even))):
        for rows in row_options:
            @pl.when((slot == parity) & (used == rows))
            def _():
                permute(perm, next_perm, rows)

    _for_each_piece(copy_lists, i, seg_copy(slot), _start)

    @pl.when(i > 0)
    def _():
        _wait_rows(tot_ref[i - 1], seg_copy(1 - slot))

    for_each_tail(_wait)

    @pl.when(i == n_steps - 1)
    def _():
        _wait_rows(tot_ref[i], seg_copy(slot))


def _dispatch(m, ext, slots_t, tables, n_rows):
    n, d = m.shape
    tm = ROW_TILE
    assert n // tm >= N_EXPERTS, "each grid step zero-fills the tail of one expert"
    width = d + LANES
    last = n // tm - 1
    row = lambda i, *_: (i, 0)
    return pl.pallas_call(
        functools.partial(_dispatch_kernel, n // tm),
        grid_spec=pltpu.PrefetchScalarGridSpec(
            num_scalar_prefetch=len(tables),
            grid=(n // tm,),
            in_specs=[
                pl.BlockSpec((tm, d), row),
                pl.BlockSpec((tm, LANES), row),
                pl.BlockSpec((1, 8, tm), lambda i, *_: (i, 0, 0)),
                pl.BlockSpec((1, 8, tm), lambda i, *_: (jnp.minimum(i + 1, last), 0, 0)),
            ],
            out_specs=pl.BlockSpec(memory_space=pl.ANY),
            scratch_shapes=[
                pltpu.VMEM((2, LOCAL_ROWS, width), BF16),
                pltpu.VMEM((EXPERT_TILE, width), BF16),
                pltpu.VMEM((LOCAL_ROWS, tm), BF16),
                pltpu.VMEM((LOCAL_ROWS, tm), BF16),
                pltpu.SemaphoreType.DMA((2,)),
                pltpu.SemaphoreType.DMA(()),
            ],
        ),
        out_shape=jax.ShapeDtypeStruct((n_rows, width), BF16),
        compiler_params=pltpu.CompilerParams(
            dimension_semantics=("arbitrary",), vmem_limit_bytes=VMEM_LIMIT),
        name="dispatch",
    )(*tables, m, ext, slots_t, slots_t)


PART_ROWS = 64


def _expert_kernel(first_ref, count_ref, valid_ref, nu_ref, wg_ref, wu_ref, wd_ref, xs_hbm, os_hbm,
                   xbuf, obuf, zeros, wgu_s, wd_s, in_sems, out_sems, zero_sem):
    e = pl.program_id(0)
    d = os_hbm.shape[1]
    f = wg_ref.shape[2]
    n = count_ref[e]
    t0 = first_ref[e]
    n_tiles = os_hbm.shape[0] // EXPERT_TILE

    def rows(t):
        return pl.ds(pl.multiple_of(t * EXPERT_TILE, EXPERT_TILE), EXPERT_TILE)

    def in_copy(k, slot, first=t0):
        return pltpu.make_async_copy(xs_hbm.at[rows(first + k), :], xbuf.at[slot],
                                     in_sems.at[slot])

    def out_copy(k, slot):
        return pltpu.make_async_copy(obuf.at[slot], os_hbm.at[rows(t0 + k), :], out_sems.at[slot])

    def for_each_unused(act):
        for j in range(-(-n_tiles // N_EXPERTS)):
            t = nu_ref[0] + e + j * N_EXPERTS

            @pl.when(t < n_tiles)
            def _():
                act(pltpu.make_async_copy(zeros, os_hbm.at[rows(t), :], zero_sem))

    @pl.when(e == 0)
    def _():
        zeros[...] = jnp.zeros_like(zeros)

    for_each_unused(_start)

    @pl.when((e == 0) & (n > 0))
    def _():
        in_copy(0, 0).start()

    @pl.when(n > 0)
    def _():
        wgu_s[:, 0:f] = wg_ref[0].astype(BF16)
        wgu_s[:, f:] = wu_ref[0].astype(BF16)
        wd_s[...] = wd_ref[0].astype(BF16)

    lane = lax.broadcasted_iota(jnp.int32, (1, LANES), 1)

    def ffn(slot, m_rows):
        x = xbuf[slot, 0:m_rows, 0:d]
        rec = xbuf[slot, 0:m_rows, d:].astype(F32)

        def lanes_sum(first, count):
            keep = (lane >= first) & (lane < first + count)
            return jnp.sum(jnp.where(keep, rec, 0.0), axis=-1, keepdims=True)

        first_choice = lanes_sum(EXT_E0, 1) == e.astype(F32)
        w = jnp.where(first_choice, lanes_sum(EXT_W0, N_SPLIT), lanes_sum(EXT_W1, N_SPLIT))
        h = _dot(x, wgu_s[...])
        hg = h[:, 0:f]
        hu = h[:, f:]
        a = hg * jax.nn.sigmoid(hg) * hu * w
        obuf[slot, 0:m_rows, :] = _dot(a.astype(BF16), wd_s[...]).astype(BF16)
        if m_rows < EXPERT_TILE:
            obuf[slot, m_rows:, :] = jnp.zeros((EXPERT_TILE - m_rows, d), BF16)

    def tile(k, carry):
        slot = k % 2

        @pl.when(k + 1 < n)
        def _():
            in_copy(k + 1, 1 - slot).start()

        in_copy(k, slot).wait()

        @pl.when(k >= 2)
        def _():
            out_copy(k - 2, slot).wait()

        parts_used = jnp.minimum(
            (valid_ref[e] - k * EXPERT_TILE + PART_ROWS - 1) // PART_ROWS, EXPERT_TILE // PART_ROWS)
        for parts in range(1, EXPERT_TILE // PART_ROWS + 1):
            @pl.when(parts_used == parts)
            def _():
                ffn(slot, parts * PART_ROWS)

        out_copy(k, slot).start()
        return carry

    lax.fori_loop(0, n, tile, 0)

    for back in (2, 1):
        @pl.when(n >= back)
        def _():
            out_copy(n - back, (n - back) % 2).wait()

    nxt = jnp.minimum(e + 1, N_EXPERTS - 1)

    @pl.when((e + 1 < N_EXPERTS) & (count_ref[nxt] > 0))
    def _():
        in_copy(0, 0, first_ref[nxt]).start()

    for_each_unused(_wait)


def _experts(xs, first_tile, tile_count, valid_rows, n_used, w_gate, w_up, w_down):
    n_rows, width = xs.shape
    n_exp, d, f = w_gate.shape
    te = EXPERT_TILE
    weight = lambda e, *_: (e, 0, 0)
    return pl.pallas_call(
        _expert_kernel,
        grid_spec=pltpu.PrefetchScalarGridSpec(
            num_scalar_prefetch=4,
            grid=(n_exp,),
            in_specs=[
                pl.BlockSpec((1, d, f), weight),
                pl.BlockSpec((1, d, f), weight),
                pl.BlockSpec((1, f, d), weight),
                pl.BlockSpec(memory_space=pl.ANY),
            ],
            out_specs=pl.BlockSpec(memory_space=pl.ANY),
            scratch_shapes=[
                pltpu.VMEM((2, te, width), BF16),
                pltpu.VMEM((2, te, d), BF16),
                pltpu.VMEM((te, d), BF16),
                pltpu.VMEM((d, 2 * f), BF16), pltpu.VMEM((f, d), BF16),
                pltpu.SemaphoreType.DMA((2,)),
                pltpu.SemaphoreType.DMA((2,)),
                pltpu.SemaphoreType.DMA(()),
            ],
        ),
        out_shape=jax.ShapeDtypeStruct((n_rows, d), BF16),
        compiler_params=pltpu.CompilerParams(
            dimension_semantics=("arbitrary",), vmem_limit_bytes=VMEM_LIMIT),
        name="experts",
    )(first_tile, tile_count, valid_rows, n_used, w_gate, w_up, w_down, xs)


def _compact(valid, local, hbm, k_max):
    pos = jnp.cumsum(valid, axis=1) - valid
    k = jnp.arange(k_max, dtype=jnp.int32)[None, :, None]
    hit = (pos[:, None, :] == k) & (valid[:, None, :] == 1)
    pick = lambda a: jnp.sum(jnp.where(hit, a[:, None, :], 0), axis=2).reshape(-1)
    return pick(local), pick(hbm), jnp.sum(valid, axis=1)


def _copy_lists(n_units, lrow, grow):
    tables = ()
    for units, k_max in _copy_classes():
        if units == COPY_UNITS:
            per_seg = ROW_TILE // COPY_ROWS
            c = jnp.arange(per_seg, dtype=jnp.int32)[None, None, :]
            valid = (c < (n_units // COPY_UNITS)[:, :, None]).astype(jnp.int32)
            off = jnp.broadcast_to(c * COPY_UNITS, valid.shape)
        else:
            valid = ((n_units // units) % 2)[:, :, None]
            off = ((n_units // (2 * units)) * (2 * units))[:, :, None]
        flat = lambda a: a.reshape(a.shape[0], -1)
        tables += _compact(flat(valid), flat(lrow[:, :, None] + off), flat(grow[:, :, None] + off),
                           k_max)
    return tables


def _moe_layout(cnt, n_pairs):
    c = cnt[:, 0, :N_EXPERTS].astype(jnp.int32)
    n_tiles = c.shape[0]
    n8 = (c + SEG_ALIGN - 1) // SEG_ALIGN
    lrow = jnp.cumsum(n8, axis=1) - n8
    units_e = jnp.sum(n8, axis=0)
    per_tile = EXPERT_TILE // SEG_ALIGN
    tiles_e = (units_e + per_tile - 1) // per_tile
    e_end = jnp.cumsum(tiles_e)
    e_off = (e_end - tiles_e) * per_tile
    grow = e_off[None, :] + jnp.cumsum(n8, axis=0) - n8
    tail0 = e_off + units_e
    tailn = tiles_e * per_tile - units_e
    worst = n_pairs + n_tiles * N_EXPERTS * (SEG_ALIGN - 1) + N_EXPERTS * (EXPERT_TILE - SEG_ALIGN)
    n_rows = -(-worst // EXPERT_TILE) * EXPERT_TILE
    seg_tables = _copy_lists(n8, lrow, grow) + (jnp.sum(n8, axis=1),)
    expert_tables = (e_end - tiles_e, tiles_e, units_e * SEG_ALIGN, e_end[-1:])
    return seg_tables, (tail0, tailn), expert_tables, n_rows


def _combine_kernel(*refs):
    copy_lists, tot_ref, rest = _group_lists(refs)
    h_ref, slots_ref, p_ref, g_ref, wg_ref, wp_ref, gf_ref, os_hbm, o_ref, buf, sems = rest
    i = pl.program_id(0)
    nt = pl.num_programs(0)
    slot = i % 2

    def seg_copy(slot_):
        def make(l, g, rows):
            return pltpu.make_async_copy(os_hbm.at[pl.ds(g, rows), :],
                                         buf.at[slot_, pl.ds(l, rows), :], sems.at[slot_])
        return make

    @pl.when(i == 0)
    def _():
        buf[...] = jnp.zeros_like(buf)
        _for_each_piece(copy_lists, 0, seg_copy(0), _start)

    @pl.when(i + 1 < nt)
    def _():
        _for_each_piece(copy_lists, i + 1, seg_copy(1 - slot), _start)

    _wait_rows(tot_ref[i], seg_copy(slot))

    def finish(rows):
        slots = slots_ref[...]
        perm = _permutation(slots[:, 0:1], slots[:, 1:2], slots_axis=1, n_slots=rows)
        h = h_ref[...] + _dot(perm, buf[slot, 0:rows, :])
        gate = jax.nn.sigmoid(_dot(_rms_norm(h, g_ref[...]).astype(BF16), wg_ref[...]))
        h = h + gate * _dot(p_ref[...].astype(BF16), wp_ref[...])
        o_ref[...] = _rms_norm(h, gf_ref[...])

    row_options = _used_row_options()
    used = _round_up_to_option(tot_ref[i] * SEG_ALIGN, row_options)
    for rows in row_options:
        @pl.when(used == rows)
        def _():
            finish(rows)


def _combine(h1, slots, seg_tables, o_sorted, p2, g_ple, w_gate, w_proj, g_final):
    n, d = h1.shape
    tm = ROW_TILE
    row = lambda i, *_: (i, 0)
    full2 = lambda i, *_: (0, 0)
    return pl.pallas_call(
        _combine_kernel,
        grid_spec=pltpu.PrefetchScalarGridSpec(
            num_scalar_prefetch=len(seg_tables),
            grid=(n // tm,),
            in_specs=[
                pl.BlockSpec((tm, d), row),
                pl.BlockSpec((tm, LANES), row),
                pl.BlockSpec((tm, p2.shape[1]), row),
                pl.BlockSpec((1, d), full2),
                pl.BlockSpec(w_gate.shape, full2),
                pl.BlockSpec(w_proj.shape, full2),
                pl.BlockSpec((1, d), full2),
                pl.BlockSpec(memory_space=pl.ANY),
            ],
            out_specs=pl.BlockSpec((tm, d), row),
            scratch_shapes=[
                pltpu.VMEM((2, LOCAL_ROWS, d), BF16),
                pltpu.SemaphoreType.DMA((2,)),
            ],
        ),
        out_shape=jax.ShapeDtypeStruct((n, d), F32),
        compiler_params=pltpu.CompilerParams(
            dimension_semantics=("arbitrary",), vmem_limit_bytes=VMEM_LIMIT),
        name="combine",
    )(*seg_tables, h1, slots, p2, g_ple, w_gate, w_proj, g_final, o_sorted)


def _pad_lanes(a):
    return jnp.pad(a, ((0, 0), (0, LANES - a.shape[1])))


def kernel(x, p, g_mix, w_in, b_f, w_pool, s_pool, w_out, g_ffn, w_grp, b_grp, w_rt, b_rt,
           w_e_gate, w_e_up, w_e_down, g_ple, w_ple_gate, w_ple_proj, g_final):
    batch, seq_len, d = x.shape
    n = batch * seq_len
    assert w_in.shape[0] == 1, "single-layer stack only: the final norm is fused into the layer"
    i = 0
    pool_dim = s_pool.shape[1]
    attn_dim = N_HEADS * HEAD_DIM
    main = pool_dim + 3 * attn_dim
    h = x.reshape(n, d)
    w_f = _pad_lanes(jnp.repeat(w_in[i, :, main:], N_SPLIT, axis=1))
    w_main = jnp.concatenate([w_in[i, :, :main], w_f], axis=1).astype(BF16)
    b_f3 = _pad_lanes(jnp.repeat(b_f[i], N_SPLIT)[None])
    pool, q, k, v = _inproj(h, g_mix[i][None], w_main, b_f3,
                            w_pool[i].astype(BF16), s_pool[i][None], seq_len)
    attn = _attention(q, k, v, batch, seq_len)
    w_router = _pad_lanes(jnp.concatenate([w_grp[i], w_rt[i]], axis=1))
    b_router = _pad_lanes(jnp.concatenate([b_grp[i], b_rt[i]])[None])
    h1, m, ext, slots, slots_t, cnt = _mix(h, pool, attn, w_out[i].astype(BF16), g_ffn[i][None],
                                          w_router, b_router)
    seg_tables, tail_tables, expert_tables, n_rows = _moe_layout(cnt, 2 * n)
    x_sorted = _dispatch(m, ext, slots_t, seg_tables + tail_tables + expert_tables[3:], n_rows)
    f = w_e_gate.shape[-1]
    o_sorted = _experts(x_sorted, *expert_tables,
                        w_e_gate[i].reshape(N_EXPERTS, d, f),
                        w_e_up[i].reshape(N_EXPERTS, d, f),
                        w_e_down[i].reshape(N_EXPERTS, f, d))
    out = _combine(h1, slots, seg_tables, o_sorted, p[i].reshape(n, -1), g_ple[i][None],
                   w_ple_gate[i].astype(BF16), w_ple_proj[i].astype(BF16), g_final[None])
    return out.reshape(batch, seq_len, d)
```

```python
import functools
import math

import jax
import jax.numpy as jnp
from jax import lax
from jax.experimental import pallas as pl
from jax.experimental.pallas import tpu as pltpu

HEAD_DIM = 64
N_HEADS = 8
POOL_WINDOWS = (2, 4, 8, 16)
POOL_GROUP_DIM = 128
POOL_HISTORY = 16
N_GROUPS = 4
EXPERTS_PER_GROUP = 8
N_EXPERTS = N_GROUPS * EXPERTS_PER_GROUP
EPS = 1e-6
LANES = 128
ROUTER_LANE0 = N_GROUPS
NEG_INF = float("-inf")
LOG2E = math.log2(math.e)
N_SPLIT = 3

ROW_TILE = 512
ATTN_Q = 512
VMEM_LIMIT = 48 * 1024 * 1024

BF16 = jnp.bfloat16
F32 = jnp.float32


def _dot(a, b):
    return jnp.dot(a, b, preferred_element_type=F32)


def _split2(a):
    hi = a.astype(BF16)
    lo = (a - hi.astype(F32)).astype(BF16)
    return hi, lo


def _split3(a):
    hi = a.astype(BF16)
    r = a - hi.astype(F32)
    mid = r.astype(BF16)
    lo = (r - mid.astype(F32)).astype(BF16)
    return hi, mid, lo


def _dot_precise(a, w):
    a1, a2 = _split2(a)
    w1, w2 = _split2(w)
    n = w.shape[1]
    r = _dot(a1, jnp.concatenate([w1, w2], axis=1))
    return r[:, 0:n] + (r[:, n:] + _dot(a2, w1))


def _rms_norm(x, g):
    return x * lax.rsqrt(jnp.mean(x * x, axis=-1, keepdims=True) + EPS) * g


def _own_half_start(head):
    return 0 if head % 2 == 0 else HEAD_DIM


def _inproj_kernel(tiles_per_seq, x_ref, g_ref, w_ref, bf_ref, wp_ref, sp_ref,
                   pool_ref, q_ref, k_ref, v_ref,
                   carry_c, carry_u, u_s, q_s, k_s, v_s, c_s):
    i = pl.program_id(0)
    tm = x_ref.shape[0]
    pool_dim = pool_ref.shape[1]
    attn_dim = N_HEADS * HEAD_DIM

    @pl.when(i == 0)
    def _():
        for ref in (u_s, q_s, k_s, v_s, c_s, carry_u):
            ref[...] = jnp.zeros_like(ref)

    @pl.when(i % tiles_per_seq == 0)
    def _():
        carry_c[...] = jnp.zeros_like(carry_c)

    prev_seq_tile = (i + tiles_per_seq - 1) % tiles_per_seq

    @pl.when(prev_seq_tile == 0)
    def _():
        carry_u[...] = jnp.zeros_like(carry_u)

    u_prev, c_prev = u_s[...], c_s[...]
    q_prev, k_prev, v_prev = q_s[...], k_s[...], v_s[...]

    a = _rms_norm(x_ref[...], g_ref[...])
    ab = a.astype(BF16)
    o0 = pool_dim
    u = _dot(ab, w_ref[:, 0:o0])
    qf = _dot(ab, w_ref[:, o0:o0 + attn_dim]) * (LOG2E * HEAD_DIM ** -0.5)
    kf = _dot(ab, w_ref[:, o0 + attn_dim:o0 + 2 * attn_dim])
    vf = _dot(ab, w_ref[:, o0 + 2 * attn_dim:o0 + 3 * attn_dim])
    fl = _dot(ab, w_ref[:, o0 + 3 * attn_dim:]) + bf_ref[...]

    c1, c2, c3 = _split3(c_prev * LOG2E)
    lane = lax.broadcasted_iota(jnp.int32, (1, LANES), 1)
    piece_id = lane % N_SPLIT
    pieces = jnp.where(piece_id == 0, c1, jnp.where(piece_id == 1, c2, c3)).astype(F32)
    for h in range(N_HEADS):
        own = (lane >= _own_half_start(h)) & (lane < _own_half_start(h) + HEAD_DIM)
        pair = slice((h // 2) * LANES, (h // 2 + 1) * LANES)
        blk = slice(h * LANES, (h + 1) * LANES)
        spare = HEAD_DIM - _own_half_start(h)
        first = (lane >= spare) & (lane < spare + N_SPLIT)
        second = (lane >= spare + N_SPLIT) & (lane < spare + 2 * N_SPLIT)
        to_first = pltpu.roll(pieces, (spare - h * N_SPLIT) % LANES, axis=1)
        to_second = pltpu.roll(pieces, (spare + N_SPLIT - h * N_SPLIT) % LANES, axis=1)
        bias_q = jnp.where(first, to_first, jnp.where(second, 1.0, 0.0))
        bias_k = jnp.where(second, -to_second, jnp.where(first, 1.0, 0.0))
        q_ref[:, blk] = jnp.where(own, q_prev[:, pair], bias_q).astype(BF16)
        k_ref[0, blk, :] = jnp.where(own, k_prev[:, pair], bias_k).T.astype(BF16)
        one_col = (lane == HEAD_DIM - _own_half_start(h)).astype(F32)
        v_ref[:, blk] = jnp.where(own, v_prev[:, pair], one_col).astype(BF16)

    ext = jnp.concatenate([carry_u[...], u_prev], axis=0)
    carry_u[...] = u_prev[tm - POOL_HISTORY:, :]
    pos = (prev_seq_tile * tm + 1 + lax.broadcasted_iota(jnp.int32, (tm, 1), 0)).astype(F32)
    for gi, w in enumerate(POOL_WINDOWS):
        lo, hi = gi * POOL_GROUP_DIM, (gi + 1) * POOL_GROUP_DIM
        s = ext[:, lo:hi]
        shift = 1
        while shift < w:
            s = s + pltpu.roll(s, shift, axis=0)
            shift *= 2
        mean = s[POOL_HISTORY:, :] / jnp.minimum(pos, float(w))
        d = mean - u_prev[:, lo:hi]
        y = _dot(d.astype(BF16), wp_ref[gi]) * sp_ref[:, lo:hi]
        pool_ref[:, lo:hi] = y.astype(BF16)

    lf = jnp.minimum(fl, 0.0) - jnp.log1p(jnp.exp(-jnp.abs(fl)))
    row = lax.broadcasted_iota(jnp.int32, (tm, tm), 0)
    col = lax.broadcasted_iota(jnp.int32, (tm, tm), 1)
    tril = (col <= row).astype(BF16)
    sums = _dot(tril, jnp.concatenate(_split3(lf), axis=1))
    c = carry_c[...] + (sums[:, 0:LANES] + (sums[:, LANES:2 * LANES] + sums[:, 2 * LANES:]))
    carry_c[...] = c[tm - 1:tm, :]
    u_s[...] = u
    q_s[...] = qf
    k_s[...] = kf
    v_s[...] = vf
    c_s[...] = c


def _inproj(x2, g_mix, w_main, b_f, w_pool, s_pool, seq_len):
    n, d = x2.shape
    tm = ROW_TILE
    pool_dim = w_pool.shape[0] * w_pool.shape[1]
    head_w = N_HEADS * LANES
    attn_dim = N_HEADS * HEAD_DIM
    n_tiles = n // tm
    projected = lambda i: (jnp.minimum(i, n_tiles - 1), 0)
    finished = lambda i: (jnp.maximum(i - 1, 0), 0)
    tiles_per_seq = seq_len // tm

    def finished_keys(i):
        j = jnp.maximum(i - 1, 0)
        return (j // tiles_per_seq, 0, j % tiles_per_seq)

    full2 = lambda i: (0, 0)
    return pl.pallas_call(
        functools.partial(_inproj_kernel, seq_len // tm),
        grid=(n_tiles + 1,),
        in_specs=[
            pl.BlockSpec((tm, d), projected),
            pl.BlockSpec((1, d), full2),
            pl.BlockSpec(w_main.shape, full2),
            pl.BlockSpec((1, LANES), full2),
            pl.BlockSpec(w_pool.shape, lambda i: (0, 0, 0)),
            pl.BlockSpec((1, pool_dim), full2),
        ],
        out_specs=[
            pl.BlockSpec((tm, pool_dim), finished),
            pl.BlockSpec((tm, head_w), finished),
            pl.BlockSpec((1, head_w, tm), finished_keys),
            pl.BlockSpec((tm, head_w), finished),
        ],
        out_shape=[
            jax.ShapeDtypeStruct((n, pool_dim), BF16),
            jax.ShapeDtypeStruct((n, head_w), BF16),
            jax.ShapeDtypeStruct((n // seq_len, head_w, seq_len), BF16),
            jax.ShapeDtypeStruct((n, head_w), BF16),
        ],
        scratch_shapes=[
            pltpu.VMEM((1, LANES), F32), pltpu.VMEM((POOL_HISTORY, pool_dim), F32),
            pltpu.VMEM((tm, pool_dim), F32), pltpu.VMEM((tm, attn_dim), F32),
            pltpu.VMEM((tm, attn_dim), F32), pltpu.VMEM((tm, attn_dim), F32),
            pltpu.VMEM((tm, LANES), F32),
        ],
        compiler_params=pltpu.CompilerParams(
            dimension_semantics=("arbitrary",), vmem_limit_bytes=VMEM_LIMIT),
        name="inproj",
    )(x2, g_mix, w_main, b_f, w_pool, s_pool)


def _attend_pair(q_ref, k_ref, v_ref, n_keys):
    tq = q_ref.shape[0]
    past = n_keys - tq
    blks = [slice(hh * LANES, (hh + 1) * LANES) for hh in range(2)]
    row = lax.broadcasted_iota(jnp.int32, (tq, tq), 0)
    col = lax.broadcasted_iota(jnp.int32, (tq, tq), 1)
    qs = [q_ref[:, blk] for blk in blks]
    s_diag = [jnp.where(col <= row, _dot(q, k_ref[0, blk, past:n_keys]), NEG_INF)
              for q, blk in zip(qs, blks)]
    m = [jnp.max(s, axis=-1, keepdims=True) for s in s_diag]
    if past:
        s_past = [_dot(q, k_ref[0, blk, 0:past]) for q, blk in zip(qs, blks)]
        m = [jnp.maximum(mi, jnp.max(s, axis=-1, keepdims=True)) for mi, s in zip(m, s_past)]
    acc = [_dot(jnp.exp2(s - mi).astype(BF16), v_ref[past:n_keys, blk])
           for s, mi, blk in zip(s_diag, m, blks)]
    if past:
        acc = [a + _dot(jnp.exp2(s - mi).astype(BF16), v_ref[0:past, blk])
               for a, s, mi, blk in zip(acc, s_past, m, blks)]
    return acc


def _attn_kernel(q_ref, k_ref, v_ref, o_ref):
    qi = pl.program_id(2)
    tq = q_ref.shape[0]
    lane = lax.broadcasted_iota(jnp.int32, (1, LANES), 1)
    for tile in range(v_ref.shape[0] // tq):
        @pl.when(qi == tile)
        def _():
            outs = []
            for hh, acc in enumerate(_attend_pair(q_ref, k_ref, v_ref, (tile + 1) * tq)):
                sum_lane = HEAD_DIM - _own_half_start(hh)
                l = jnp.sum(jnp.where(lane == sum_lane, acc, 0.0), axis=-1, keepdims=True)
                outs.append(acc / l)
            o_ref[...] = jnp.where(lane < HEAD_DIM, outs[0], outs[1]).astype(BF16)


def _attention(q, k, v, batch, seq_len):
    n = q.shape[0]
    pairs = N_HEADS // 2
    tq = ATTN_Q
    qt = seq_len // tq
    pair_block = lambda b, hp, qi: (b, hp)
    return pl.pallas_call(
        _attn_kernel,
        grid=(batch, pairs, qt),
        in_specs=[
            pl.BlockSpec((tq, 2 * LANES), lambda b, hp, qi: (b * qt + qi, hp)),
            pl.BlockSpec((1, 2 * LANES, seq_len), lambda b, hp, qi: (b, hp, 0)),
            pl.BlockSpec((seq_len, 2 * LANES), pair_block),
        ],
        out_specs=pl.BlockSpec((tq, LANES), lambda b, hp, qi: (b * qt + qi, hp)),
        out_shape=jax.ShapeDtypeStruct((n, pairs * LANES), BF16),
        compiler_params=pltpu.CompilerParams(
            dimension_semantics=("arbitrary", "arbitrary", "arbitrary"),
            vmem_limit_bytes=VMEM_LIMIT),
        name="attn",
    )(q, k, v)


def _route(logits):
    lane = lax.broadcasted_iota(jnp.int32, logits.shape, 1)
    big = jnp.int32(LANES)
    gl = jnp.where(lane < N_GROUPS, logits, NEG_INF)
    gmax = jnp.max(gl, axis=-1, keepdims=True)
    g_w = 1.0 / jnp.sum(jnp.exp(gl - gmax), axis=-1, keepdims=True)
    g_idx = jnp.min(jnp.where(gl == gmax, lane, big), axis=-1, keepdims=True)
    e_lo = ROUTER_LANE0 + EXPERTS_PER_GROUP * g_idx
    el = jnp.where((lane >= e_lo) & (lane < e_lo + EXPERTS_PER_GROUP), logits, NEG_INF)
    v1 = jnp.max(el, axis=-1, keepdims=True)
    i1 = jnp.min(jnp.where(el == v1, lane, big), axis=-1, keepdims=True)
    el2 = jnp.where(lane == i1, NEG_INF, el)
    v2 = jnp.max(el2, axis=-1, keepdims=True)
    i2 = jnp.min(jnp.where(el2 == v2, lane, big), axis=-1, keepdims=True)
    e2 = jnp.exp(v2 - v1)
    w1 = g_w / (1.0 + e2)
    w2 = g_w * e2 / (1.0 + e2)
    return (i1 - ROUTER_LANE0).astype(F32), (i2 - ROUTER_LANE0).astype(F32), w1, w2


EXT_W0 = 0
EXT_W1 = N_SPLIT
EXT_E0 = 2 * N_SPLIT


def _routing_record(e0, e1, w0, w1):
    lane = lax.broadcasted_iota(jnp.int32, (1, LANES), 1)
    rec = jnp.where(lane == EXT_E0, e0, jnp.where(lane == EXT_E0 + 1, e1, 0.0))
    for first, w in ((EXT_W0, w0), (EXT_W1, w1)):
        for k, piece in enumerate(_split3(w)):
            rec = jnp.where(lane == first + k, piece.astype(F32), rec)
    return rec.astype(BF16)


def _local_slots(e0, e1):
    t = e0.shape[0]
    lane = lax.broadcasted_iota(jnp.int32, (t, LANES), 1).astype(F32)
    oh0 = lane == e0
    oh1 = lane == e1
    picked = (oh0 | oh1).astype(BF16)
    cnt = jnp.sum(picked.astype(F32), axis=0, keepdims=True)
    units = jnp.floor((cnt + (SEG_ALIGN - 1)) * (1.0 / SEG_ALIGN))
    r128 = lax.broadcasted_iota(jnp.int32, (LANES, LANES), 0)
    c128 = lax.broadcasted_iota(jnp.int32, (LANES, LANES), 1)
    before = (r128 < c128).astype(BF16)
    lstart = SEG_ALIGN * _dot(jnp.broadcast_to(units, (8, LANES)).astype(BF16), before)[0:1, :]
    row = lax.broadcasted_iota(jnp.int32, (t, t), 0)
    col = lax.broadcasted_iota(jnp.int32, (t, t), 1)
    earlier = (col < row).astype(BF16)
    base = _dot(earlier, picked) + lstart
    slot0 = jnp.sum(jnp.where(oh0, base, 0.0), axis=-1, keepdims=True)
    slot1 = jnp.sum(jnp.where(oh1, base, 0.0), axis=-1, keepdims=True)
    return slot0, slot1, cnt


def _mix_kernel(x_ref, pool_ref, attn_ref, wo_ref, g_ref, wr_ref, br_ref,
                h_ref, m_ref, ext_ref, slots_ref, slots_t_ref, cnt_ref, logits_s):
    pool_dim = pool_ref.shape[1]

    @pl.when(pl.program_id(0) == 0)
    def _():
        logits_s[...] = jnp.zeros_like(logits_s)

    prev_logits = logits_s[...]
    h = x_ref[...] + (_dot(pool_ref[...], wo_ref[0:pool_dim, :])
                      + _dot(attn_ref[...], wo_ref[pool_dim:, :]))
    h_ref[...] = h
    e0, e1, w0, w1 = _route(prev_logits)
    ext_ref[...] = _routing_record(e0, e1, w0, w1)
    slot0, slot1, cnt = _local_slots(e0, e1)
    cnt_ref[0] = cnt
    lane = lax.broadcasted_iota(jnp.int32, (1, LANES), 1)
    slots = jnp.where(lane == 0, slot0, jnp.where(lane == 1, slot1, 0.0))
    slots_ref[...] = slots
    slots_t_ref[0] = slots.T[0:8, :]
    m = _rms_norm(h, g_ref[...])
    m_ref[...] = m.astype(BF16)
    logits_s[...] = _dot_precise(m, wr_ref[...]) + br_ref[...]


def _mix(x2, pool, attn, w_out, g_ffn, w_router, b_router):
    n, d = x2.shape
    tm = ROW_TILE
    n_tiles = n // tm
    row = lambda i: (jnp.minimum(i, n_tiles - 1), 0)
    routed = lambda i: (jnp.maximum(i - 1, 0), 0)
    full2 = lambda i: (0, 0)
    return pl.pallas_call(
        _mix_kernel,
        grid=(n_tiles + 1,),
        in_specs=[
            pl.BlockSpec((tm, d), row),
            pl.BlockSpec((tm, pool.shape[1]), row),
            pl.BlockSpec((tm, attn.shape[1]), row),
            pl.BlockSpec(w_out.shape, full2),
            pl.BlockSpec((1, d), full2),
            pl.BlockSpec(w_router.shape, full2),
            pl.BlockSpec((1, LANES), full2),
        ],
        out_specs=[
            pl.BlockSpec((tm, d), row),
            pl.BlockSpec((tm, d), row),
            pl.BlockSpec((tm, LANES), routed),
            pl.BlockSpec((tm, LANES), routed),
            pl.BlockSpec((1, 8, tm), lambda i: (jnp.maximum(i - 1, 0), 0, 0)),
            pl.BlockSpec((1, 1, LANES), lambda i: (jnp.maximum(i - 1, 0), 0, 0)),
        ],
        out_shape=[
            jax.ShapeDtypeStruct((n, d), F32),
            jax.ShapeDtypeStruct((n, d), BF16),
            jax.ShapeDtypeStruct((n, LANES), BF16),
            jax.ShapeDtypeStruct((n, LANES), F32),
            jax.ShapeDtypeStruct((n_tiles, 8, tm), F32),
            jax.ShapeDtypeStruct((n_tiles, 1, LANES), F32),
        ],
        scratch_shapes=[pltpu.VMEM((tm, LANES), F32)],
        compiler_params=pltpu.CompilerParams(
            dimension_semantics=("arbitrary",), vmem_limit_bytes=VMEM_LIMIT),
        name="mix",
    )(x2, pool, attn, w_out, g_ffn, w_router, b_router)


SEG_ALIGN = 16
COPY_UNITS = 2
COPY_ROWS = COPY_UNITS * SEG_ALIGN
EXPERT_TILE = 768
LOCAL_ROWS = 2 * ROW_TILE + N_EXPERTS * SEG_ALIGN


TRIM_ROWS = 128
TRIM_OPTIONS = 3


def _used_row_options():
    return [LOCAL_ROWS - k * TRIM_ROWS for k in reversed(range(TRIM_OPTIONS))]


def _round_up_to_option(rows, options):
    out = jnp.int32(options[-1])
    for opt in reversed(options[:-1]):
        out = jnp.where(rows <= opt, jnp.int32(opt), out)
    return out


def _permutation(slot0, slot1, slots_axis, n_slots=None):
    n_slots = LOCAL_ROWS if n_slots is None else n_slots
    shape = (1, n_slots) if slots_axis == 1 else (n_slots, 1)
    s = lax.broadcasted_iota(jnp.int32, shape, slots_axis)
    return ((s == slot0.astype(jnp.int32)) | (s == slot1.astype(jnp.int32))).astype(BF16)


def _copy_classes():
    classes = [(COPY_UNITS, LOCAL_ROWS // COPY_ROWS)]
    units = COPY_UNITS // 2
    while units >= 1:
        classes.append((units, N_EXPERTS))
        units //= 2
    return classes


def _for_each_piece(list_refs, tile, make_copy, act):
    for (units, k_max), (local_ref, hbm_ref, count_ref) in zip(_copy_classes(), list_refs):
        def body(k, carry, units=units, k_max=k_max, local_ref=local_ref, hbm_ref=hbm_ref):
            idx = tile * k_max + k
            act(make_copy(pl.multiple_of(local_ref[idx] * SEG_ALIGN, SEG_ALIGN),
                          pl.multiple_of(hbm_ref[idx] * SEG_ALIGN, SEG_ALIGN),
                          units * SEG_ALIGN))
            return carry

        lax.fori_loop(0, count_ref[tile], body, 0)


def _group_lists(refs):
    n = len(_copy_classes())
    return [tuple(refs[3 * c:3 * c + 3]) for c in range(n)], refs[3 * n], refs[3 * n + 1:]


def _wait_rows(total_units, make_copy):
    for b in range((LOCAL_ROWS // SEG_ALIGN).bit_length()):
        @pl.when(((total_units >> b) & 1) == 1)
        def _():
            make_copy(0, 0, SEG_ALIGN << b).wait()


def _start(copy):
    copy.start()


def _wait(copy):
    copy.wait()


def _dispatch_kernel(n_steps, *refs):
    copy_lists, tot_ref, rest = _group_lists(refs)
    (tail0_ref, tailn_ref, nu_ref, m_ref, ext_ref, st_ref, st_next_ref, xs_hbm,
     buf, zeros, perm_even, perm_odd, sems, tail_sem) = rest
    i = pl.program_id(0)
    slot = i % 2
    d = m_ref.shape[1]
    n_expert_tiles = xs_hbm.shape[0] // EXPERT_TILE

    def seg_copy(slot_):
        def make(l, g, rows):
            return pltpu.make_async_copy(buf.at[slot_, pl.ds(l, rows), :],
                                         xs_hbm.at[pl.ds(g, rows), :], sems.at[slot_])
        return make

    def for_each_tail(act):
        @pl.when(i < N_EXPERTS)
        def _():
            e = jnp.minimum(i, N_EXPERTS - 1)
            g0 = tail0_ref[e] * SEG_ALIGN
            n = tailn_ref[e]
            for b in range((EXPERT_TILE // SEG_ALIGN - 1).bit_length()):
                @pl.when(((n >> b) & 1) == 1)
                def _():
                    rows = SEG_ALIGN << b
                    higher = (n >> (b + 1)) << (b + 1)
                    g = pl.multiple_of(g0 + higher * SEG_ALIGN, SEG_ALIGN)
                    act(pltpu.make_async_copy(zeros.at[pl.ds(0, rows), :],
                                              xs_hbm.at[pl.ds(g, rows), :], tail_sem))

        for k in range(-(-n_expert_tiles // n_steps)):
            t = nu_ref[0] + i + k * n_steps

            @pl.when(t < n_expert_tiles)
            def _():
                g = pl.multiple_of(t * EXPERT_TILE, EXPERT_TILE)
                act(pltpu.make_async_copy(zeros, xs_hbm.at[pl.ds(g, EXPERT_TILE), :], tail_sem))

    @pl.when(i == 0)
    def _():
        zeros[...] = jnp.zeros_like(zeros)

    for_each_tail(_start)

    def build(slots_t, perm):
        perm[...] = _permutation(slots_t[0, 0:1, :], slots_t[0, 1:2, :], slots_axis=0)

    @pl.when(i == 0)
    def _():
        build(st_ref, perm_even)

    def permute(perm, next_perm, rows):
        p = perm[0:rows, :]
        build(st_next_ref, next_perm)
        buf[slot, 0:rows, 0:d] = _dot(p, m_ref[...]).astype(BF16)
        buf[slot, 0:rows, d:] = _dot(p, ext_ref[...]).astype(BF16)

    row_options = _used_row_options()
    used = _round_up_to_option(tot_ref[i] * SEG_ALIGN, row_options)
    for parity, (perm, next_perm) in enumerate(((perm_even, perm_odd), (perm_odd, perm_even))):
        for rows in row_options:
            @pl.when((slot == parity) & (used == rows))
            def _():
                permute(perm, next_perm, rows)

    _for_each_piece(copy_lists, i, seg_copy(slot), _start)

    @pl.when(i > 0)
    def _():
        _wait_rows(tot_ref[i - 1], seg_copy(1 - slot))

    for_each_tail(_wait)

    @pl.when(i == n_steps - 1)
    def _():
        _wait_rows(tot_ref[i], seg_copy(slot))


def _dispatch(m, ext, slots_t, tables, n_rows):
    n, d = m.shape
    tm = ROW_TILE
    assert n // tm >= N_EXPERTS, "each grid step zero-fills the tail of one expert"
    width = d + LANES
    last = n // tm - 1
    row = lambda i, *_: (i, 0)
    return pl.pallas_call(
        functools.partial(_dispatch_kernel, n // tm),
        grid_spec=pltpu.PrefetchScalarGridSpec(
            num_scalar_prefetch=len(tables),
            grid=(n // tm,),
            in_specs=[
                pl.BlockSpec((tm, d), row),
                pl.BlockSpec((tm, LANES), row),
                pl.BlockSpec((1, 8, tm), lambda i, *_: (i, 0, 0)),
                pl.BlockSpec((1, 8, tm), lambda i, *_: (jnp.minimum(i + 1, last), 0, 0)),
            ],
            out_specs=pl.BlockSpec(memory_space=pl.ANY),
            scratch_shapes=[
                pltpu.VMEM((2, LOCAL_ROWS, width), BF16),
                pltpu.VMEM((EXPERT_TILE, width), BF16),
                pltpu.VMEM((LOCAL_ROWS, tm), BF16),
                pltpu.VMEM((LOCAL_ROWS, tm), BF16),
                pltpu.SemaphoreType.DMA((2,)),
                pltpu.SemaphoreType.DMA(()),
            ],
        ),
        out_shape=jax.ShapeDtypeStruct((n_rows, width), BF16),
        compiler_params=pltpu.CompilerParams(
            dimension_semantics=("arbitrary",), vmem_limit_bytes=VMEM_LIMIT),
        name="dispatch",
    )(*tables, m, ext, slots_t, slots_t)


PART_ROWS = 64


def _expert_kernel(first_ref, count_ref, valid_ref, nu_ref, wg_ref, wu_ref, wd_ref, xs_hbm, os_hbm,
                   xbuf, obuf, zeros, wgu_s, wd_s, in_sems, out_sems, zero_sem):
    e = pl.program_id(0)
    d = os_hbm.shape[1]
    f = wg_ref.shape[2]
    n = count_ref[e]
    t0 = first_ref[e]
    n_tiles = os_hbm.shape[0] // EXPERT_TILE

    def rows(t):
        return pl.ds(pl.multiple_of(t * EXPERT_TILE, EXPERT_TILE), EXPERT_TILE)

    def in_copy(k, slot, first=t0):
        return pltpu.make_async_copy(xs_hbm.at[rows(first + k), :], xbuf.at[slot],
                                     in_sems.at[slot])

    def out_copy(k, slot):
        return pltpu.make_async_copy(obuf.at[slot], os_hbm.at[rows(t0 + k), :], out_sems.at[slot])

    def for_each_unused(act):
        for j in range(-(-n_tiles // N_EXPERTS)):
            t = nu_ref[0] + e + j * N_EXPERTS

            @pl.when(t < n_tiles)
            def _():
                act(pltpu.make_async_copy(zeros, os_hbm.at[rows(t), :], zero_sem))

    @pl.when(e == 0)
    def _():
        zeros[...] = jnp.zeros_like(zeros)

    for_each_unused(_start)

    @pl.when((e == 0) & (n > 0))
    def _():
        in_copy(0, 0).start()

    @pl.when(n > 0)
    def _():
        wgu_s[:, 0:f] = wg_ref[0].astype(BF16)
        wgu_s[:, f:] = wu_ref[0].astype(BF16)
        wd_s[...] = wd_ref[0].astype(BF16)

    lane = lax.broadcasted_iota(jnp.int32, (1, LANES), 1)

    def ffn(slot, m_rows):
        x = xbuf[slot, 0:m_rows, 0:d]
        rec = xbuf[slot, 0:m_rows, d:].astype(F32)

        def lanes_sum(first, count):
            keep = (lane >= first) & (lane < first + count)
            return jnp.sum(jnp.where(keep, rec, 0.0), axis=-1, keepdims=True)

        first_choice = lanes_sum(EXT_E0, 1) == e.astype(F32)
        w = jnp.where(first_choice, lanes_sum(EXT_W0, N_SPLIT), lanes_sum(EXT_W1, N_SPLIT))
        h = _dot(x, wgu_s[...])
        hg = h[:, 0:f]
        hu = h[:, f:]
        a = hg * jax.nn.sigmoid(hg) * hu * w
        obuf[slot, 0:m_rows, :] = _dot(a.astype(BF16), wd_s[...]).astype(BF16)
        if m_rows < EXPERT_TILE:
            obuf[slot, m_rows:, :] = jnp.zeros((EXPERT_TILE - m_rows, d), BF16)

    def tile(k, carry):
        slot = k % 2

        @pl.when(k + 1 < n)
        def _():
            in_copy(k + 1, 1 - slot).start()

        in_copy(k, slot).wait()

        @pl.when(k >= 2)
        def _():
            out_copy(k - 2, slot).wait()

        parts_used = jnp.minimum(
            (valid_ref[e] - k * EXPERT_TILE + PART_ROWS - 1) // PART_ROWS, EXPERT_TILE // PART_ROWS)
        for parts in range(1, EXPERT_TILE // PART_ROWS + 1):
            @pl.when(parts_used == parts)
            def _():
                ffn(slot, parts * PART_ROWS)

        out_copy(k, slot).start()
        return carry

    lax.fori_loop(0, n, tile, 0)

    for back in (2, 1):
        @pl.when(n >= back)
        def _():
            out_copy(n - back, (n - back) % 2).wait()

    nxt = jnp.minimum(e + 1, N_EXPERTS - 1)

    @pl.when((e + 1 < N_EXPERTS) & (count_ref[nxt] > 0))
    def _():
        in_copy(0, 0, first_ref[nxt]).start()

    for_each_unused(_wait)


def _experts(xs, first_tile, tile_count, valid_rows, n_used, w_gate, w_up, w_down):
    n_rows, width = xs.shape
    n_exp, d, f = w_gate.shape
    te = EXPERT_TILE
    weight = lambda e, *_: (e, 0, 0)
    return pl.pallas_call(
        _expert_kernel,
        grid_spec=pltpu.PrefetchScalarGridSpec(
            num_scalar_prefetch=4,
            grid=(n_exp,),
            in_specs=[
                pl.BlockSpec((1, d, f), weight),
                pl.BlockSpec((1, d, f), weight),
                pl.BlockSpec((1, f, d), weight),
                pl.BlockSpec(memory_space=pl.ANY),
            ],
            out_specs=pl.BlockSpec(memory_space=pl.ANY),
            scratch_shapes=[
                pltpu.VMEM((2, te, width), BF16),
                pltpu.VMEM((2, te, d), BF16),
                pltpu.VMEM((te, d), BF16),
                pltpu.VMEM((d, 2 * f), BF16), pltpu.VMEM((f, d), BF16),
                pltpu.SemaphoreType.DMA((2,)),
                pltpu.SemaphoreType.DMA((2,)),
                pltpu.SemaphoreType.DMA(()),
            ],
        ),
        out_shape=jax.ShapeDtypeStruct((n_rows, d), BF16),
        compiler_params=pltpu.CompilerParams(
            dimension_semantics=("arbitrary",), vmem_limit_bytes=VMEM_LIMIT),
        name="experts",
    )(first_tile, tile_count, valid_rows, n_used, w_gate, w_up, w_down, xs)


def _compact(valid, local, hbm, k_max):
    pos = jnp.cumsum(valid, axis=1) - valid
    k = jnp.arange(k_max, dtype=jnp.int32)[None, :, None]
    hit = (pos[:, None, :] == k) & (valid[:, None, :] == 1)
    pick = lambda a: jnp.sum(jnp.where(hit, a[:, None, :], 0), axis=2).reshape(-1)
    return pick(local), pick(hbm), jnp.sum(valid, axis=1)


def _copy_lists(n_units, lrow, grow):
    tables = ()
    for units, k_max in _copy_classes():
        if units == COPY_UNITS:
            per_seg = ROW_TILE // COPY_ROWS
            c = jnp.arange(per_seg, dtype=jnp.int32)[None, None, :]
            valid = (c < (n_units // COPY_UNITS)[:, :, None]).astype(jnp.int32)
            off = jnp.broadcast_to(c * COPY_UNITS, valid.shape)
        else:
            valid = ((n_units // units) % 2)[:, :, None]
            off = ((n_units // (2 * units)) * (2 * units))[:, :, None]
        flat = lambda a: a.reshape(a.shape[0], -1)
        tables += _compact(flat(valid), flat(lrow[:, :, None] + off), flat(grow[:, :, None] + off),
                           k_max)
    return tables


def _moe_layout(cnt, n_pairs):
    c = cnt[:, 0, :N_EXPERTS].astype(jnp.int32)
    n_tiles = c.shape[0]
    n8 = (c + SEG_ALIGN - 1) // SEG_ALIGN
    lrow = jnp.cumsum(n8, axis=1) - n8
    units_e = jnp.sum(n8, axis=0)
    per_tile = EXPERT_TILE // SEG_ALIGN
    tiles_e = (units_e + per_tile - 1) // per_tile
    e_end = jnp.cumsum(tiles_e)
    e_off = (e_end - tiles_e) * per_tile
    grow = e_off[None, :] + jnp.cumsum(n8, axis=0) - n8
    tail0 = e_off + units_e
    tailn = tiles_e * per_tile - units_e
    worst = n_pairs + n_tiles * N_EXPERTS * (SEG_ALIGN - 1) + N_EXPERTS * (EXPERT_TILE - SEG_ALIGN)
    n_rows = -(-worst // EXPERT_TILE) * EXPERT_TILE
    seg_tables = _copy_lists(n8, lrow, grow) + (jnp.sum(n8, axis=1),)
    expert_tables = (e_end - tiles_e, tiles_e, units_e * SEG_ALIGN, e_end[-1:])
    return seg_tables, (tail0, tailn), expert_tables, n_rows


def _combine_kernel(*refs):
    copy_lists, tot_ref, rest = _group_lists(refs)
    h_ref, slots_ref, p_ref, g_ref, wg_ref, wp_ref, gf_ref, os_hbm, o_ref, buf, sems = rest
    i = pl.program_id(0)
    nt = pl.num_programs(0)
    slot = i % 2

    def seg_copy(slot_):
        def make(l, g, rows):
            return pltpu.make_async_copy(os_hbm.at[pl.ds(g, rows), :],
                                         buf.at[slot_, pl.ds(l, rows), :], sems.at[slot_])
        return make

    @pl.when(i == 0)
    def _():
        buf[...] = jnp.zeros_like(buf)
        _for_each_piece(copy_lists, 0, seg_copy(0), _start)

    @pl.when(i + 1 < nt)
    def _():
        _for_each_piece(copy_lists, i + 1, seg_copy(1 - slot), _start)

    _wait_rows(tot_ref[i], seg_copy(slot))

    def finish(rows):
        slots = slots_ref[...]
        perm = _permutation(slots[:, 0:1], slots[:, 1:2], slots_axis=1, n_slots=rows)
        h = h_ref[...] + _dot(perm, buf[slot, 0:rows, :])
        gate = jax.nn.sigmoid(_dot(_rms_norm(h, g_ref[...]).astype(BF16), wg_ref[...]))
        h = h + gate * _dot(p_ref[...].astype(BF16), wp_ref[...])
        o_ref[...] = _rms_norm(h, gf_ref[...])

    row_options = _used_row_options()
    used = _round_up_to_option(tot_ref[i] * SEG_ALIGN, row_options)
    for rows in row_options:
        @pl.when(used == rows)
        def _():
            finish(rows)


def _combine(h1, slots, seg_tables, o_sorted, p2, g_ple, w_gate, w_proj, g_final):
    n, d = h1.shape
    tm = ROW_TILE
    row = lambda i, *_: (i, 0)
    full2 = lambda i, *_: (0, 0)
    return pl.pallas_call(
        _combine_kernel,
        grid_spec=pltpu.PrefetchScalarGridSpec(
            num_scalar_prefetch=len(seg_tables),
            grid=(n // tm,),
            in_specs=[
                pl.BlockSpec((tm, d), row),
                pl.BlockSpec((tm, LANES), row),
                pl.BlockSpec((tm, p2.shape[1]), row),
                pl.BlockSpec((1, d), full2),
                pl.BlockSpec(w_gate.shape, full2),
                pl.BlockSpec(w_proj.shape, full2),
                pl.BlockSpec((1, d), full2),
                pl.BlockSpec(memory_space=pl.ANY),
            ],
            out_specs=pl.BlockSpec((tm, d), row),
            scratch_shapes=[
                pltpu.VMEM((2, LOCAL_ROWS, d), BF16),
                pltpu.SemaphoreType.DMA((2,)),
            ],
        ),
        out_shape=jax.ShapeDtypeStruct((n, d), F32),
        compiler_params=pltpu.CompilerParams(
            dimension_semantics=("arbitrary",), vmem_limit_bytes=VMEM_LIMIT),
        name="combine",
    )(*seg_tables, h1, slots, p2, g_ple, w_gate, w_proj, g_final, o_sorted)


def _pad_lanes(a):
    return jnp.pad(a, ((0, 0), (0, LANES - a.shape[1])))


def kernel(x, p, g_mix, w_in, b_f, w_pool, s_pool, w_out, g_ffn, w_grp, b_grp, w_rt, b_rt,
           w_e_gate, w_e_up, w_e_down, g_ple, w_ple_gate, w_ple_proj, g_final):
    batch, seq_len, d = x.shape
    n = batch * seq_len
    assert w_in.shape[0] == 1, "single-layer stack only: the final norm is fused into the layer"
    i = 0
    pool_dim = s_pool.shape[1]
    attn_dim = N_HEADS * HEAD_DIM
    main = pool_dim + 3 * attn_dim
    h = x.reshape(n, d)
    w_f = _pad_lanes(jnp.repeat(w_in[i, :, main:], N_SPLIT, axis=1))
    w_main = jnp.concatenate([w_in[i, :, :main], w_f], axis=1).astype(BF16)
    b_f3 = _pad_lanes(jnp.repeat(b_f[i], N_SPLIT)[None])
    pool, q, k, v = _inproj(h, g_mix[i][None], w_main, b_f3,
                            w_pool[i].astype(BF16), s_pool[i][None], seq_len)
    attn = _attention(q, k, v, batch, seq_len)
    w_router = _pad_lanes(jnp.concatenate([w_grp[i], w_rt[i]], axis=1))
    b_router = _pad_lanes(jnp.concatenate([b_grp[i], b_rt[i]])[None])
    h1, m, ext, slots, slots_t, cnt = _mix(h, pool, attn, w_out[i].astype(BF16), g_ffn[i][None],
                                          w_router, b_router)
    seg_tables, tail_tables, expert_tables, n_rows = _moe_layout(cnt, 2 * n)
    x_sorted = _dispatch(m, ext, slots_t, seg_tables + tail_tables + expert_tables[3:], n_rows)
    f = w_e_gate.shape[-1]
    o_sorted = _experts(x_sorted, *expert_tables,
                        w_e_gate[i].reshape(N_EXPERTS, d, f),
                        w_e_up[i].reshape(N_EXPERTS, d, f),
                        w_e_down[i].reshape(N_EXPERTS, f, d))
    out = _combine(h1, slots, seg_tables, o_sorted, p[i].reshape(n, -1), g_ple[i][None],
                   w_ple_gate[i].astype(BF16), w_ple_proj[i].astype(BF16), g_final[None])
    return out.reshape(batch, seq_len, d)
```

```python
import functools
import math

import jax
import jax.numpy as jnp
from jax import lax
from jax.experimental import pallas as pl
from jax.experimental.pallas import tpu as pltpu

HEAD_DIM = 64
N_HEADS = 8
POOL_WINDOWS = (2, 4, 8, 16)
POOL_GROUP_DIM = 128
POOL_HISTORY = 16
N_GROUPS = 4
EXPERTS_PER_GROUP = 8
N_EXPERTS = N_GROUPS * EXPERTS_PER_GROUP
EPS = 1e-6
LANES = 128
ROUTER_LANE0 = N_GROUPS
NEG_INF = float("-inf")
LOG2E = math.log2(math.e)
N_SPLIT = 3

ROW_TILE = 512
ATTN_Q = 512
VMEM_LIMIT = 48 * 1024 * 1024

BF16 = jnp.bfloat16
F32 = jnp.float32


def _dot(a, b):
    return jnp.dot(a, b, preferred_element_type=F32)


def _split2(a):
    hi = a.astype(BF16)
    lo = (a - hi.astype(F32)).astype(BF16)
    return hi, lo


def _split3(a):
    hi = a.astype(BF16)
    r = a - hi.astype(F32)
    mid = r.astype(BF16)
    lo = (r - mid.astype(F32)).astype(BF16)
    return hi, mid, lo


def _dot_precise(a, w):
    a1, a2 = _split2(a)
    w1, w2 = _split2(w)
    n = w.shape[1]
    r = _dot(a1, jnp.concatenate([w1, w2], axis=1))
    return r[:, 0:n] + (r[:, n:] + _dot(a2, w1))


def _rms_norm(x, g):
    return x * lax.rsqrt(jnp.mean(x * x, axis=-1, keepdims=True) + EPS) * g


def _own_half_start(head):
    return 0 if head % 2 == 0 else HEAD_DIM


def _inproj_kernel(tiles_per_seq, n_tiles, x_hbm, g_ref, w_ref, bf_ref, wp_ref, sp_ref,
                   pool_ref, q_ref, k_ref, v_ref,
                   carry_c, carry_u, u_s, q_s, k_s, v_s, c_s, xbuf, x_sems):
    i = pl.program_id(0)
    tm = xbuf.shape[1]
    pool_dim = pool_ref.shape[1]
    attn_dim = N_HEADS * HEAD_DIM

    @pl.when(i == 0)
    def _():
        for ref in (u_s, q_s, k_s, v_s, c_s, carry_u):
            ref[...] = jnp.zeros_like(ref)

    @pl.when(i % tiles_per_seq == 0)
    def _():
        carry_c[...] = jnp.zeros_like(carry_c)

    prev_seq_tile = (i + tiles_per_seq - 1) % tiles_per_seq

    @pl.when(prev_seq_tile == 0)
    def _():
        carry_u[...] = jnp.zeros_like(carry_u)

    u_prev, c_prev = u_s[...], c_s[...]
    q_prev, k_prev, v_prev = q_s[...], k_s[...], v_s[...]

    a = _rms_norm(xbuf[_ring_fetch(x_hbm, xbuf, x_sems, i, n_tiles)], g_ref[...])
    ab = a.astype(BF16)
    o0 = pool_dim
    u = _dot(ab, w_ref[:, 0:o0])
    qf = _dot(ab, w_ref[:, o0:o0 + attn_dim]) * (LOG2E * HEAD_DIM ** -0.5)
    kf = _dot(ab, w_ref[:, o0 + attn_dim:o0 + 2 * attn_dim])
    vf = _dot(ab, w_ref[:, o0 + 2 * attn_dim:o0 + 3 * attn_dim])
    fl = _dot(ab, w_ref[:, o0 + 3 * attn_dim:]) + bf_ref[...]

    c1, c2, c3 = _split3(c_prev * LOG2E)
    lane = lax.broadcasted_iota(jnp.int32, (1, LANES), 1)
    piece_id = lane % N_SPLIT
    pieces = jnp.where(piece_id == 0, c1, jnp.where(piece_id == 1, c2, c3)).astype(F32)
    for h in range(N_HEADS):
        own = (lane >= _own_half_start(h)) & (lane < _own_half_start(h) + HEAD_DIM)
        pair = slice((h // 2) * LANES, (h // 2 + 1) * LANES)
        blk = slice(h * LANES, (h + 1) * LANES)
        spare = HEAD_DIM - _own_half_start(h)
        first = (lane >= spare) & (lane < spare + N_SPLIT)
        second = (lane >= spare + N_SPLIT) & (lane < spare + 2 * N_SPLIT)
        to_first = pltpu.roll(pieces, (spare - h * N_SPLIT) % LANES, axis=1)
        to_second = pltpu.roll(pieces, (spare + N_SPLIT - h * N_SPLIT) % LANES, axis=1)
        bias_q = jnp.where(first, to_first, jnp.where(second, 1.0, 0.0))
        bias_k = jnp.where(second, -to_second, jnp.where(first, 1.0, 0.0))
        q_ref[:, blk] = jnp.where(own, q_prev[:, pair], bias_q).astype(BF16)
        k_ref[0, blk, :] = jnp.where(own, k_prev[:, pair], bias_k).T.astype(BF16)
        one_col = (lane == HEAD_DIM - _own_half_start(h)).astype(F32)
        v_ref[:, blk] = jnp.where(own, v_prev[:, pair], one_col).astype(BF16)

    ext = jnp.concatenate([carry_u[...], u_prev], axis=0)
    carry_u[...] = u_prev[tm - POOL_HISTORY:, :]
    pos = (prev_seq_tile * tm + 1 + lax.broadcasted_iota(jnp.int32, (tm, 1), 0)).astype(F32)
    for gi, w in enumerate(POOL_WINDOWS):
        lo, hi = gi * POOL_GROUP_DIM, (gi + 1) * POOL_GROUP_DIM
        s = ext[:, lo:hi]
        shift = 1
        while shift < w:
            s = s + pltpu.roll(s, shift, axis=0)
            shift *= 2
        mean = s[POOL_HISTORY:, :] / jnp.minimum(pos, float(w))
        d = mean - u_prev[:, lo:hi]
        y = _dot(d.astype(BF16), wp_ref[gi]) * sp_ref[:, lo:hi]
        pool_ref[:, lo:hi] = y.astype(BF16)

    lf = jnp.minimum(fl, 0.0) - jnp.log1p(jnp.exp(-jnp.abs(fl)))
    row = lax.broadcasted_iota(jnp.int32, (tm, tm), 0)
    col = lax.broadcasted_iota(jnp.int32, (tm, tm), 1)
    tril = (col <= row).astype(BF16)
    sums = _dot(tril, jnp.concatenate(_split3(lf), axis=1))
    c = carry_c[...] + (sums[:, 0:LANES] + (sums[:, LANES:2 * LANES] + sums[:, 2 * LANES:]))
    carry_c[...] = c[tm - 1:tm, :]
    u_s[...] = u
    q_s[...] = qf
    k_s[...] = kf
    v_s[...] = vf
    c_s[...] = c


def _inproj(x2, g_mix, w_main, b_f, w_pool, s_pool, seq_len):
    n, d = x2.shape
    tm = ROW_TILE
    pool_dim = w_pool.shape[0] * w_pool.shape[1]
    head_w = N_HEADS * LANES
    attn_dim = N_HEADS * HEAD_DIM
    n_tiles = n // tm
    finished = lambda i: (jnp.maximum(i - 1, 0), 0)
    tiles_per_seq = seq_len // tm

    def finished_keys(i):
        j = jnp.maximum(i - 1, 0)
        return (j // tiles_per_seq, 0, j % tiles_per_seq)

    full2 = lambda i: (0, 0)
    return pl.pallas_call(
        functools.partial(_inproj_kernel, seq_len // tm, n_tiles),
        grid=(n_tiles + 1,),
        in_specs=[
            pl.BlockSpec(memory_space=pl.ANY),
            pl.BlockSpec((1, d), full2),
            pl.BlockSpec(w_main.shape, full2),
            pl.BlockSpec((1, LANES), full2),
            pl.BlockSpec(w_pool.shape, lambda i: (0, 0, 0)),
            pl.BlockSpec((1, pool_dim), full2),
        ],
        out_specs=[
            pl.BlockSpec((tm, pool_dim), finished),
            pl.BlockSpec((tm, head_w), finished),
            pl.BlockSpec((1, head_w, tm), finished_keys),
            pl.BlockSpec((tm, head_w), finished),
        ],
        out_shape=[
            jax.ShapeDtypeStruct((n, pool_dim), BF16),
            jax.ShapeDtypeStruct((n, head_w), BF16),
            jax.ShapeDtypeStruct((n // seq_len, head_w, seq_len), BF16),
            jax.ShapeDtypeStruct((n, head_w), BF16),
        ],
        scratch_shapes=[
            pltpu.VMEM((1, LANES), F32), pltpu.VMEM((POOL_HISTORY, pool_dim), F32),
            pltpu.VMEM((tm, pool_dim), F32), pltpu.VMEM((tm, attn_dim), F32),
            pltpu.VMEM((tm, attn_dim), F32), pltpu.VMEM((tm, attn_dim), F32),
            pltpu.VMEM((tm, LANES), F32),
            pltpu.VMEM((X_RING, tm, d), F32), pltpu.SemaphoreType.DMA((X_RING,)),
        ],
        compiler_params=pltpu.CompilerParams(
            dimension_semantics=("arbitrary",), vmem_limit_bytes=VMEM_LIMIT),
        name="inproj",
    )(x2, g_mix, w_main, b_f, w_pool, s_pool)


def _attend_pair(q_ref, k_ref, v_ref, n_keys):
    tq = q_ref.shape[0]
    past = n_keys - tq
    blks = [slice(hh * LANES, (hh + 1) * LANES) for hh in range(2)]
    row = lax.broadcasted_iota(jnp.int32, (tq, tq), 0)
    col = lax.broadcasted_iota(jnp.int32, (tq, tq), 1)
    qs = [q_ref[:, blk] for blk in blks]
    s_diag = [jnp.where(col <= row, _dot(q, k_ref[0, blk, past:n_keys]), NEG_INF)
              for q, blk in zip(qs, blks)]
    m = [jnp.max(s, axis=-1, keepdims=True) for s in s_diag]
    if past:
        s_past = [_dot(q, k_ref[0, blk, 0:past]) for q, blk in zip(qs, blks)]
        m = [jnp.maximum(mi, jnp.max(s, axis=-1, keepdims=True)) for mi, s in zip(m, s_past)]
    acc = [_dot(jnp.exp2(s - mi).astype(BF16), v_ref[past:n_keys, blk])
           for s, mi, blk in zip(s_diag, m, blks)]
    if past:
        acc = [a + _dot(jnp.exp2(s - mi).astype(BF16), v_ref[0:past, blk])
               for a, s, mi, blk in zip(acc, s_past, m, blks)]
    return acc


def _attn_kernel(q_ref, k_ref, v_ref, o_ref):
    qi = pl.program_id(2)
    tq = q_ref.shape[0]
    lane = lax.broadcasted_iota(jnp.int32, (1, LANES), 1)
    for tile in range(v_ref.shape[0] // tq):
        @pl.when(qi == tile)
        def _():
            outs = []
            for hh, acc in enumerate(_attend_pair(q_ref, k_ref, v_ref, (tile + 1) * tq)):
                sum_lane = HEAD_DIM - _own_half_start(hh)
                l = jnp.sum(jnp.where(lane == sum_lane, acc, 0.0), axis=-1, keepdims=True)
                outs.append(acc / l)
            o_ref[...] = jnp.where(lane < HEAD_DIM, outs[0], outs[1]).astype(BF16)


def _attention(q, k, v, batch, seq_len):
    n = q.shape[0]
    pairs = N_HEADS // 2
    tq = ATTN_Q
    qt = seq_len // tq
    pair_block = lambda b, hp, qi: (b, hp)
    return pl.pallas_call(
        _attn_kernel,
        grid=(batch, pairs, qt),
        in_specs=[
            pl.BlockSpec((tq, 2 * LANES), lambda b, hp, qi: (b * qt + qi, hp)),
            pl.BlockSpec((1, 2 * LANES, seq_len), lambda b, hp, qi: (b, hp, 0)),
            pl.BlockSpec((seq_len, 2 * LANES), pair_block),
        ],
        out_specs=pl.BlockSpec((tq, LANES), lambda b, hp, qi: (b * qt + qi, hp)),
        out_shape=jax.ShapeDtypeStruct((n, pairs * LANES), BF16),
        compiler_params=pltpu.CompilerParams(
            dimension_semantics=("arbitrary", "arbitrary", "arbitrary"),
            vmem_limit_bytes=VMEM_LIMIT),
        name="attn",
    )(q, k, v)


def _route(logits):
    lane = lax.broadcasted_iota(jnp.int32, logits.shape, 1)
    big = jnp.int32(LANES)
    gl = jnp.where(lane < N_GROUPS, logits, NEG_INF)
    gmax = jnp.max(gl, axis=-1, keepdims=True)
    g_w = 1.0 / jnp.sum(jnp.exp(gl - gmax), axis=-1, keepdims=True)
    g_idx = jnp.min(jnp.where(gl == gmax, lane, big), axis=-1, keepdims=True)
    e_lo = ROUTER_LANE0 + EXPERTS_PER_GROUP * g_idx
    el = jnp.where((lane >= e_lo) & (lane < e_lo + EXPERTS_PER_GROUP), logits, NEG_INF)
    v1 = jnp.max(el, axis=-1, keepdims=True)
    i1 = jnp.min(jnp.where(el == v1, lane, big), axis=-1, keepdims=True)
    el2 = jnp.where(lane == i1, NEG_INF, el)
    v2 = jnp.max(el2, axis=-1, keepdims=True)
    i2 = jnp.min(jnp.where(el2 == v2, lane, big), axis=-1, keepdims=True)
    e2 = jnp.exp(v2 - v1)
    w1 = g_w / (1.0 + e2)
    w2 = g_w * e2 / (1.0 + e2)
    return (i1 - ROUTER_LANE0).astype(F32), (i2 - ROUTER_LANE0).astype(F32), w1, w2


EXT_W0 = 0
EXT_W1 = N_SPLIT
EXT_E0 = 2 * N_SPLIT


def _routing_record(e0, e1, w0, w1):
    lane = lax.broadcasted_iota(jnp.int32, (1, LANES), 1)
    rec = jnp.where(lane == EXT_E0, e0, jnp.where(lane == EXT_E0 + 1, e1, 0.0))
    for first, w in ((EXT_W0, w0), (EXT_W1, w1)):
        for k, piece in enumerate(_split3(w)):
            rec = jnp.where(lane == first + k, piece.astype(F32), rec)
    return rec.astype(BF16)


def _local_slots(e0, e1):
    t = e0.shape[0]
    lane = lax.broadcasted_iota(jnp.int32, (t, LANES), 1).astype(F32)
    oh0 = lane == e0
    oh1 = lane == e1
    picked = (oh0 | oh1).astype(BF16)
    cnt = jnp.sum(picked.astype(F32), axis=0, keepdims=True)
    units = jnp.floor((cnt + (SEG_ALIGN - 1)) * (1.0 / SEG_ALIGN))
    r128 = lax.broadcasted_iota(jnp.int32, (LANES, LANES), 0)
    c128 = lax.broadcasted_iota(jnp.int32, (LANES, LANES), 1)
    before = (r128 < c128).astype(BF16)
    lstart = SEG_ALIGN * _dot(jnp.broadcast_to(units, (8, LANES)).astype(BF16), before)[0:1, :]
    row = lax.broadcasted_iota(jnp.int32, (t, t), 0)
    col = lax.broadcasted_iota(jnp.int32, (t, t), 1)
    earlier = (col < row).astype(BF16)
    base = _dot(earlier, picked) + lstart
    slot0 = jnp.sum(jnp.where(oh0, base, 0.0), axis=-1, keepdims=True)
    slot1 = jnp.sum(jnp.where(oh1, base, 0.0), axis=-1, keepdims=True)
    return slot0, slot1, cnt


X_RING = 3


def _ring_fetch(hbm, buf, sems, i, n_tiles):
    tm = buf.shape[1]

    def copy(tile):
        slot = tile % X_RING
        rows = pl.ds(pl.multiple_of(tile * tm, tm), tm)
        return pltpu.make_async_copy(hbm.at[rows, :], buf.at[slot], sems.at[slot])

    @pl.when(i == 0)
    def _():
        for tile in range(X_RING - 1):
            copy(tile).start()

    @pl.when(i + X_RING - 1 < n_tiles)
    def _():
        copy(i + X_RING - 1).start()

    @pl.when(i < n_tiles)
    def _():
        copy(i).wait()

    return jnp.minimum(i, n_tiles - 1) % X_RING


def _mix_kernel(n_tiles, x_hbm, pool_ref, attn_ref, wo_ref, g_ref, wr_ref, br_ref,
                h_ref, m_ref, ext_ref, slots_ref, slots_t_ref, cnt_ref, logits_s, xbuf, x_sems):
    i = pl.program_id(0)
    pool_dim = pool_ref.shape[1]

    @pl.when(i == 0)
    def _():
        logits_s[...] = jnp.zeros_like(logits_s)

    prev_logits = logits_s[...]
    x = xbuf[_ring_fetch(x_hbm, xbuf, x_sems, i, n_tiles)]
    h = x + (_dot(pool_ref[...], wo_ref[0:pool_dim, :])
             + _dot(attn_ref[...], wo_ref[pool_dim:, :]))
    h_ref[...] = h
    e0, e1, w0, w1 = _route(prev_logits)
    ext_ref[...] = _routing_record(e0, e1, w0, w1)
    slot0, slot1, cnt = _local_slots(e0, e1)
    cnt_ref[0] = cnt
    lane = lax.broadcasted_iota(jnp.int32, (1, LANES), 1)
    slots = jnp.where(lane == 0, slot0, jnp.where(lane == 1, slot1, 0.0))
    slots_ref[...] = slots
    slots_t_ref[0] = slots.T[0:8, :]
    m = _rms_norm(h, g_ref[...])
    m_ref[...] = m.astype(BF16)
    logits_s[...] = _dot_precise(m, wr_ref[...]) + br_ref[...]


def _mix(x2, pool, attn, w_out, g_ffn, w_router, b_router):
    n, d = x2.shape
    tm = ROW_TILE
    n_tiles = n // tm
    row = lambda i: (jnp.minimum(i, n_tiles - 1), 0)
    routed = lambda i: (jnp.maximum(i - 1, 0), 0)
    full2 = lambda i: (0, 0)
    assert n_tiles >= X_RING
    return pl.pallas_call(
        functools.partial(_mix_kernel, n_tiles),
        grid=(n_tiles + 1,),
        in_specs=[
            pl.BlockSpec(memory_space=pl.ANY),
            pl.BlockSpec((tm, pool.shape[1]), row),
            pl.BlockSpec((tm, attn.shape[1]), row),
            pl.BlockSpec(w_out.shape, full2),
            pl.BlockSpec((1, d), full2),
            pl.BlockSpec(w_router.shape, full2),
            pl.BlockSpec((1, LANES), full2),
        ],
        out_specs=[
            pl.BlockSpec((tm, d), row),
            pl.BlockSpec((tm, d), row),
            pl.BlockSpec((tm, LANES), routed),
            pl.BlockSpec((tm, LANES), routed),
            pl.BlockSpec((1, 8, tm), lambda i: (jnp.maximum(i - 1, 0), 0, 0)),
            pl.BlockSpec((1, 1, LANES), lambda i: (jnp.maximum(i - 1, 0), 0, 0)),
        ],
        out_shape=[
            jax.ShapeDtypeStruct((n, d), F32),
            jax.ShapeDtypeStruct((n, d), BF16),
            jax.ShapeDtypeStruct((n, LANES), BF16),
            jax.ShapeDtypeStruct((n, LANES), F32),
            jax.ShapeDtypeStruct((n_tiles, 8, tm), F32),
            jax.ShapeDtypeStruct((n_tiles, 1, LANES), F32),
        ],
        scratch_shapes=[pltpu.VMEM((tm, LANES), F32), pltpu.VMEM((X_RING, tm, d), F32),
                        pltpu.SemaphoreType.DMA((X_RING,))],
        compiler_params=pltpu.CompilerParams(
            dimension_semantics=("arbitrary",), vmem_limit_bytes=VMEM_LIMIT),
        name="mix",
    )(x2, pool, attn, w_out, g_ffn, w_router, b_router)


SEG_ALIGN = 16
COPY_UNITS = 2
COPY_ROWS = COPY_UNITS * SEG_ALIGN
EXPERT_TILE = 768
LOCAL_ROWS = 2 * ROW_TILE + N_EXPERTS * SEG_ALIGN


TRIM_ROWS = 128
TRIM_OPTIONS = 3


def _used_row_options():
    return [LOCAL_ROWS - k * TRIM_ROWS for k in reversed(range(TRIM_OPTIONS))]


def _round_up_to_option(rows, options):
    out = jnp.int32(options[-1])
    for opt in reversed(options[:-1]):
        out = jnp.where(rows <= opt, jnp.int32(opt), out)
    return out


def _permutation(slot0, slot1, slots_axis, n_slots=None):
    n_slots = LOCAL_ROWS if n_slots is None else n_slots
    shape = (1, n_slots) if slots_axis == 1 else (n_slots, 1)
    s = lax.broadcasted_iota(jnp.int32, shape, slots_axis)
    return ((s == slot0.astype(jnp.int32)) | (s == slot1.astype(jnp.int32))).astype(BF16)


def _copy_classes():
    classes = [(COPY_UNITS, LOCAL_ROWS // COPY_ROWS)]
    units = COPY_UNITS // 2
    while units >= 1:
        classes.append((units, N_EXPERTS))
        units //= 2
    return classes


def _for_each_piece(list_refs, tile, make_copy, act):
    for (units, k_max), (local_ref, hbm_ref, count_ref) in zip(_copy_classes(), list_refs):
        def body(k, carry, units=units, k_max=k_max, local_ref=local_ref, hbm_ref=hbm_ref):
            idx = tile * k_max + k
            act(make_copy(pl.multiple_of(local_ref[idx] * SEG_ALIGN, SEG_ALIGN),
                          pl.multiple_of(hbm_ref[idx] * SEG_ALIGN, SEG_ALIGN),
                          units * SEG_ALIGN))
            return carry

        lax.fori_loop(0, count_ref[tile], body, 0)


def _group_lists(refs):
    n = len(_copy_classes())
    return [tuple(refs[3 * c:3 * c + 3]) for c in range(n)], refs[3 * n], refs[3 * n + 1:]


def _wait_rows(total_units, make_copy):
    for b in range((LOCAL_ROWS // SEG_ALIGN).bit_length()):
        @pl.when(((total_units >> b) & 1) == 1)
        def _():
            make_copy(0, 0, SEG_ALIGN << b).wait()


def _start(copy):
    copy.start()


def _wait(copy):
    copy.wait()


def _dispatch_kernel(n_steps, *refs):
    copy_lists, tot_ref, rest = _group_lists(refs)
    (tail0_ref, tailn_ref, nu_ref, m_ref, ext_ref, st_ref, st_next_ref, xs_hbm,
     buf, zeros, perm_even, perm_odd, sems, tail_sem) = rest
    i = pl.program_id(0)
    slot = i % 2
    d = m_ref.shape[1]
    n_expert_tiles = xs_hbm.shape[0] // EXPERT_TILE

    def seg_copy(slot_):
        def make(l, g, rows):
            return pltpu.make_async_copy(buf.at[slot_, pl.ds(l, rows), :],
                                         xs_hbm.at[pl.ds(g, rows), :], sems.at[slot_])
        return make

    def for_each_tail(act):
        @pl.when(i < N_EXPERTS)
        def _():
            e = jnp.minimum(i, N_EXPERTS - 1)
            g0 = tail0_ref[e] * SEG_ALIGN
            n = tailn_ref[e]
            for b in range((EXPERT_TILE // SEG_ALIGN - 1).bit_length()):
                @pl.when(((n >> b) & 1) == 1)
                def _():
                    rows = SEG_ALIGN << b
                    higher = (n >> (b + 1)) << (b + 1)
                    g = pl.multiple_of(g0 + higher * SEG_ALIGN, SEG_ALIGN)
                    act(pltpu.make_async_copy(zeros.at[pl.ds(0, rows), :],
                                              xs_hbm.at[pl.ds(g, rows), :], tail_sem))

        for k in range(-(-n_expert_tiles // n_steps)):
            t = nu_ref[0] + i + k * n_steps

            @pl.when(t < n_expert_tiles)
            def _():
                g = pl.multiple_of(t * EXPERT_TILE, EXPERT_TILE)
                act(pltpu.make_async_copy(zeros, xs_hbm.at[pl.ds(g, EXPERT_TILE), :], tail_sem))

    @pl.when(i == 0)
    def _():
        zeros[...] = jnp.zeros_like(zeros)

    for_each_tail(_start)

    def build(slots_t, perm):
        perm[...] = _permutation(slots_t[0, 0:1, :], slots_t[0, 1:2, :], slots_axis=0)

    @pl.when(i == 0)
    def _():
        build(st_ref, perm_even)

    def permute(perm, next_perm, rows):
        p = perm[0:rows, :]
        build(st_next_ref, next_perm)
        buf[slot, 0:rows, 0:d] = _dot(p, m_ref[...]).astype(BF16)
        buf[slot, 0:rows, d:] = _dot(p, ext_ref[...]).astype(BF16)

    row_options = _used_row_options()
    used = _round_up_to_option(tot_ref[i] * SEG_ALIGN, row_options)
    for parity, (perm, next_perm) in enumerate(((perm_even, perm_odd), (perm_odd, perm_even))):
        for rows in row_options:
            @pl.when((slot == parity) & (used == rows))
            def _():
                permute(perm, next_perm, rows)

    _for_each_piece(copy_lists, i, seg_copy(slot), _start)

    @pl.when(i > 0)
    def _():
        _wait_rows(tot_ref[i - 1], seg_copy(1 - slot))

    for_each_tail(_wait)

    @pl.when(i == n_steps - 1)
    def _():
        _wait_rows(tot_ref[i], seg_copy(slot))


def _dispatch(m, ext, slots_t, tables, n_rows):
    n, d = m.shape
    tm = ROW_TILE
    assert n // tm >= N_EXPERTS, "each grid step zero-fills the tail of one expert"
    width = d + LANES
    last = n // tm - 1
    row = lambda i, *_: (i, 0)
    return pl.pallas_call(
        functools.partial(_dispatch_kernel, n // tm),
        grid_spec=pltpu.PrefetchScalarGridSpec(
            num_scalar_prefetch=len(tables),
            grid=(n // tm,),
            in_specs=[
                pl.BlockSpec((tm, d), row),
                pl.BlockSpec((tm, LANES), row),
                pl.BlockSpec((1, 8, tm), lambda i, *_: (i, 0, 0)),
                pl.BlockSpec((1, 8, tm), lambda i, *_: (jnp.minimum(i + 1, last), 0, 0)),
            ],
            out_specs=pl.BlockSpec(memory_space=pl.ANY),
            scratch_shapes=[
                pltpu.VMEM((2, LOCAL_ROWS, width), BF16),
                pltpu.VMEM((EXPERT_TILE, width), BF16),
                pltpu.VMEM((LOCAL_ROWS, tm), BF16),
                pltpu.VMEM((LOCAL_ROWS, tm), BF16),
                pltpu.SemaphoreType.DMA((2,)),
                pltpu.SemaphoreType.DMA(()),
            ],
        ),
        out_shape=jax.ShapeDtypeStruct((n_rows, width), BF16),
        compiler_params=pltpu.CompilerParams(
            dimension_semantics=("arbitrary",), vmem_limit_bytes=VMEM_LIMIT),
        name="dispatch",
    )(*tables, m, ext, slots_t, slots_t)


PART_ROWS = 64


def _expert_kernel(first_ref, count_ref, valid_ref, nu_ref, wg_ref, wu_ref, wd_ref, xs_hbm, os_hbm,
                   xbuf, obuf, zeros, wgu_s, wd_s, in_sems, out_sems, zero_sem):
    e = pl.program_id(0)
    d = os_hbm.shape[1]
    f = wg_ref.shape[2]
    n = count_ref[e]
    t0 = first_ref[e]
    n_tiles = os_hbm.shape[0] // EXPERT_TILE

    def rows(t):
        return pl.ds(pl.multiple_of(t * EXPERT_TILE, EXPERT_TILE), EXPERT_TILE)

    def in_copy(k, slot, first=t0):
        return pltpu.make_async_copy(xs_hbm.at[rows(first + k), :], xbuf.at[slot],
                                     in_sems.at[slot])

    def out_copy(k, slot):
        return pltpu.make_async_copy(obuf.at[slot], os_hbm.at[rows(t0 + k), :], out_sems.at[slot])

    def for_each_unused(act):
        for j in range(-(-n_tiles // N_EXPERTS)):
            t = nu_ref[0] + e + j * N_EXPERTS

            @pl.when(t < n_tiles)
            def _():
                act(pltpu.make_async_copy(zeros, os_hbm.at[rows(t), :], zero_sem))

    @pl.when(e == 0)
    def _():
        zeros[...] = jnp.zeros_like(zeros)

    for_each_unused(_start)

    @pl.when((e == 0) & (n > 0))
    def _():
        in_copy(0, 0).start()

    @pl.when(n > 0)
    def _():
        wgu_s[:, 0:f] = wg_ref[0].astype(BF16)
        wgu_s[:, f:] = wu_ref[0].astype(BF16)
        wd_s[...] = wd_ref[0].astype(BF16)

    lane = lax.broadcasted_iota(jnp.int32, (1, LANES), 1)

    def ffn(slot, m_rows):
        x = xbuf[slot, 0:m_rows, 0:d]
        rec = xbuf[slot, 0:m_rows, d:].astype(F32)

        def lanes_sum(first, count):
            keep = (lane >= first) & (lane < first + count)
            return jnp.sum(jnp.where(keep, rec, 0.0), axis=-1, keepdims=True)

        first_choice = lanes_sum(EXT_E0, 1) == e.astype(F32)
        w = jnp.where(first_choice, lanes_sum(EXT_W0, N_SPLIT), lanes_sum(EXT_W1, N_SPLIT))
        h = _dot(x, wgu_s[...])
        hg = h[:, 0:f]
        hu = h[:, f:]
        a = hg * jax.nn.sigmoid(hg) * hu * w
        obuf[slot, 0:m_rows, :] = _dot(a.astype(BF16), wd_s[...]).astype(BF16)
        if m_rows < EXPERT_TILE:
            obuf[slot, m_rows:, :] = jnp.zeros((EXPERT_TILE - m_rows, d), BF16)

    def tile(k, carry):
        slot = k % 2

        @pl.when(k + 1 < n)
        def _():
            in_copy(k + 1, 1 - slot).start()

        in_copy(k, slot).wait()

        @pl.when(k >= 2)
        def _():
            out_copy(k - 2, slot).wait()

        parts_used = jnp.minimum(
            (valid_ref[e] - k * EXPERT_TILE + PART_ROWS - 1) // PART_ROWS, EXPERT_TILE // PART_ROWS)
        for parts in range(1, EXPERT_TILE // PART_ROWS + 1):
            @pl.when(parts_used == parts)
            def _():
                ffn(slot, parts * PART_ROWS)

        out_copy(k, slot).start()
        return carry

    lax.fori_loop(0, n, tile, 0)

    for back in (2, 1):
        @pl.when(n >= back)
        def _():
            out_copy(n - back, (n - back) % 2).wait()

    nxt = jnp.minimum(e + 1, N_EXPERTS - 1)

    @pl.when((e + 1 < N_EXPERTS) & (count_ref[nxt] > 0))
    def _():
        in_copy(0, 0, first_ref[nxt]).start()

    for_each_unused(_wait)


def _experts(xs, first_tile, tile_count, valid_rows, n_used, w_gate, w_up, w_down):
    n_rows, width = xs.shape
    n_exp, d, f = w_gate.shape
    te = EXPERT_TILE
    weight = lambda e, *_: (e, 0, 0)
    return pl.pallas_call(
        _expert_kernel,
        grid_spec=pltpu.PrefetchScalarGridSpec(
            num_scalar_prefetch=4,
            grid=(n_exp,),
            in_specs=[
                pl.BlockSpec((1, d, f), weight),
                pl.BlockSpec((1, d, f), weight),
                pl.BlockSpec((1, f, d), weight),
                pl.BlockSpec(memory_space=pl.ANY),
            ],
            out_specs=pl.BlockSpec(memory_space=pl.ANY),
            scratch_shapes=[
                pltpu.VMEM((2, te, width), BF16),
                pltpu.VMEM((2, te, d), BF16),
                pltpu.VMEM((te, d), BF16),
                pltpu.VMEM((d, 2 * f), BF16), pltpu.VMEM((f, d), BF16),
                pltpu.SemaphoreType.DMA((2,)),
                pltpu.SemaphoreType.DMA((2,)),
                pltpu.SemaphoreType.DMA(()),
            ],
        ),
        out_shape=jax.ShapeDtypeStruct((n_rows, d), BF16),
        compiler_params=pltpu.CompilerParams(
            dimension_semantics=("arbitrary",), vmem_limit_bytes=VMEM_LIMIT),
        name="experts",
    )(first_tile, tile_count, valid_rows, n_used, w_gate, w_up, w_down, xs)


def _compact(valid, local, hbm, k_max):
    pos = jnp.cumsum(valid, axis=1) - valid
    k = jnp.arange(k_max, dtype=jnp.int32)[None, :, None]
    hit = (pos[:, None, :] == k) & (valid[:, None, :] == 1)
    pick = lambda a: jnp.sum(jnp.where(hit, a[:, None, :], 0), axis=2).reshape(-1)
    return pick(local), pick(hbm), jnp.sum(valid, axis=1)


def _copy_lists(n_units, lrow, grow):
    tables = ()
    for units, k_max in _copy_classes():
        if units == COPY_UNITS:
            per_seg = ROW_TILE // COPY_ROWS
            c = jnp.arange(per_seg, dtype=jnp.int32)[None, None, :]
            valid = (c < (n_units // COPY_UNITS)[:, :, None]).astype(jnp.int32)
            off = jnp.broadcast_to(c * COPY_UNITS, valid.shape)
        else:
            valid = ((n_units // units) % 2)[:, :, None]
            off = ((n_units // (2 * units)) * (2 * units))[:, :, None]
        flat = lambda a: a.reshape(a.shape[0], -1)
        tables += _compact(flat(valid), flat(lrow[:, :, None] + off), flat(grow[:, :, None] + off),
                           k_max)
    return tables


def _moe_layout(cnt, n_pairs):
    c = cnt[:, 0, :N_EXPERTS].astype(jnp.int32)
    n_tiles = c.shape[0]
    n8 = (c + SEG_ALIGN - 1) // SEG_ALIGN
    lrow = jnp.cumsum(n8, axis=1) - n8
    units_e = jnp.sum(n8, axis=0)
    per_tile = EXPERT_TILE // SEG_ALIGN
    tiles_e = (units_e + per_tile - 1) // per_tile
    e_end = jnp.cumsum(tiles_e)
    e_off = (e_end - tiles_e) * per_tile
    grow = e_off[None, :] + jnp.cumsum(n8, axis=0) - n8
    tail0 = e_off + units_e
    tailn = tiles_e * per_tile - units_e
    worst = n_pairs + n_tiles * N_EXPERTS * (SEG_ALIGN - 1) + N_EXPERTS * (EXPERT_TILE - SEG_ALIGN)
    n_rows = -(-worst // EXPERT_TILE) * EXPERT_TILE
    seg_tables = _copy_lists(n8, lrow, grow) + (jnp.sum(n8, axis=1),)
    expert_tables = (e_end - tiles_e, tiles_e, units_e * SEG_ALIGN, e_end[-1:])
    return seg_tables, (tail0, tailn), expert_tables, n_rows


def _combine_kernel(nt, *refs):
    copy_lists, tot_ref, rest = _group_lists(refs)
    (h_hbm, slots_ref, p_ref, g_ref, wg_ref, wp_ref, gf_ref, os_hbm, o_ref,
     buf, hbuf, sems, h_sems) = rest
    i = pl.program_id(0)
    slot = i % 2
    h_slot = _ring_fetch(h_hbm, hbuf, h_sems, i, nt)

    def seg_copy(slot_):
        def make(l, g, rows):
            return pltpu.make_async_copy(os_hbm.at[pl.ds(g, rows), :],
                                         buf.at[slot_, pl.ds(l, rows), :], sems.at[slot_])
        return make

    @pl.when(i == 0)
    def _():
        buf[...] = jnp.zeros_like(buf)
        _for_each_piece(copy_lists, 0, seg_copy(0), _start)

    @pl.when(i + 1 < nt)
    def _():
        _for_each_piece(copy_lists, i + 1, seg_copy(1 - slot), _start)

    _wait_rows(tot_ref[i], seg_copy(slot))

    def finish(rows):
        slots = slots_ref[...]
        perm = _permutation(slots[:, 0:1], slots[:, 1:2], slots_axis=1, n_slots=rows)
        h = hbuf[h_slot] + _dot(perm, buf[slot, 0:rows, :])
        gate = jax.nn.sigmoid(_dot(_rms_norm(h, g_ref[...]).astype(BF16), wg_ref[...]))
        h = h + gate * _dot(p_ref[...].astype(BF16), wp_ref[...])
        o_ref[...] = _rms_norm(h, gf_ref[...])

    row_options = _used_row_options()
    used = _round_up_to_option(tot_ref[i] * SEG_ALIGN, row_options)
    for rows in row_options:
        @pl.when(used == rows)
        def _():
            finish(rows)


def _combine(h1, slots, seg_tables, o_sorted, p2, g_ple, w_gate, w_proj, g_final):
    n, d = h1.shape
    tm = ROW_TILE
    row = lambda i, *_: (i, 0)
    full2 = lambda i, *_: (0, 0)
    return pl.pallas_call(
        functools.partial(_combine_kernel, n // tm),
        grid_spec=pltpu.PrefetchScalarGridSpec(
            num_scalar_prefetch=len(seg_tables),
            grid=(n // tm,),
            in_specs=[
                pl.BlockSpec(memory_space=pl.ANY),
                pl.BlockSpec((tm, LANES), row),
                pl.BlockSpec((tm, p2.shape[1]), row),
                pl.BlockSpec((1, d), full2),
                pl.BlockSpec(w_gate.shape, full2),
                pl.BlockSpec(w_proj.shape, full2),
                pl.BlockSpec((1, d), full2),
                pl.BlockSpec(memory_space=pl.ANY),
            ],
            out_specs=pl.BlockSpec((tm, d), row),
            scratch_shapes=[
                pltpu.VMEM((2, LOCAL_ROWS, d), BF16),
                pltpu.VMEM((X_RING, tm, d), F32),
                pltpu.SemaphoreType.DMA((2,)),
                pltpu.SemaphoreType.DMA((X_RING,)),
            ],
        ),
        out_shape=jax.ShapeDtypeStruct((n, d), F32),
        compiler_params=pltpu.CompilerParams(
            dimension_semantics=("arbitrary",), vmem_limit_bytes=VMEM_LIMIT),
        name="combine",
    )(*seg_tables, h1, slots, p2, g_ple, w_gate, w_proj, g_final, o_sorted)


def _pad_lanes(a):
    return jnp.pad(a, ((0, 0), (0, LANES - a.shape[1])))


def kernel(x, p, g_mix, w_in, b_f, w_pool, s_pool, w_out, g_ffn, w_grp, b_grp, w_rt, b_rt,
           w_e_gate, w_e_up, w_e_down, g_ple, w_ple_gate, w_ple_proj, g_final):
    batch, seq_len, d = x.shape
    n = batch * seq_len
    assert w_in.shape[0] == 1, "single-layer stack only: the final norm is fused into the layer"
    i = 0
    pool_dim = s_pool.shape[1]
    attn_dim = N_HEADS * HEAD_DIM
    main = pool_dim + 3 * attn_dim
    h = x.reshape(n, d)
    w_f = _pad_lanes(jnp.repeat(w_in[i, :, main:], N_SPLIT, axis=1))
    w_main = jnp.concatenate([w_in[i, :, :main], w_f], axis=1).astype(BF16)
    b_f3 = _pad_lanes(jnp.repeat(b_f[i], N_SPLIT)[None])
    pool, q, k, v = _inproj(h, g_mix[i][None], w_main, b_f3,
                            w_pool[i].astype(BF16), s_pool[i][None], seq_len)
    attn = _attention(q, k, v, batch, seq_len)
    w_router = _pad_lanes(jnp.concatenate([w_grp[i], w_rt[i]], axis=1))
    b_router = _pad_lanes(jnp.concatenate([b_grp[i], b_rt[i]])[None])
    h1, m, ext, slots, slots_t, cnt = _mix(h, pool, attn, w_out[i].astype(BF16), g_ffn[i][None],
                                          w_router, b_router)
    seg_tables, tail_tables, expert_tables, n_rows = _moe_layout(cnt, 2 * n)
    x_sorted = _dispatch(m, ext, slots_t, seg_tables + tail_tables + expert_tables[3:], n_rows)
    f = w_e_gate.shape[-1]
    o_sorted = _experts(x_sorted, *expert_tables,
                        w_e_gate[i].reshape(N_EXPERTS, d, f),
                        w_e_up[i].reshape(N_EXPERTS, d, f),
                        w_e_down[i].reshape(N_EXPERTS, f, d))
    out = _combine(h1, slots, seg_tables, o_sorted, p[i].reshape(n, -1), g_ple[i][None],
                   w_ple_gate[i].astype(BF16), w_ple_proj[i].astype(BF16), g_final[None])
    return out.reshape(batch, seq_len, d)
```

```python
import functools
import math

import jax
import jax.numpy as jnp
from jax import lax
from jax.experimental import pallas as pl
from jax.experimental.pallas import tpu as pltpu

HEAD_DIM = 64
N_HEADS = 8
POOL_WINDOWS = (2, 4, 8, 16)
POOL_GROUP_DIM = 128
POOL_HISTORY = 16
N_GROUPS = 4
EXPERTS_PER_GROUP = 8
N_EXPERTS = N_GROUPS * EXPERTS_PER_GROUP
EPS = 1e-6
LANES = 128
ROUTER_LANE0 = N_GROUPS
NEG_INF = float("-inf")
LOG2E = math.log2(math.e)
N_SPLIT = 3

ROW_TILE = 512
ATTN_Q = 512
VMEM_LIMIT = 48 * 1024 * 1024

BF16 = jnp.bfloat16
F32 = jnp.float32


def _dot(a, b):
    return jnp.dot(a, b, preferred_element_type=F32)


def _split2(a):
    hi = a.astype(BF16)
    lo = (a - hi.astype(F32)).astype(BF16)
    return hi, lo


def _split3(a):
    hi = a.astype(BF16)
    r = a - hi.astype(F32)
    mid = r.astype(BF16)
    lo = (r - mid.astype(F32)).astype(BF16)
    return hi, mid, lo


def _dot_precise(a, w):
    a1, a2 = _split2(a)
    w1, w2 = _split2(w)
    n = w.shape[1]
    r = _dot(a1, jnp.concatenate([w1, w2], axis=1))
    return r[:, 0:n] + (r[:, n:] + _dot(a2, w1))


def _rms_norm(x, g):
    return x * lax.rsqrt(jnp.mean(x * x, axis=-1, keepdims=True) + EPS) * g


def _own_half_start(head):
    return 0 if head % 2 == 0 else HEAD_DIM


def _inproj_kernel(tiles_per_seq, x_ref, g_ref, w_ref, bf_ref, wp_ref, sp_ref,
                   pool_ref, q_ref, k_ref, v_ref,
                   carry_c, carry_u, u_s, q_s, k_s, v_s, c_s):
    i = pl.program_id(0)
    tm = x_ref.shape[0]
    pool_dim = pool_ref.shape[1]
    attn_dim = N_HEADS * HEAD_DIM

    @pl.when(i == 0)
    def _():
        for ref in (u_s, q_s, k_s, v_s, c_s, carry_u):
            ref[...] = jnp.zeros_like(ref)

    @pl.when(i % tiles_per_seq == 0)
    def _():
        carry_c[...] = jnp.zeros_like(carry_c)

    prev_seq_tile = (i + tiles_per_seq - 1) % tiles_per_seq

    @pl.when(prev_seq_tile == 0)
    def _():
        carry_u[...] = jnp.zeros_like(carry_u)

    u_prev, c_prev = u_s[...], c_s[...]
    q_prev, k_prev, v_prev = q_s[...], k_s[...], v_s[...]

    a = _rms_norm(x_ref[...], g_ref[...])
    ab = a.astype(BF16)
    o0 = pool_dim
    u = _dot(ab, w_ref[:, 0:o0])
    qf = _dot(ab, w_ref[:, o0:o0 + attn_dim]) * (LOG2E * HEAD_DIM ** -0.5)
    kf = _dot(ab, w_ref[:, o0 + attn_dim:o0 + 2 * attn_dim])
    vf = _dot(ab, w_ref[:, o0 + 2 * attn_dim:o0 + 3 * attn_dim])
    fl = _dot(ab, w_ref[:, o0 + 3 * attn_dim:]) + bf_ref[...]

    c1, c2, c3 = _split3(c_prev * LOG2E)
    lane = lax.broadcasted_iota(jnp.int32, (1, LANES), 1)
    piece_id = lane % N_SPLIT
    pieces = jnp.where(piece_id == 0, c1, jnp.where(piece_id == 1, c2, c3)).astype(F32)
    for h in range(N_HEADS):
        own = (lane >= _own_half_start(h)) & (lane < _own_half_start(h) + HEAD_DIM)
        pair = slice((h // 2) * LANES, (h // 2 + 1) * LANES)
        blk = slice(h * LANES, (h + 1) * LANES)
        spare = HEAD_DIM - _own_half_start(h)
        first = (lane >= spare) & (lane < spare + N_SPLIT)
        second = (lane >= spare + N_SPLIT) & (lane < spare + 2 * N_SPLIT)
        to_first = pltpu.roll(pieces, (spare - h * N_SPLIT) % LANES, axis=1)
        to_second = pltpu.roll(pieces, (spare + N_SPLIT - h * N_SPLIT) % LANES, axis=1)
        bias_q = jnp.where(first, to_first, jnp.where(second, 1.0, 0.0))
        bias_k = jnp.where(second, -to_second, jnp.where(first, 1.0, 0.0))
        q_ref[:, blk] = jnp.where(own, q_prev[:, pair], bias_q).astype(BF16)
        k_ref[0, blk, :] = jnp.where(own, k_prev[:, pair], bias_k).T.astype(BF16)
        one_col = (lane == HEAD_DIM - _own_half_start(h)).astype(F32)
        v_ref[:, blk] = jnp.where(own, v_prev[:, pair], one_col).astype(BF16)

    ext = jnp.concatenate([carry_u[...], u_prev], axis=0)
    carry_u[...] = u_prev[tm - POOL_HISTORY:, :]
    pos = (prev_seq_tile * tm + 1 + lax.broadcasted_iota(jnp.int32, (tm, 1), 0)).astype(F32)
    for gi, w in enumerate(POOL_WINDOWS):
        lo, hi = gi * POOL_GROUP_DIM, (gi + 1) * POOL_GROUP_DIM
        s = ext[:, lo:hi]
        shift = 1
        while shift < w:
            s = s + pltpu.roll(s, shift, axis=0)
            shift *= 2
        mean = s[POOL_HISTORY:, :] / jnp.minimum(pos, float(w))
        d = mean - u_prev[:, lo:hi]
        y = _dot(d.astype(BF16), wp_ref[gi]) * sp_ref[:, lo:hi]
        pool_ref[:, lo:hi] = y.astype(BF16)

    lf = jnp.minimum(fl, 0.0) - jnp.log1p(jnp.exp(-jnp.abs(fl)))
    row = lax.broadcasted_iota(jnp.int32, (tm, tm), 0)
    col = lax.broadcasted_iota(jnp.int32, (tm, tm), 1)
    tril = (col <= row).astype(BF16)
    sums = _dot(tril, jnp.concatenate(_split3(lf), axis=1))
    c = carry_c[...] + (sums[:, 0:LANES] + (sums[:, LANES:2 * LANES] + sums[:, 2 * LANES:]))
    carry_c[...] = c[tm - 1:tm, :]
    u_s[...] = u
    q_s[...] = qf
    k_s[...] = kf
    v_s[...] = vf
    c_s[...] = c


def _inproj(x2, g_mix, w_main, b_f, w_pool, s_pool, seq_len):
    n, d = x2.shape
    tm = ROW_TILE
    pool_dim = w_pool.shape[0] * w_pool.shape[1]
    head_w = N_HEADS * LANES
    attn_dim = N_HEADS * HEAD_DIM
    n_tiles = n // tm
    projected = lambda i: (jnp.minimum(i, n_tiles - 1), 0)
    finished = lambda i: (jnp.maximum(i - 1, 0), 0)
    tiles_per_seq = seq_len // tm

    def finished_keys(i):
        j = jnp.maximum(i - 1, 0)
        return (j // tiles_per_seq, 0, j % tiles_per_seq)

    full2 = lambda i: (0, 0)
    return pl.pallas_call(
        functools.partial(_inproj_kernel, seq_len // tm),
        grid=(n_tiles + 1,),
        in_specs=[
            pl.BlockSpec((tm, d), projected),
            pl.BlockSpec((1, d), full2),
            pl.BlockSpec(w_main.shape, full2),
            pl.BlockSpec((1, LANES), full2),
            pl.BlockSpec(w_pool.shape, lambda i: (0, 0, 0)),
            pl.BlockSpec((1, pool_dim), full2),
        ],
        out_specs=[
            pl.BlockSpec((tm, pool_dim), finished),
            pl.BlockSpec((tm, head_w), finished),
            pl.BlockSpec((1, head_w, tm), finished_keys),
            pl.BlockSpec((tm, head_w), finished),
        ],
        out_shape=[
            jax.ShapeDtypeStruct((n, pool_dim), BF16),
            jax.ShapeDtypeStruct((n, head_w), BF16),
            jax.ShapeDtypeStruct((n // seq_len, head_w, seq_len), BF16),
            jax.ShapeDtypeStruct((n, head_w), BF16),
        ],
        scratch_shapes=[
            pltpu.VMEM((1, LANES), F32), pltpu.VMEM((POOL_HISTORY, pool_dim), F32),
            pltpu.VMEM((tm, pool_dim), F32), pltpu.VMEM((tm, attn_dim), F32),
            pltpu.VMEM((tm, attn_dim), F32), pltpu.VMEM((tm, attn_dim), F32),
            pltpu.VMEM((tm, LANES), F32),
        ],
        compiler_params=pltpu.CompilerParams(
            dimension_semantics=("arbitrary",), vmem_limit_bytes=VMEM_LIMIT),
        name="inproj",
    )(x2, g_mix, w_main, b_f, w_pool, s_pool)


def _attend_pair(q_ref, k_ref, v_ref, n_keys):
    tq = q_ref.shape[0]
    past = n_keys - tq
    blks = [slice(hh * LANES, (hh + 1) * LANES) for hh in range(2)]
    row = lax.broadcasted_iota(jnp.int32, (tq, tq), 0)
    col = lax.broadcasted_iota(jnp.int32, (tq, tq), 1)
    qs = [q_ref[:, blk] for blk in blks]
    s_diag = [jnp.where(col <= row, _dot(q, k_ref[0, blk, past:n_keys]), NEG_INF)
              for q, blk in zip(qs, blks)]
    m = [jnp.max(s, axis=-1, keepdims=True) for s in s_diag]
    if past:
        s_past = [_dot(q, k_ref[0, blk, 0:past]) for q, blk in zip(qs, blks)]
        m = [jnp.maximum(mi, jnp.max(s, axis=-1, keepdims=True)) for mi, s in zip(m, s_past)]
    acc = [_dot(jnp.exp2(s - mi).astype(BF16), v_ref[past:n_keys, blk])
           for s, mi, blk in zip(s_diag, m, blks)]
    if past:
        acc = [a + _dot(jnp.exp2(s - mi).astype(BF16), v_ref[0:past, blk])
               for a, s, mi, blk in zip(acc, s_past, m, blks)]
    return acc


def _attn_kernel(q_ref, k_ref, v_ref, o_ref):
    qi = pl.program_id(2)
    tq = q_ref.shape[0]
    lane = lax.broadcasted_iota(jnp.int32, (1, LANES), 1)
    for tile in range(v_ref.shape[0] // tq):
        @pl.when(qi == tile)
        def _():
            outs = []
            for hh, acc in enumerate(_attend_pair(q_ref, k_ref, v_ref, (tile + 1) * tq)):
                sum_lane = HEAD_DIM - _own_half_start(hh)
                l = jnp.sum(jnp.where(lane == sum_lane, acc, 0.0), axis=-1, keepdims=True)
                outs.append(acc / l)
            o_ref[...] = jnp.where(lane < HEAD_DIM, outs[0], outs[1]).astype(BF16)


def _attention(q, k, v, batch, seq_len):
    n = q.shape[0]
    pairs = N_HEADS // 2
    tq = ATTN_Q
    qt = seq_len // tq
    pair_block = lambda b, hp, qi: (b, hp)
    return pl.pallas_call(
        _attn_kernel,
        grid=(batch, pairs, qt),
        in_specs=[
            pl.BlockSpec((tq, 2 * LANES), lambda b, hp, qi: (b * qt + qi, hp)),
            pl.BlockSpec((1, 2 * LANES, seq_len), lambda b, hp, qi: (b, hp, 0)),
            pl.BlockSpec((seq_len, 2 * LANES), pair_block),
        ],
        out_specs=pl.BlockSpec((tq, LANES), lambda b, hp, qi: (b * qt + qi, hp)),
        out_shape=jax.ShapeDtypeStruct((n, pairs * LANES), BF16),
        compiler_params=pltpu.CompilerParams(
            dimension_semantics=("arbitrary", "arbitrary", "arbitrary"),
            vmem_limit_bytes=VMEM_LIMIT),
        name="attn",
    )(q, k, v)


def _route(logits):
    lane = lax.broadcasted_iota(jnp.int32, logits.shape, 1)
    big = jnp.int32(LANES)
    gl = jnp.where(lane < N_GROUPS, logits, NEG_INF)
    gmax = jnp.max(gl, axis=-1, keepdims=True)
    g_w = 1.0 / jnp.sum(jnp.exp(gl - gmax), axis=-1, keepdims=True)
    g_idx = jnp.min(jnp.where(gl == gmax, lane, big), axis=-1, keepdims=True)
    e_lo = ROUTER_LANE0 + EXPERTS_PER_GROUP * g_idx
    el = jnp.where((lane >= e_lo) & (lane < e_lo + EXPERTS_PER_GROUP), logits, NEG_INF)
    v1 = jnp.max(el, axis=-1, keepdims=True)
    i1 = jnp.min(jnp.where(el == v1, lane, big), axis=-1, keepdims=True)
    el2 = jnp.where(lane == i1, NEG_INF, el)
    v2 = jnp.max(el2, axis=-1, keepdims=True)
    i2 = jnp.min(jnp.where(el2 == v2, lane, big), axis=-1, keepdims=True)
    e2 = jnp.exp(v2 - v1)
    w1 = g_w / (1.0 + e2)
    w2 = g_w * e2 / (1.0 + e2)
    return (i1 - ROUTER_LANE0).astype(F32), (i2 - ROUTER_LANE0).astype(F32), w1, w2


EXT_W0 = 0
EXT_W1 = N_SPLIT
EXT_E0 = 2 * N_SPLIT


def _routing_record(e0, e1, w0, w1):
    lane = lax.broadcasted_iota(jnp.int32, (1, LANES), 1)
    rec = jnp.where(lane == EXT_E0, e0, jnp.where(lane == EXT_E0 + 1, e1, 0.0))
    for first, w in ((EXT_W0, w0), (EXT_W1, w1)):
        for k, piece in enumerate(_split3(w)):
            rec = jnp.where(lane == first + k, piece.astype(F32), rec)
    return rec.astype(BF16)


def _local_slots(e0, e1):
    t = e0.shape[0]
    lane = lax.broadcasted_iota(jnp.int32, (t, LANES), 1).astype(F32)
    oh0 = lane == e0
    oh1 = lane == e1
    picked = (oh0 | oh1).astype(BF16)
    cnt = jnp.sum(picked.astype(F32), axis=0, keepdims=True)
    units = jnp.floor((cnt + (SEG_ALIGN - 1)) * (1.0 / SEG_ALIGN))
    r128 = lax.broadcasted_iota(jnp.int32, (LANES, LANES), 0)
    c128 = lax.broadcasted_iota(jnp.int32, (LANES, LANES), 1)
    before = (r128 < c128).astype(BF16)
    lstart = SEG_ALIGN * _dot(jnp.broadcast_to(units, (8, LANES)).astype(BF16), before)[0:1, :]
    row = lax.broadcasted_iota(jnp.int32, (t, t), 0)
    col = lax.broadcasted_iota(jnp.int32, (t, t), 1)
    earlier = (col < row).astype(BF16)
    base = _dot(earlier, picked) + lstart
    slot0 = jnp.sum(jnp.where(oh0, base, 0.0), axis=-1, keepdims=True)
    slot1 = jnp.sum(jnp.where(oh1, base, 0.0), axis=-1, keepdims=True)
    return slot0, slot1, cnt


X_RING = 3


def _ring_fetch(hbm, buf, sems, i, n_tiles):
    tm = buf.shape[1]

    def copy(tile):
        slot = tile % X_RING
        rows = pl.ds(pl.multiple_of(tile * tm, tm), tm)
        return pltpu.make_async_copy(hbm.at[rows, :], buf.at[slot], sems.at[slot])

    @pl.when(i == 0)
    def _():
        for tile in range(X_RING - 1):
            copy(tile).start()

    @pl.when(i + X_RING - 1 < n_tiles)
    def _():
        copy(i + X_RING - 1).start()

    @pl.when(i < n_tiles)
    def _():
        copy(i).wait()

    return jnp.minimum(i, n_tiles - 1) % X_RING


def _mix_kernel(n_tiles, x_hbm, pool_ref, attn_ref, wo_ref, g_ref, wr_ref, br_ref,
                h_ref, m_ref, ext_ref, slots_ref, slots_t_ref, cnt_ref, logits_s, xbuf, x_sems):
    i = pl.program_id(0)
    pool_dim = pool_ref.shape[1]

    @pl.when(i == 0)
    def _():
        logits_s[...] = jnp.zeros_like(logits_s)

    prev_logits = logits_s[...]
    x = xbuf[_ring_fetch(x_hbm, xbuf, x_sems, i, n_tiles)]
    h = x + (_dot(pool_ref[...], wo_ref[0:pool_dim, :])
             + _dot(attn_ref[...], wo_ref[pool_dim:, :]))
    h_ref[...] = h
    e0, e1, w0, w1 = _route(prev_logits)
    ext_ref[...] = _routing_record(e0, e1, w0, w1)
    slot0, slot1, cnt = _local_slots(e0, e1)
    cnt_ref[0] = cnt
    lane = lax.broadcasted_iota(jnp.int32, (1, LANES), 1)
    slots = jnp.where(lane == 0, slot0, jnp.where(lane == 1, slot1, 0.0))
    slots_ref[...] = slots
    slots_t_ref[0] = slots.T[0:8, :]
    m = _rms_norm(h, g_ref[...])
    m_ref[...] = m.astype(BF16)
    logits_s[...] = _dot_precise(m, wr_ref[...]) + br_ref[...]


def _mix(x2, pool, attn, w_out, g_ffn, w_router, b_router):
    n, d = x2.shape
    tm = ROW_TILE
    n_tiles = n // tm
    row = lambda i: (jnp.minimum(i, n_tiles - 1), 0)
    routed = lambda i: (jnp.maximum(i - 1, 0), 0)
    full2 = lambda i: (0, 0)
    assert n_tiles >= X_RING
    return pl.pallas_call(
        functools.partial(_mix_kernel, n_tiles),
        grid=(n_tiles + 1,),
        in_specs=[
            pl.BlockSpec(memory_space=pl.ANY),
            pl.BlockSpec((tm, pool.shape[1]), row),
            pl.BlockSpec((tm, attn.shape[1]), row),
            pl.BlockSpec(w_out.shape, full2),
            pl.BlockSpec((1, d), full2),
            pl.BlockSpec(w_router.shape, full2),
            pl.BlockSpec((1, LANES), full2),
        ],
        out_specs=[
            pl.BlockSpec((tm, d), row),
            pl.BlockSpec((tm, d), row),
            pl.BlockSpec((tm, LANES), routed),
            pl.BlockSpec((tm, LANES), routed),
            pl.BlockSpec((1, 8, tm), lambda i: (jnp.maximum(i - 1, 0), 0, 0)),
            pl.BlockSpec((1, 1, LANES), lambda i: (jnp.maximum(i - 1, 0), 0, 0)),
        ],
        out_shape=[
            jax.ShapeDtypeStruct((n, d), F32),
            jax.ShapeDtypeStruct((n, d), BF16),
            jax.ShapeDtypeStruct((n, LANES), BF16),
            jax.ShapeDtypeStruct((n, LANES), F32),
            jax.ShapeDtypeStruct((n_tiles, 8, tm), F32),
            jax.ShapeDtypeStruct((n_tiles, 1, LANES), F32),
        ],
        scratch_shapes=[pltpu.VMEM((tm, LANES), F32), pltpu.VMEM((X_RING, tm, d), F32),
                        pltpu.SemaphoreType.DMA((X_RING,))],
        compiler_params=pltpu.CompilerParams(
            dimension_semantics=("arbitrary",), vmem_limit_bytes=VMEM_LIMIT),
        name="mix",
    )(x2, pool, attn, w_out, g_ffn, w_router, b_router)


SEG_ALIGN = 16
COPY_UNITS = 2
COPY_ROWS = COPY_UNITS * SEG_ALIGN
EXPERT_TILE = 768
LOCAL_ROWS = 2 * ROW_TILE + N_EXPERTS * SEG_ALIGN


TRIM_ROWS = 128
TRIM_OPTIONS = 3


def _used_row_options():
    return [LOCAL_ROWS - k * TRIM_ROWS for k in reversed(range(TRIM_OPTIONS))]


def _round_up_to_option(rows, options):
    out = jnp.int32(options[-1])
    for opt in reversed(options[:-1]):
        out = jnp.where(rows <= opt, jnp.int32(opt), out)
    return out


def _permutation(slot0, slot1, slots_axis, n_slots=None):
    n_slots = LOCAL_ROWS if n_slots is None else n_slots
    shape = (1, n_slots) if slots_axis == 1 else (n_slots, 1)
    s = lax.broadcasted_iota(jnp.int32, shape, slots_axis)
    return ((s == slot0.astype(jnp.int32)) | (s == slot1.astype(jnp.int32))).astype(BF16)


def _copy_classes():
    classes = [(COPY_UNITS, LOCAL_ROWS // COPY_ROWS)]
    units = COPY_UNITS // 2
    while units >= 1:
        classes.append((units, N_EXPERTS))
        units //= 2
    return classes


def _for_each_piece(list_refs, tile, make_copy, act):
    for (units, k_max), (local_ref, hbm_ref, count_ref) in zip(_copy_classes(), list_refs):
        def body(k, carry, units=units, k_max=k_max, local_ref=local_ref, hbm_ref=hbm_ref):
            idx = tile * k_max + k
            act(make_copy(pl.multiple_of(local_ref[idx] * SEG_ALIGN, SEG_ALIGN),
                          pl.multiple_of(hbm_ref[idx] * SEG_ALIGN, SEG_ALIGN),
                          units * SEG_ALIGN))
            return carry

        lax.fori_loop(0, count_ref[tile], body, 0)


def _group_lists(refs):
    n = len(_copy_classes())
    return [tuple(refs[3 * c:3 * c + 3]) for c in range(n)], refs[3 * n], refs[3 * n + 1:]


def _wait_rows(total_units, make_copy):
    for b in range((LOCAL_ROWS // SEG_ALIGN).bit_length()):
        @pl.when(((total_units >> b) & 1) == 1)
        def _():
            make_copy(0, 0, SEG_ALIGN << b).wait()


def _start(copy):
    copy.start()


def _wait(copy):
    copy.wait()


def _dispatch_kernel(n_steps, *refs):
    copy_lists, tot_ref, rest = _group_lists(refs)
    (tail0_ref, tailn_ref, nu_ref, m_ref, ext_ref, st_ref, st_next_ref, xs_hbm,
     buf, zeros, perm_even, perm_odd, sems, tail_sem) = rest
    i = pl.program_id(0)
    slot = i % 2
    d = m_ref.shape[1]
    n_expert_tiles = xs_hbm.shape[0] // EXPERT_TILE

    def seg_copy(slot_):
        def make(l, g, rows):
            return pltpu.make_async_copy(buf.at[slot_, pl.ds(l, rows), :],
                                         xs_hbm.at[pl.ds(g, rows), :], sems.at[slot_])
        return make

    def for_each_tail(act):
        @pl.when(i < N_EXPERTS)
        def _():
            e = jnp.minimum(i, N_EXPERTS - 1)
            g0 = tail0_ref[e] * SEG_ALIGN
            n = tailn_ref[e]
            for b in range((EXPERT_TILE // SEG_ALIGN - 1).bit_length()):
                @pl.when(((n >> b) & 1) == 1)
                def _():
                    rows = SEG_ALIGN << b
                    higher = (n >> (b + 1)) << (b + 1)
                    g = pl.multiple_of(g0 + higher * SEG_ALIGN, SEG_ALIGN)
                    act(pltpu.make_async_copy(zeros.at[pl.ds(0, rows), :],
                                              xs_hbm.at[pl.ds(g, rows), :], tail_sem))

        for k in range(-(-n_expert_tiles // n_steps)):
            t = nu_ref[0] + i + k * n_steps

            @pl.when(t < n_expert_tiles)
            def _():
                g = pl.multiple_of(t * EXPERT_TILE, EXPERT_TILE)
                act(pltpu.make_async_copy(zeros, xs_hbm.at[pl.ds(g, EXPERT_TILE), :], tail_sem))

    @pl.when(i == 0)
    def _():
        zeros[...] = jnp.zeros_like(zeros)

    for_each_tail(_start)

    def build(slots_t, perm):
        perm[...] = _permutation(slots_t[0, 0:1, :], slots_t[0, 1:2, :], slots_axis=0)

    @pl.when(i == 0)
    def _():
        build(st_ref, perm_even)

    def permute(perm, next_perm, rows):
        p = perm[0:rows, :]
        build(st_next_ref, next_perm)
        buf[slot, 0:rows, 0:d] = _dot(p, m_ref[...]).astype(BF16)
        buf[slot, 0:rows, d:] = _dot(p, ext_ref[...]).astype(BF16)

    row_options = _used_row_options()
    used = _round_up_to_option(tot_ref[i] * SEG_ALIGN, row_options)
    for parity, (perm, next_perm) in enumerate(((perm_even, perm_odd), (perm_odd, perm_even))):
        for rows in row_options:
            @pl.when((slot == parity) & (used == rows))
            def _():
                permute(perm, next_perm, rows)

    _for_each_piece(copy_lists, i, seg_copy(slot), _start)

    @pl.when(i > 0)
    def _():
        _wait_rows(tot_ref[i - 1], seg_copy(1 - slot))

    for_each_tail(_wait)

    @pl.when(i == n_steps - 1)
    def _():
        _wait_rows(tot_ref[i], seg_copy(slot))


def _dispatch(m, ext, slots_t, tables, n_rows):
    n, d = m.shape
    tm = ROW_TILE
    assert n // tm >= N_EXPERTS, "each grid step zero-fills the tail of one expert"
    width = d + LANES
    last = n // tm - 1
    row = lambda i, *_: (i, 0)
    return pl.pallas_call(
        functools.partial(_dispatch_kernel, n // tm),
        grid_spec=pltpu.PrefetchScalarGridSpec(
            num_scalar_prefetch=len(tables),
            grid=(n // tm,),
            in_specs=[
                pl.BlockSpec((tm, d), row),
                pl.BlockSpec((tm, LANES), row),
                pl.BlockSpec((1, 8, tm), lambda i, *_: (i, 0, 0)),
                pl.BlockSpec((1, 8, tm), lambda i, *_: (jnp.minimum(i + 1, last), 0, 0)),
            ],
            out_specs=pl.BlockSpec(memory_space=pl.ANY),
            scratch_shapes=[
                pltpu.VMEM((2, LOCAL_ROWS, width), BF16),
                pltpu.VMEM((EXPERT_TILE, width), BF16),
                pltpu.VMEM((LOCAL_ROWS, tm), BF16),
                pltpu.VMEM((LOCAL_ROWS, tm), BF16),
                pltpu.SemaphoreType.DMA((2,)),
                pltpu.SemaphoreType.DMA(()),
            ],
        ),
        out_shape=jax.ShapeDtypeStruct((n_rows, width), BF16),
        compiler_params=pltpu.CompilerParams(
            dimension_semantics=("arbitrary",), vmem_limit_bytes=VMEM_LIMIT),
        name="dispatch",
    )(*tables, m, ext, slots_t, slots_t)


PART_ROWS = 64


def _expert_kernel(first_ref, count_ref, valid_ref, nu_ref, wg_ref, wu_ref, wd_ref, xs_hbm, os_hbm,
                   xbuf, obuf, zeros, wgu_s, wd_s, in_sems, out_sems, zero_sem):
    e = pl.program_id(0)
    d = os_hbm.shape[1]
    f = wg_ref.shape[2]
    n = count_ref[e]
    t0 = first_ref[e]
    n_tiles = os_hbm.shape[0] // EXPERT_TILE

    def rows(t):
        return pl.ds(pl.multiple_of(t * EXPERT_TILE, EXPERT_TILE), EXPERT_TILE)

    def in_copy(k, slot, first=t0):
        return pltpu.make_async_copy(xs_hbm.at[rows(first + k), :], xbuf.at[slot],
                                     in_sems.at[slot])

    def out_copy(k, slot):
        return pltpu.make_async_copy(obuf.at[slot], os_hbm.at[rows(t0 + k), :], out_sems.at[slot])

    def for_each_unused(act):
        for j in range(-(-n_tiles // N_EXPERTS)):
            t = nu_ref[0] + e + j * N_EXPERTS

            @pl.when(t < n_tiles)
            def _():
                act(pltpu.make_async_copy(zeros, os_hbm.at[rows(t), :], zero_sem))

    @pl.when(e == 0)
    def _():
        zeros[...] = jnp.zeros_like(zeros)

    for_each_unused(_start)

    @pl.when((e == 0) & (n > 0))
    def _():
        in_copy(0, 0).start()

    @pl.when(n > 0)
    def _():
        wgu_s[:, 0:f] = wg_ref[0].astype(BF16)
        wgu_s[:, f:] = wu_ref[0].astype(BF16)
        wd_s[...] = wd_ref[0].astype(BF16)

    lane = lax.broadcasted_iota(jnp.int32, (1, LANES), 1)

    def ffn(slot, m_rows):
        x = xbuf[slot, 0:m_rows, 0:d]
        rec = xbuf[slot, 0:m_rows, d:].astype(F32)

        def lanes_sum(first, count):
            keep = (lane >= first) & (lane < first + count)
            return jnp.sum(jnp.where(keep, rec, 0.0), axis=-1, keepdims=True)

        first_choice = lanes_sum(EXT_E0, 1) == e.astype(F32)
        w = jnp.where(first_choice, lanes_sum(EXT_W0, N_SPLIT), lanes_sum(EXT_W1, N_SPLIT))
        h = _dot(x, wgu_s[...])
        hg = h[:, 0:f]
        hu = h[:, f:]
        a = hg * jax.nn.sigmoid(hg) * hu * w
        obuf[slot, 0:m_rows, :] = _dot(a.astype(BF16), wd_s[...]).astype(BF16)
        if m_rows < EXPERT_TILE:
            obuf[slot, m_rows:, :] = jnp.zeros((EXPERT_TILE - m_rows, d), BF16)

    def tile(k, carry):
        slot = k % 2

        @pl.when(k + 1 < n)
        def _():
            in_copy(k + 1, 1 - slot).start()

        in_copy(k, slot).wait()

        @pl.when(k >= 2)
        def _():
            out_copy(k - 2, slot).wait()

        parts_used = jnp.minimum(
            (valid_ref[e] - k * EXPERT_TILE + PART_ROWS - 1) // PART_ROWS, EXPERT_TILE // PART_ROWS)
        for parts in range(1, EXPERT_TILE // PART_ROWS + 1):
            @pl.when(parts_used == parts)
            def _():
                ffn(slot, parts * PART_ROWS)

        out_copy(k, slot).start()
        return carry

    lax.fori_loop(0, n, tile, 0)

    for back in (2, 1):
        @pl.when(n >= back)
        def _():
            out_copy(n - back, (n - back) % 2).wait()

    nxt = jnp.minimum(e + 1, N_EXPERTS - 1)

    @pl.when((e + 1 < N_EXPERTS) & (count_ref[nxt] > 0))
    def _():
        in_copy(0, 0, first_ref[nxt]).start()

    for_each_unused(_wait)


def _experts(xs, first_tile, tile_count, valid_rows, n_used, w_gate, w_up, w_down):
    n_rows, width = xs.shape
    n_exp, d, f = w_gate.shape
    te = EXPERT_TILE
    weight = lambda e, *_: (e, 0, 0)
    return pl.pallas_call(
        _expert_kernel,
        grid_spec=pltpu.PrefetchScalarGridSpec(
            num_scalar_prefetch=4,
            grid=(n_exp,),
            in_specs=[
                pl.BlockSpec((1, d, f), weight),
                pl.BlockSpec((1, d, f), weight),
                pl.BlockSpec((1, f, d), weight),
                pl.BlockSpec(memory_space=pl.ANY),
            ],
            out_specs=pl.BlockSpec(memory_space=pl.ANY),
            scratch_shapes=[
                pltpu.VMEM((2, te, width), BF16),
                pltpu.VMEM((2, te, d), BF16),
                pltpu.VMEM((te, d), BF16),
                pltpu.VMEM((d, 2 * f), BF16), pltpu.VMEM((f, d), BF16),
                pltpu.SemaphoreType.DMA((2,)),
                pltpu.SemaphoreType.DMA((2,)),
                pltpu.SemaphoreType.DMA(()),
            ],
        ),
        out_shape=jax.ShapeDtypeStruct((n_rows, d), BF16),
        compiler_params=pltpu.CompilerParams(
            dimension_semantics=("arbitrary",), vmem_limit_bytes=VMEM_LIMIT),
        name="experts",
    )(first_tile, tile_count, valid_rows, n_used, w_gate, w_up, w_down, xs)


def _compact(valid, local, hbm, k_max):
    pos = jnp.cumsum(valid, axis=1) - valid
    k = jnp.arange(k_max, dtype=jnp.int32)[None, :, None]
    hit = (pos[:, None, :] == k) & (valid[:, None, :] == 1)
    pick = lambda a: jnp.sum(jnp.where(hit, a[:, None, :], 0), axis=2).reshape(-1)
    return pick(local), pick(hbm), jnp.sum(valid, axis=1)


def _copy_lists(n_units, lrow, grow):
    tables = ()
    for units, k_max in _copy_classes():
        if units == COPY_UNITS:
            per_seg = ROW_TILE // COPY_ROWS
            c = jnp.arange(per_seg, dtype=jnp.int32)[None, None, :]
            valid = (c < (n_units // COPY_UNITS)[:, :, None]).astype(jnp.int32)
            off = jnp.broadcast_to(c * COPY_UNITS, valid.shape)
        else:
            valid = ((n_units // units) % 2)[:, :, None]
            off = ((n_units // (2 * units)) * (2 * units))[:, :, None]
        flat = lambda a: a.reshape(a.shape[0], -1)
        tables += _compact(flat(valid), flat(lrow[:, :, None] + off), flat(grow[:, :, None] + off),
                           k_max)
    return tables


def _moe_layout(cnt, n_pairs):
    c = cnt[:, 0, :N_EXPERTS].astype(jnp.int32)
    n_tiles = c.shape[0]
    n8 = (c + SEG_ALIGN - 1) // SEG_ALIGN
    lrow = jnp.cumsum(n8, axis=1) - n8
    units_e = jnp.sum(n8, axis=0)
    per_tile = EXPERT_TILE // SEG_ALIGN
    tiles_e = (units_e + per_tile - 1) // per_tile
    e_end = jnp.cumsum(tiles_e)
    e_off = (e_end - tiles_e) * per_tile
    grow = e_off[None, :] + jnp.cumsum(n8, axis=0) - n8
    tail0 = e_off + units_e
    tailn = tiles_e * per_tile - units_e
    worst = n_pairs + n_tiles * N_EXPERTS * (SEG_ALIGN - 1) + N_EXPERTS * (EXPERT_TILE - SEG_ALIGN)
    n_rows = -(-worst // EXPERT_TILE) * EXPERT_TILE
    seg_tables = _copy_lists(n8, lrow, grow) + (jnp.sum(n8, axis=1),)
    expert_tables = (e_end - tiles_e, tiles_e, units_e * SEG_ALIGN, e_end[-1:])
    return seg_tables, (tail0, tailn), expert_tables, n_rows


def _combine_kernel(*refs):
    copy_lists, tot_ref, rest = _group_lists(refs)
    h_ref, slots_ref, p_ref, g_ref, wg_ref, wp_ref, gf_ref, os_hbm, o_ref, buf, sems = rest
    i = pl.program_id(0)
    nt = pl.num_programs(0)
    slot = i % 2

    def seg_copy(slot_):
        def make(l, g, rows):
            return pltpu.make_async_copy(os_hbm.at[pl.ds(g, rows), :],
                                         buf.at[slot_, pl.ds(l, rows), :], sems.at[slot_])
        return make

    @pl.when(i == 0)
    def _():
        buf[...] = jnp.zeros_like(buf)
        _for_each_piece(copy_lists, 0, seg_copy(0), _start)

    @pl.when(i + 1 < nt)
    def _():
        _for_each_piece(copy_lists, i + 1, seg_copy(1 - slot), _start)

    _wait_rows(tot_ref[i], seg_copy(slot))

    def finish(rows):
        slots = slots_ref[...]
        perm = _permutation(slots[:, 0:1], slots[:, 1:2], slots_axis=1, n_slots=rows)
        h = h_ref[...] + _dot(perm, buf[slot, 0:rows, :])
        gate = jax.nn.sigmoid(_dot(_rms_norm(h, g_ref[...]).astype(BF16), wg_ref[...]))
        h = h + gate * _dot(p_ref[...].astype(BF16), wp_ref[...])
        o_ref[...] = _rms_norm(h, gf_ref[...])

    row_options = _used_row_options()
    used = _round_up_to_option(tot_ref[i] * SEG_ALIGN, row_options)
    for rows in row_options:
        @pl.when(used == rows)
        def _():
            finish(rows)


def _combine(h1, slots, seg_tables, o_sorted, p2, g_ple, w_gate, w_proj, g_final):
    n, d = h1.shape
    tm = ROW_TILE
    row = lambda i, *_: (i, 0)
    full2 = lambda i, *_: (0, 0)
    return pl.pallas_call(
        _combine_kernel,
        grid_spec=pltpu.PrefetchScalarGridSpec(
            num_scalar_prefetch=len(seg_tables),
            grid=(n // tm,),
            in_specs=[
                pl.BlockSpec((tm, d), row),
                pl.BlockSpec((tm, LANES), row),
                pl.BlockSpec((tm, p2.shape[1]), row),
                pl.BlockSpec((1, d), full2),
                pl.BlockSpec(w_gate.shape, full2),
                pl.BlockSpec(w_proj.shape, full2),
                pl.BlockSpec((1, d), full2),
                pl.BlockSpec(memory_space=pl.ANY),
            ],
            out_specs=pl.BlockSpec((tm, d), row),
            scratch_shapes=[
                pltpu.VMEM((2, LOCAL_ROWS, d), BF16),
                pltpu.SemaphoreType.DMA((2,)),
            ],
        ),
        out_shape=jax.ShapeDtypeStruct((n, d), F32),
        compiler_params=pltpu.CompilerParams(
            dimension_semantics=("arbitrary",), vmem_limit_bytes=VMEM_LIMIT),
        name="combine",
    )(*seg_tables, h1, slots, p2, g_ple, w_gate, w_proj, g_final, o_sorted)


def _pad_lanes(a):
    return jnp.pad(a, ((0, 0), (0, LANES - a.shape[1])))


def kernel(x, p, g_mix, w_in, b_f, w_pool, s_pool, w_out, g_ffn, w_grp, b_grp, w_rt, b_rt,
           w_e_gate, w_e_up, w_e_down, g_ple, w_ple_gate, w_ple_proj, g_final):
    batch, seq_len, d = x.shape
    n = batch * seq_len
    assert w_in.shape[0] == 1, "single-layer stack only: the final norm is fused into the layer"
    i = 0
    pool_dim = s_pool.shape[1]
    attn_dim = N_HEADS * HEAD_DIM
    main = pool_dim + 3 * attn_dim
    h = x.reshape(n, d)
    w_f = _pad_lanes(jnp.repeat(w_in[i, :, main:], N_SPLIT, axis=1))
    w_main = jnp.concatenate([w_in[i, :, :main], w_f], axis=1).astype(BF16)
    b_f3 = _pad_lanes(jnp.repeat(b_f[i], N_SPLIT)[None])
    pool, q, k, v = _inproj(h, g_mix[i][None], w_main, b_f3,
                            w_pool[i].astype(BF16), s_pool[i][None], seq_len)
    attn = _attention(q, k, v, batch, seq_len)
    w_router = _pad_lanes(jnp.concatenate([w_grp[i], w_rt[i]], axis=1))
    b_router = _pad_lanes(jnp.concatenate([b_grp[i], b_rt[i]])[None])
    h1, m, ext, slots, slots_t, cnt = _mix(h, pool, attn, w_out[i].astype(BF16), g_ffn[i][None],
                                          w_router, b_router)
    seg_tables, tail_tables, expert_tables, n_rows = _moe_layout(cnt, 2 * n)
    x_sorted = _dispatch(m, ext, slots_t, seg_tables + tail_tables + expert_tables[3:], n_rows)
    f = w_e_gate.shape[-1]
    o_sorted = _experts(x_sorted, *expert_tables,
                        w_e_gate[i].reshape(N_EXPERTS, d, f),
                        w_e_up[i].reshape(N_EXPERTS, d, f),
                        w_e_down[i].reshape(N_EXPERTS, f, d))
    out = _combine(h1, slots, seg_tables, o_sorted, p[i].reshape(n, -1), g_ple[i][None],
                   w_ple_gate[i].astype(BF16), w_ple_proj[i].astype(BF16), g_final[None])
    return out.reshape(batch, seq_len, d)
```
